```python
import math
import jax, jax.numpy as jnp
from jax import lax
import numpy as np

D_MODEL = 2048
BATCH = 1
SEQ = 8192
DEPTH = 2
DEC_BATCH = 128
DEC_SEQ = 1
PAST_LEN = 8192
PAGE_SIZE = 128

HEAD_DIM = 128
RET_HEADS = D_MODEL // 2 // HEAD_DIM
RET_DK = HEAD_DIM
RET_DV = HEAD_DIM
RET_W = RET_HEADS * RET_DV
RET_CHUNK = 128
SWA_HEADS = D_MODEL // 2 // HEAD_DIM
SWA_KV_HEADS = 2
SWA_GROUP = SWA_HEADS // SWA_KV_HEADS
SWA_W = SWA_HEADS * HEAD_DIM
SWA_KV_W = SWA_KV_HEADS * HEAD_DIM
WINDOW = 128
MIX_W = RET_W + SWA_W
IN_SIZES = (RET_W, RET_W, RET_W, RET_W, SWA_W, SWA_KV_W, SWA_KV_W)
IN_W = RET_W * 4 + SWA_W + 2 * SWA_KV_W
MEM_LEN = 256
MEM_HEADS = 4
MEM_HD = 128
MEM_W = MEM_HEADS * MEM_HD
D_FF = ((8 * D_MODEL + 3 * 256 - 1) // (3 * 256)) * 256
ROPE_BASE = 10000.0
EPS = 1e-6

kernel_name = "hymba_retention_swa_sink_memxattn_step"


def rmsnorm(x, g):
    xf = x.astype(jnp.float32)
    y = xf * lax.rsqrt(jnp.mean(xf * xf, axis=-1, keepdims=True) + EPS)
    return (y * g.astype(jnp.float32)).astype(x.dtype)


def rotary(x, pos):
    half = x.shape[-1] // 2
    inv = ROPE_BASE ** (-jnp.arange(half, dtype=jnp.float32) / half)
    ang = pos.astype(jnp.float32)[:, None] * inv[None, :]
    cos = jnp.cos(ang)[None, :, None, :]
    sin = jnp.sin(ang)[None, :, None, :]
    xf = x.astype(jnp.float32)
    x1, x2 = xf[..., :half], xf[..., half:]
    return jnp.concatenate([x1 * cos - x2 * sin, x1 * sin + x2 * cos], axis=-1).astype(x.dtype)


def retention_log_decay():
    return jnp.log1p(-jnp.exp2(-5.0 - jnp.arange(RET_HEADS, dtype=jnp.float32)))


def retention(q, k, v, r0):
    B, L, H, DK = q.shape
    DV = v.shape[-1]
    f32 = jnp.float32
    C = RET_CHUNK if L % RET_CHUNK == 0 else L
    nc = L // C
    lg = retention_log_decay()
    idx = jnp.arange(C, dtype=f32)
    diff = idx[:, None] - idx[None, :]
    dmask = jnp.where(diff[None] >= 0, jnp.exp(lg[:, None, None] * jnp.maximum(diff, 0.0)[None]), 0.0)
    xi = jnp.exp(lg[:, None] * (idx + 1.0)[None])
    zeta = jnp.exp(lg[:, None] * (C - 1.0 - idx)[None])
    g_chunk = jnp.exp(lg * C)
    qc = q.astype(f32).reshape(B, nc, C, H, DK)
    kc = k.astype(f32).reshape(B, nc, C, H, DK)
    vc = v.astype(f32).reshape(B, nc, C, H, DV)
    s = jnp.einsum('bnchd,bnmhd->bnhcm', qc, kc) * dmask
    inner = jnp.einsum('bnhcm,bnmhv->bnchv', s, vc)
    kv = jnp.einsum('bnmhd,hm,bnmhv->nbhdv', kc, zeta, vc)

    def step(r, kv_n):
        return g_chunk[None, :, None, None] * r + kv_n, r

    r_last, r_prev = lax.scan(step, r0.astype(f32), kv)
    cross = jnp.einsum('bnchd,hc,nbhdv->bnchv', qc, xi, r_prev)
    out = (inner + cross).reshape(B, L, H, DV)
    return out, r_last


def sink_softmax(s, mask, sink):
    s = jnp.where(mask, s, -jnp.inf)
    sk = jnp.broadcast_to(sink.astype(jnp.float32).reshape(SWA_KV_HEADS, SWA_GROUP, 1, 1), s.shape[:-1] + (1,))
    p = jax.nn.softmax(jnp.concatenate([s, sk], axis=-1), axis=-1)
    return p[..., :-1]


def swa_prompt(q, k, v, sink):
    B, L, _, hd = q.shape
    W = WINDOW
    nb = L // W
    qb = q.reshape(B, nb, W, SWA_KV_HEADS, SWA_GROUP, hd)
    kb = k.reshape(B, nb, W, SWA_KV_HEADS, hd)
    vb = v.reshape(B, nb, W, SWA_KV_HEADS, hd)
    kk = jnp.concatenate([jnp.concatenate([jnp.zeros_like(kb[:, :1]), kb[:, :-1]], axis=1), kb], axis=2)
    vv = jnp.concatenate([jnp.concatenate([jnp.zeros_like(vb[:, :1]), vb[:, :-1]], axis=1), vb], axis=2)
    blk = jnp.arange(nb)[:, None] * W
    qpos = blk + jnp.arange(W)[None, :]
    kpos = blk - W + jnp.arange(2 * W)[None, :]
    d = qpos[:, :, None] - kpos[:, None, :]
    mask = (d >= 0) & (d <= WINDOW) & (kpos[:, None, :] >= 0)
    s = jnp.einsum('bnqkgd,bnskd->bnkgqs', qb, kk).astype(jnp.float32) * (hd ** -0.5)
    p = sink_softmax(s, mask[None, :, None, None], sink)
    o = jnp.einsum('bnkgqs,bnskd->bnqkgd', p.astype(vv.dtype), vv)
    return o.reshape(B, L, SWA_W)


def swa_decode(q, k, v, ck, cv, sink):
    B, T, _, hd = q.shape
    kk = jnp.concatenate([ck.astype(k.dtype), k], axis=1)
    vv = jnp.concatenate([cv.astype(v.dtype), v], axis=1)
    qpos = PAST_LEN + jnp.arange(T)
    kpos = jnp.concatenate([PAST_LEN - WINDOW + jnp.arange(WINDOW), PAST_LEN + jnp.arange(T)])
    d = qpos[:, None] - kpos[None, :]
    mask = (d >= 0) & (d <= WINDOW)
    qg = q.reshape(B, T, SWA_KV_HEADS, SWA_GROUP, hd)
    s = jnp.einsum('btkgd,bskd->bkgts', qg, kk).astype(jnp.float32) * (hd ** -0.5)
    p = sink_softmax(s, mask, sink)
    o = jnp.einsum('bkgts,bskd->btkgd', p.astype(vv.dtype), vv).reshape(B, T, SWA_W)
    return o, kk[:, -WINDOW:], vv[:, -WINDOW:]


def mixer(h, pos, w_in, ret_gn, sink, w_out, r0, ck, cv):
    B, L, _ = h.shape
    proj = h @ w_in
    splits = [int(s) for s in np.cumsum(IN_SIZES)[:-1]]
    rq, rk, rv, rg, sq, sk, sv = jnp.split(proj, splits, axis=-1)
    rq = rotary(rq.reshape(B, L, RET_HEADS, RET_DK), pos)
    rk = rotary(rk.reshape(B, L, RET_HEADS, RET_DK), pos) * (RET_DK ** -0.5)
    rv = rv.reshape(B, L, RET_HEADS, RET_DV)
    if r0 is None:
        r0 = jnp.zeros((B, RET_HEADS, RET_DK, RET_DV), jnp.float32)
    ro, r_new = retention(rq, rk, rv, r0)
    ro = ro * lax.rsqrt(jnp.mean(ro * ro, axis=-1, keepdims=True) + EPS)
    ro = ro.reshape(B, L, RET_W) * ret_gn.astype(jnp.float32)
    ro = (jax.nn.silu(rg.astype(jnp.float32)) * ro).astype(h.dtype)
    sq = sq.reshape(B, L, SWA_HEADS, HEAD_DIM)
    sk = sk.reshape(B, L, SWA_KV_HEADS, HEAD_DIM)
    sv = sv.reshape(B, L, SWA_KV_HEADS, HEAD_DIM)
    if ck is None:
        so = swa_prompt(sq, sk, sv, sink)
        nk, nv = sk[:, -WINDOW:], sv[:, -WINDOW:]
    else:
        so, nk, nv = swa_decode(sq, sk, sv, ck, cv, sink)
    y = jnp.concatenate([ro, so.astype(h.dtype)], axis=-1) @ w_out
    return y, r_new.astype(h.dtype), nk, nv


def cross_attn(h, mk, mv, wq, wo):
    B, L, _ = h.shape
    q = (h @ wq).reshape(B, L, MEM_HEADS, MEM_HD)
    s = jnp.einsum('blhd,bmhd->bhlm', q, mk.astype(q.dtype)).astype(jnp.float32) * (MEM_HD ** -0.5)
    p = jax.nn.softmax(s, axis=-1)
    o = jnp.einsum('bhlm,bmhd->blhd', p.astype(h.dtype), mv.astype(h.dtype)).reshape(B, L, MEM_W)
    return o @ wo


def run_trunk(x, pos, P, mem=None, state_ret=None, cache_win_k=None, cache_win_v=None,
              cache_mem_k=None, cache_mem_v=None):
    rs, ks, vs, mks, mvs = [], [], [], [], []
    for l in range(DEPTH):
        h = rmsnorm(x, P['g_mix'][l])
        r0 = None if state_ret is None else state_ret[l]
        ck = None if cache_win_k is None else cache_win_k[l]
        cv = None if cache_win_v is None else cache_win_v[l]
        y, r_new, nk, nv = mixer(h, pos, P['w_in'][l], P['ret_gn'][l], P['sinks'][l], P['w_out'][l], r0, ck, cv)
        x = x + y
        if mem is not None:
            mn = rmsnorm(mem, P['g_mem'][l])
            B = mem.shape[0]
            mk = (mn @ P['wk_c'][l]).reshape(B, MEM_LEN, MEM_HEADS, MEM_HD)
            mv = (mn @ P['wv_c'][l]).reshape(B, MEM_LEN, MEM_HEADS, MEM_HD)
        else:
            mk, mv = cache_mem_k[l], cache_mem_v[l]
        x = x + cross_attn(rmsnorm(x, P['g_cross'][l]), mk, mv, P['wq_c'][l], P['wo_c'][l])
        h = rmsnorm(x, P['g_ffn'][l])
        x = x + (jax.nn.silu(h @ P['w_gate'][l]) * (h @ P['w_up'][l])) @ P['w_down'][l]
        rs.append(r_new); ks.append(nk); vs.append(nv); mks.append(mk); mvs.append(mv)
    x = rmsnorm(x, P['g_final'])
    return x, rs, ks, vs, mks, mvs


def setup_inputs(seed: int = 0) -> dict:
    key = jax.random.key(seed)
    ks = jax.random.split(key, 32)
    f32 = jnp.float32
    nrm = lambda k, shape, scale: jax.random.normal(k, shape, f32) * scale
    gain = lambda k, shape: 1.0 + 0.02 * jax.random.normal(k, shape, f32)
    return {
        "x_prompt": nrm(ks[0], (BATCH, SEQ, D_MODEL), 1.0),
        "x_sample": nrm(ks[1], (DEC_BATCH, DEC_SEQ, D_MODEL), 1.0),
        "mem_prompt": nrm(ks[2], (BATCH, MEM_LEN, D_MODEL), 1.0),
        "state_ret": nrm(ks[3], (DEPTH, DEC_BATCH, RET_HEADS, RET_DK, RET_DV), 0.5),
        "cache_win_k": nrm(ks[4], (DEPTH, DEC_BATCH, WINDOW, SWA_KV_HEADS, HEAD_DIM), 1.0),
        "cache_win_v": nrm(ks[5], (DEPTH, DEC_BATCH, WINDOW, SWA_KV_HEADS, HEAD_DIM), 1.0),
        "cache_mem_k": nrm(ks[6], (DEPTH, DEC_BATCH, MEM_LEN, MEM_HEADS, MEM_HD), 1.0),
        "cache_mem_v": nrm(ks[7], (DEPTH, DEC_BATCH, MEM_LEN, MEM_HEADS, MEM_HD), 1.0),
        "g_mix": gain(ks[8], (DEPTH, D_MODEL)),
        "w_in": nrm(ks[9], (DEPTH, D_MODEL, IN_W), D_MODEL ** -0.5),
        "ret_gn": gain(ks[10], (DEPTH, RET_W)),
        "sinks": nrm(ks[11], (DEPTH, SWA_HEADS), 0.5),
        "w_out": nrm(ks[12], (DEPTH, MIX_W, D_MODEL), MIX_W ** -0.5),
        "g_cross": gain(ks[13], (DEPTH, D_MODEL)),
        "g_mem": gain(ks[14], (DEPTH, D_MODEL)),
        "wq_c": nrm(ks[15], (DEPTH, D_MODEL, MEM_W), D_MODEL ** -0.5),
        "wk_c": nrm(ks[16], (DEPTH, D_MODEL, MEM_W), D_MODEL ** -0.5),
        "wv_c": nrm(ks[17], (DEPTH, D_MODEL, MEM_W), D_MODEL ** -0.5),
        "wo_c": nrm(ks[18], (DEPTH, MEM_W, D_MODEL), MEM_W ** -0.5),
        "g_ffn": gain(ks[19], (DEPTH, D_MODEL)),
        "w_gate": nrm(ks[20], (DEPTH, D_MODEL, D_FF), D_MODEL ** -0.5),
        "w_up": nrm(ks[21], (DEPTH, D_MODEL, D_FF), D_MODEL ** -0.5),
        "w_down": nrm(ks[22], (DEPTH, D_FF, D_MODEL), D_FF ** -0.5),
        "g_final": gain(ks[23], (D_MODEL,)),
    }


def reference(x_prompt, x_sample, mem_prompt, state_ret, cache_win_k, cache_win_v, cache_mem_k, cache_mem_v,
              g_mix, w_in, ret_gn, sinks, w_out, g_cross, g_mem, wq_c, wk_c, wv_c, wo_c,
              g_ffn, w_gate, w_up, w_down, g_final):
    P = dict(g_mix=g_mix, w_in=w_in, ret_gn=ret_gn, sinks=sinks, w_out=w_out, g_cross=g_cross,
             g_mem=g_mem, wq_c=wq_c, wk_c=wk_c, wv_c=wv_c, wo_c=wo_c, g_ffn=g_ffn,
             w_gate=w_gate, w_up=w_up, w_down=w_down, g_final=g_final)
    pos_p = jnp.arange(x_prompt.shape[1], dtype=jnp.int32)
    y_prompt, rp, kp, vp, mkp, mvp = run_trunk(x_prompt, pos_p, P, mem=mem_prompt)
    pos_s = PAST_LEN + jnp.arange(x_sample.shape[1], dtype=jnp.int32)
    y_sample, rs, ks_, vs_, _, _ = run_trunk(x_sample, pos_s, P, state_ret=state_ret,
                                             cache_win_k=cache_win_k, cache_win_v=cache_win_v,
                                             cache_mem_k=cache_mem_k, cache_mem_v=cache_mem_v)
    return (y_prompt, y_sample,
            jnp.stack(rp), jnp.stack(kp), jnp.stack(vp), jnp.stack(mkp), jnp.stack(mvp),
            jnp.stack(rs), jnp.stack(ks_), jnp.stack(vs_))
```

```python
import functools

import numpy as np
import jax
import jax.numpy as jnp
from jax import lax
from jax.experimental import pallas as pl
from jax.experimental.pallas import tpu as pltpu

D_MODEL = 2048
DEPTH = 2
PAST_LEN = 8192
HEAD_DIM = 128
RET_HEADS = 8
RET_W = RET_HEADS * HEAD_DIM
RET_CHUNK = 128
SWA_HEADS = 8
SWA_KV_HEADS = 2
SWA_GROUP = SWA_HEADS // SWA_KV_HEADS
SWA_W = SWA_HEADS * HEAD_DIM
WINDOW = 128
MEM_LEN = 256
MEM_HEADS = 4
MEM_W = MEM_HEADS * HEAD_DIM
IN_W = 4 * RET_W + SWA_W + 2 * SWA_KV_HEADS * HEAD_DIM
D_FF = 5632
ROPE_BASE = 10000.0
EPS = 1e-6
ATT_SCALE = HEAD_DIM ** -0.5

COL_RQ, COL_RK, COL_RV, COL_RG = 0, 8, 16, 24
COL_SQ, COL_SK, COL_SV = 32, 40, 42

V7X_VMEM_BYTES = 64 * 1024 * 1024
V7X_VMEM_BUDGET = V7X_VMEM_BYTES - 8 * 1024 * 1024
LANES = 128

BF16 = jnp.bfloat16
F32 = jnp.float32


def _vmem_limit(block_bytes, scratch_bytes=0, temp_bytes=0):
    need = 2 * block_bytes + scratch_bytes + temp_bytes + 4 * 1024 * 1024
    return int(min(max(need, 16 * 1024 * 1024), V7X_VMEM_BUDGET))


def _nbytes(shape, dtype):
    return int(np.prod(shape)) * jnp.dtype(dtype).itemsize


def _params(semantics, block_bytes, scratch_bytes=0, temp_bytes=0):
    return pltpu.CompilerParams(
        dimension_semantics=semantics,
        vmem_limit_bytes=_vmem_limit(block_bytes, scratch_bytes, temp_bytes))


def _rms_rows(x, g):
    ms = jnp.mean(x * x, axis=-1, keepdims=True)
    return x * lax.rsqrt(ms + EPS) * g


def _rms_to(dst_ref, src_ref, g_ref, rows):
    g = g_ref[...]
    chunk = min(rows, 128)

    def body(c, carry):
        r = pl.multiple_of(c * chunk, chunk)
        x = src_ref[pl.ds(r, chunk), :]
        dst_ref[pl.ds(r, chunk), :] = _rms_rows(x, g).astype(dst_ref.dtype)
        return carry

    lax.fori_loop(0, rows // chunk, body, 0)


def _sigmoid(x):
    return 1.0 / (1.0 + jnp.exp(-x))


def _dot(a, b):
    return jnp.dot(a, b, preferred_element_type=F32)


def _dot_nt(a, b):
    return lax.dot_general(a, b, (((1,), (1,)), ((), ())), preferred_element_type=F32)


def _in_proj_kernel(x_ref, g_ref, w_ref, o_ref, h_ref, *, tm):
    @pl.when(pl.program_id(1) == 0)
    def _():
        _rms_to(h_ref, x_ref, g_ref, tm)

    o_ref[...] = _dot(h_ref[...], w_ref[...])


def _in_proj(x, g, w):
    m = x.shape[0]
    tm = min(m, 1024)
    tn = 512
    blocks = (_nbytes((tm, D_MODEL), F32) + _nbytes((D_MODEL, tn), BF16) + _nbytes((tm, tn), F32))
    return pl.pallas_call(
        functools.partial(_in_proj_kernel, tm=tm),
        out_shape=jax.ShapeDtypeStruct((m, IN_W), F32),
        grid=(m // tm, IN_W // tn),
        in_specs=[
            pl.BlockSpec((tm, D_MODEL), lambda i, j: (i, 0)),
            pl.BlockSpec((1, D_MODEL), lambda i, j: (0, 0)),
            pl.BlockSpec((D_MODEL, tn), lambda i, j: (0, j)),
        ],
        out_specs=pl.BlockSpec((tm, tn), lambda i, j: (i, j)),
        scratch_shapes=[pltpu.VMEM((tm, D_MODEL), BF16)],
        compiler_params=_params(("parallel", "arbitrary"), blocks, _nbytes((tm, D_MODEL), BF16)),
        name="in_proj",
    )(x, g.reshape(1, D_MODEL), w)


def _out_proj_kernel(ro_ref, so_ref, x_ref, wa_ref, wb_ref, gc_ref, wq_ref, x1_ref, q_ref, h_ref, *, tm):
    y = _dot(ro_ref[...].astype(BF16), wa_ref[...]) + _dot(so_ref[...].astype(BF16), wb_ref[...])
    x1_ref[...] = x_ref[...] + y
    _rms_to(h_ref, x1_ref, gc_ref, tm)
    q_ref[...] = _dot(h_ref[...], wq_ref[...])


def _out_proj(ro, so, x, w_out, g_cross, wq):
    m = x.shape[0]
    tm = min(m, 512)
    blocks = (_nbytes((tm, RET_W), ro.dtype) + _nbytes((tm, SWA_W), so.dtype)
              + 2 * _nbytes((tm, D_MODEL), F32) + _nbytes((D_MODEL, D_MODEL), BF16)
              + _nbytes((D_MODEL, MEM_W), BF16) + _nbytes((tm, MEM_W), F32))
    return pl.pallas_call(
        functools.partial(_out_proj_kernel, tm=tm),
        out_shape=(jax.ShapeDtypeStruct((m, D_MODEL), F32), jax.ShapeDtypeStruct((m, MEM_W), F32)),
        grid=(m // tm,),
        in_specs=[
            pl.BlockSpec((tm, RET_W), lambda i: (i, 0)),
            pl.BlockSpec((tm, SWA_W), lambda i: (i, 0)),
            pl.BlockSpec((tm, D_MODEL), lambda i: (i, 0)),
            pl.BlockSpec((RET_W, D_MODEL), lambda i: (0, 0)),
            pl.BlockSpec((SWA_W, D_MODEL), lambda i: (1, 0)),
            pl.BlockSpec((1, D_MODEL), lambda i: (0, 0)),
            pl.BlockSpec((D_MODEL, MEM_W), lambda i: (0, 0)),
        ],
        out_specs=(pl.BlockSpec((tm, D_MODEL), lambda i: (i, 0)),
                   pl.BlockSpec((tm, MEM_W), lambda i: (i, 0))),
        scratch_shapes=[pltpu.VMEM((tm, D_MODEL), BF16)],
        compiler_params=_params(("parallel",), blocks, _nbytes((tm, D_MODEL), BF16),
                                _nbytes((tm, D_MODEL), F32)),
        name="out_proj",
    )(ro, so, x, w_out, w_out, g_cross.reshape(1, D_MODEL), wq)


def _ffn_kernel(x_ref, g_ref, wg_ref, wu_ref, wd_ref, gf_ref, o_ref, h_ref, acc_ref, *, tm, nf, final_norm):
    j = pl.program_id(1)

    @pl.when(j == 0)
    def _():
        _rms_to(h_ref, x_ref, g_ref, tm)
        acc_ref[...] = jnp.zeros_like(acc_ref)

    h = h_ref[...]
    a = _dot(h, wg_ref[...])
    u = _dot(h, wu_ref[...])
    act = (a * _sigmoid(a) * u).astype(BF16)
    acc_ref[...] += _dot(act, wd_ref[...])

    @pl.when(j == nf - 1)
    def _():
        o_ref[...] = x_ref[...] + acc_ref[...]
        if final_norm:
            _rms_to(o_ref, o_ref, gf_ref, tm)


def _ffn(x, g, wg, wu, wd, g_final, final_norm):
    m = x.shape[0]
    tm = min(m, 512)
    tf = 512
    nf = D_FF // tf
    blocks = (2 * _nbytes((tm, D_MODEL), F32) + 3 * _nbytes((D_MODEL, tf), BF16))
    scratch = _nbytes((tm, D_MODEL), BF16) + _nbytes((tm, D_MODEL), F32)
    return pl.pallas_call(
        functools.partial(_ffn_kernel, tm=tm, nf=nf, final_norm=final_norm),
        out_shape=jax.ShapeDtypeStruct((m, D_MODEL), F32),
        grid=(m // tm, nf),
        in_specs=[
            pl.BlockSpec((tm, D_MODEL), lambda i, j: (i, 0)),
            pl.BlockSpec((1, D_MODEL), lambda i, j: (0, 0)),
            pl.BlockSpec((D_MODEL, tf), lambda i, j: (0, j)),
            pl.BlockSpec((D_MODEL, tf), lambda i, j: (0, j)),
            pl.BlockSpec((tf, D_MODEL), lambda i, j: (j, 0)),
            pl.BlockSpec((1, D_MODEL), lambda i, j: (0, 0)),
        ],
        out_specs=pl.BlockSpec((tm, D_MODEL), lambda i, j: (i, 0)),
        scratch_shapes=[pltpu.VMEM((tm, D_MODEL), BF16), pltpu.VMEM((tm, D_MODEL), F32)],
        compiler_params=_params(("parallel", "arbitrary"), blocks, scratch,
                                3 * _nbytes((tm, tf), F32) + _nbytes((tm, D_MODEL), F32)),
        name="ffn",
    )(x, g.reshape(1, D_MODEL), wg, wu, wd, g_final.reshape(1, D_MODEL))


def _matmul_residual_kernel(a_ref, w_ref, x_ref, o_ref):
    o_ref[...] = x_ref[...] + _dot(a_ref[...].astype(BF16), w_ref[...])


def _matmul_residual(a, w, x):
    m, k = a.shape
    n = w.shape[1]
    blocks = _nbytes((m, k), a.dtype) + _nbytes((k, n), BF16) + 2 * _nbytes((m, n), F32)
    return pl.pallas_call(
        _matmul_residual_kernel,
        out_shape=jax.ShapeDtypeStruct((m, n), F32),
        grid=(1,),
        in_specs=[pl.BlockSpec((m, k), lambda i: (0, 0)),
                  pl.BlockSpec((k, n), lambda i: (0, 0)),
                  pl.BlockSpec((m, n), lambda i: (0, 0))],
        out_specs=pl.BlockSpec((m, n), lambda i: (0, 0)),
        compiler_params=_params(("arbitrary",), blocks),
        name="matmul_residual",
    )(a, w, x)


def _rope_tables(positions):
    half = HEAD_DIM // 2
    inv = ROPE_BASE ** (-np.arange(half, dtype=np.float64) / half)
    ang = np.asarray(positions, np.float64)[:, None] * inv[None, :]
    cos, sin = np.cos(ang), np.sin(ang)
    c = np.concatenate([cos, cos], axis=-1).astype(np.float32)
    s = np.concatenate([-sin, sin], axis=-1).astype(np.float32)
    return c, s


def _decay_tables():
    c = RET_CHUNK
    lg = np.log1p(-np.exp2(-5.0 - np.arange(RET_HEADS, dtype=np.float64)))
    idx = np.arange(c, dtype=np.float64)
    diff = idx[:, None] - idx[None, :]
    dmask = np.where(diff[None] >= 0, np.exp(lg[:, None, None] * np.maximum(diff, 0.0)[None]), 0.0)
    xi = np.exp(lg[:, None] * (idx + 1.0)[None])
    zeta = np.exp(lg[:, None] * (c - 1.0 - idx)[None])
    bcast = lambda t: np.broadcast_to(t[:, :, None], (RET_HEADS, c, LANES)).astype(np.float32)
    return dmask.astype(np.float32), bcast(xi), bcast(zeta)


def _rotary(x, c, s):
    return x * c + pltpu.roll(x, HEAD_DIM // 2, 1) * s


def _ret_prompt_kernel(q_ref, k_ref, v_ref, g_ref, c_ref, s_ref, dm_ref, xi_ref, zt_ref, gn_ref,
                       o_ref, st_ref, *, nchunk):
    @pl.when(pl.program_id(1) == 0)
    def _():
        st_ref[...] = jnp.zeros_like(st_ref)

    c_tab = c_ref[...]
    s_tab = s_ref[...]
    qr = _rotary(q_ref[...], c_tab, s_tab)
    kr = _rotary(k_ref[...], c_tab, s_tab) * ATT_SCALE
    dmask = dm_ref[...]
    xi = xi_ref[...]
    zeta = zt_ref[...]
    g_chunk = xi[RET_CHUNK - 1:RET_CHUNK, :]
    gn = gn_ref[...]
    state = st_ref[...]
    for c in range(nchunk):
        rows = slice(c * RET_CHUNK, (c + 1) * RET_CHUNK)
        qc = qr[rows]
        kc = kr[rows]
        vc = v_ref[rows, :].astype(BF16)
        s = _dot_nt(qc.astype(BF16), kc.astype(BF16)) * dmask
        inner = _dot(s.astype(BF16), vc)
        cross = _dot((qc * xi).astype(BF16), state.astype(BF16))
        kv = _dot((kc * zeta).T.astype(BF16), vc)
        state = g_chunk * state + kv
        out = inner + cross
        out = out * lax.rsqrt(jnp.mean(out * out, axis=-1, keepdims=True) + EPS) * gn
        gate = g_ref[rows, :]
        o_ref[rows, :] = (gate * _sigmoid(gate) * out).astype(o_ref.dtype)
    st_ref[...] = state


def _ret_prompt(proj, c_tab, s_tab, dmask, xi, zeta, ret_gn):
    m = proj.shape[0]
    r = 1024
    nchunk = r // RET_CHUNK
    col = lambda off: pl.BlockSpec((r, HEAD_DIM), lambda h, i: (i, off + h))
    head_tab = pl.BlockSpec((None, RET_CHUNK, LANES), lambda h, i: (h, 0, 0))
    blocks = (6 * _nbytes((r, HEAD_DIM), F32) + 3 * _nbytes((RET_CHUNK, LANES), F32)
              + _nbytes((r, HEAD_DIM), BF16) + _nbytes((HEAD_DIM, HEAD_DIM), F32))
    return pl.pallas_call(
        functools.partial(_ret_prompt_kernel, nchunk=nchunk),
        out_shape=(jax.ShapeDtypeStruct((m, RET_W), BF16),
                   jax.ShapeDtypeStruct((RET_HEADS, HEAD_DIM, HEAD_DIM), F32)),
        grid=(RET_HEADS, m // r),
        in_specs=[col(COL_RQ), col(COL_RK), col(COL_RV), col(COL_RG),
                  pl.BlockSpec((r, HEAD_DIM), lambda h, i: (i, 0)),
                  pl.BlockSpec((r, HEAD_DIM), lambda h, i: (i, 0)),
                  head_tab, head_tab, head_tab,
                  pl.BlockSpec((1, HEAD_DIM), lambda h, i: (0, h))],
        out_specs=(pl.BlockSpec((r, HEAD_DIM), lambda h, i: (i, h)),
                   pl.BlockSpec((None, HEAD_DIM, HEAD_DIM), lambda h, i: (h, 0, 0))),
        compiler_params=_params(("parallel", "arbitrary"), blocks, 0, 4 * _nbytes((r, HEAD_DIM), F32)),
        name="ret_prompt",
    )(proj, proj, proj, proj, c_tab, s_tab, dmask, xi, zeta, ret_gn.reshape(1, RET_W))


def _swa_prompt_kernel(sink_ref, q_ref, k_ref, v_ref, kp_ref, vp_ref, o_ref, *, nblk):
    kvh = pl.program_id(0)
    i = pl.program_id(1)
    kfull = jnp.concatenate([kp_ref[...], k_ref[...]], axis=0).astype(BF16)
    vfull = jnp.concatenate([vp_ref[...], v_ref[...]], axis=0).astype(BF16)
    row = lax.broadcasted_iota(jnp.int32, (WINDOW, 2 * WINDOW), 0)
    col = lax.broadcasted_iota(jnp.int32, (WINDOW, 2 * WINDOW), 1)
    band = (col >= row) & (col <= row + WINDOW)
    for j in range(nblk):
        rows = slice(j * WINDOW, (j + 1) * WINDOW)
        kk = kfull[j * WINDOW:(j + 2) * WINDOW]
        vv = vfull[j * WINDOW:(j + 2) * WINDOW]
        if j == 0:
            mask = band & (col >= jnp.where(i > 0, 0, WINDOW))
        else:
            mask = band
        qj = q_ref[rows, :].astype(BF16)
        qs = jnp.concatenate([qj[:, h * HEAD_DIM:(h + 1) * HEAD_DIM] for h in range(SWA_GROUP)], axis=0)
        s = _dot_nt(qs, kk) * ATT_SCALE
        ps = []
        for h in range(SWA_GROUP):
            sh = jnp.where(mask, s[h * WINDOW:(h + 1) * WINDOW], -jnp.inf)
            sink = sink_ref[kvh * SWA_GROUP + h]
            mx = jnp.maximum(jnp.max(sh, axis=-1, keepdims=True), sink)
            e = jnp.exp(sh - mx)
            den = jnp.sum(e, axis=-1, keepdims=True) + jnp.exp(sink - mx)
            ps.append((e / den).astype(BF16))
        o = _dot(jnp.concatenate(ps, axis=0), vv)
        for h in range(SWA_GROUP):
            o_ref[rows, h * HEAD_DIM:(h + 1) * HEAD_DIM] = o[h * WINDOW:(h + 1) * WINDOW].astype(o_ref.dtype)


def _swa_prompt(proj, sinks):
    m = proj.shape[0]
    r = 512
    nblk = r // WINDOW
    gw = SWA_GROUP * HEAD_DIM
    prev = lambda off: pl.BlockSpec(
        (WINDOW, HEAD_DIM), lambda kv, i: (jnp.maximum(i * nblk - 1, 0), off + kv))
    blocks = (_nbytes((r, gw), F32) + 2 * _nbytes((r, HEAD_DIM), F32) + 2 * _nbytes((WINDOW, HEAD_DIM), F32)
              + _nbytes((r, gw), BF16))
    return pl.pallas_call(
        functools.partial(_swa_prompt_kernel, nblk=nblk),
        out_shape=jax.ShapeDtypeStruct((m, SWA_W), BF16),
        grid=(SWA_KV_HEADS, m // r),
        in_specs=[pl.BlockSpec(memory_space=pltpu.SMEM),
                  pl.BlockSpec((r, gw), lambda kv, i: (i, COL_SQ // SWA_GROUP + kv)),
                  pl.BlockSpec((r, HEAD_DIM), lambda kv, i: (i, COL_SK + kv)),
                  pl.BlockSpec((r, HEAD_DIM), lambda kv, i: (i, COL_SV + kv)),
                  prev(COL_SK), prev(COL_SV)],
        out_specs=pl.BlockSpec((r, gw), lambda kv, i: (i, kv)),
        compiler_params=_params(("parallel", "parallel"), blocks, 0, 8 * _nbytes((gw, 2 * WINDOW), F32)),
        name="swa_prompt",
    )(sinks, proj, proj, proj, proj, proj)


def _mem_kv_kernel(m_ref, g_ref, wk_ref, wv_ref, k_ref, v_ref):
    h = _rms_rows(m_ref[...], g_ref[...]).astype(BF16)
    k_ref[...] = _dot(h, wk_ref[...])
    v_ref[...] = _dot(h, wv_ref[...])


def _mem_kv(mem, g, wk, wv):
    full = lambda shape: pl.BlockSpec(shape, lambda i: (0,) * len(shape))
    blocks = (_nbytes((MEM_LEN, D_MODEL), F32) + 2 * _nbytes((D_MODEL, MEM_W), BF16)
              + 2 * _nbytes((MEM_LEN, MEM_W), F32))
    return pl.pallas_call(
        _mem_kv_kernel,
        out_shape=(jax.ShapeDtypeStruct((MEM_LEN, MEM_W), F32),) * 2,
        grid=(1,),
        in_specs=[full((MEM_LEN, D_MODEL)), full((1, D_MODEL)), full((D_MODEL, MEM_W)), full((D_MODEL, MEM_W))],
        out_specs=(full((MEM_LEN, MEM_W)),) * 2,
        compiler_params=_params(("arbitrary",), blocks, 0, _nbytes((MEM_LEN, D_MODEL), F32)),
        name="mem_kv",
    )(mem, g.reshape(1, D_MODEL), wk, wv)


def _cross_prompt_kernel(q_ref, mk_ref, mv_ref, wo_ref, x_ref, o_ref):
    q = q_ref[...].astype(BF16)
    mk = mk_ref[...].astype(BF16)
    mv = mv_ref[...].astype(BF16)
    outs = []
    for h in range(MEM_HEADS):
        cols = slice(h * HEAD_DIM, (h + 1) * HEAD_DIM)
        s = _dot_nt(q[:, cols], mk[:, cols]) * ATT_SCALE
        e = jnp.exp(s - jnp.max(s, axis=-1, keepdims=True))
        p = e / jnp.sum(e, axis=-1, keepdims=True)
        outs.append(_dot(p.astype(BF16), mv[:, cols]).astype(BF16))
    o_ref[...] = x_ref[...] + _dot(jnp.concatenate(outs, axis=1), wo_ref[...])


def _cross_prompt(q, mk, mv, wo, x):
    m = x.shape[0]
    tm = 512
    blocks = (_nbytes((tm, MEM_W), F32) + 2 * _nbytes((MEM_LEN, MEM_W), F32) + _nbytes((MEM_W, D_MODEL), BF16)
              + 2 * _nbytes((tm, D_MODEL), F32))
    return pl.pallas_call(
        _cross_prompt_kernel,
        out_shape=jax.ShapeDtypeStruct((m, D_MODEL), F32),
        grid=(m // tm,),
        in_specs=[pl.BlockSpec((tm, MEM_W), lambda i: (i, 0)),
                  pl.BlockSpec((MEM_LEN, MEM_W), lambda i: (0, 0)),
                  pl.BlockSpec((MEM_LEN, MEM_W), lambda i: (0, 0)),
                  pl.BlockSpec((MEM_W, D_MODEL), lambda i: (0, 0)),
                  pl.BlockSpec((tm, D_MODEL), lambda i: (i, 0))],
        out_specs=pl.BlockSpec((tm, D_MODEL), lambda i: (i, 0)),
        compiler_params=_params(("parallel",), blocks, 0, 2 * _nbytes((tm, D_MODEL), F32)),
        name="cross_prompt",
    )(q, mk, mv, wo, x)


def _ret_sample_kernel(gam_ref, q_ref, k_ref, v_ref, g_ref, c_ref, s_ref, gn_ref, st_ref,
                       o_ref, nst_ref, qt_ref, kt_ref, *, bt):
    h = pl.program_id(0)
    t = pl.program_id(1)

    @pl.when(t == 0)
    def _():
        c_tab = c_ref[...]
        s_tab = s_ref[...]
        qt_ref[...] = _rotary(q_ref[...], c_tab, s_tab).T
        kt_ref[...] = (_rotary(k_ref[...], c_tab, s_tab) * ATT_SCALE).T

    gamma = gam_ref[h]
    gn = gn_ref[...]
    lane = lax.broadcasted_iota(jnp.int32, (HEAD_DIM, LANES), 1)
    ones = jnp.ones((LANES, LANES), BF16)

    def body(ib, carry):
        b = t * bt + ib
        sel = lane == b
        k_col = _dot(jnp.where(sel, kt_ref[...], 0.0).astype(BF16), ones)
        q_col = _dot(jnp.where(sel, qt_ref[...], 0.0).astype(BF16), ones)
        v_row = v_ref[pl.ds(b, 1), :]
        new = gamma * st_ref[ib] + k_col * v_row
        nst_ref[ib] = new
        out = jnp.sum(q_col * new, axis=0, keepdims=True)
        out = out * lax.rsqrt(jnp.mean(out * out, axis=-1, keepdims=True) + EPS) * gn
        gate = g_ref[pl.ds(b, 1), :]
        o_ref[pl.ds(b, 1), :] = gate * _sigmoid(gate) * out
        return carry

    lax.fori_loop(0, bt, body, 0)


def _ret_sample(proj, state, gammas, c_tab, s_tab, ret_gn):
    b = proj.shape[0]
    bt = 16
    col = lambda off: pl.BlockSpec((b, HEAD_DIM), lambda h, t: (0, off + h))
    row = pl.BlockSpec((1, HEAD_DIM), lambda h, t: (0, 0))
    st_spec = pl.BlockSpec((bt, None, HEAD_DIM, HEAD_DIM), lambda h, t: (t, h, 0, 0))
    blocks = 5 * _nbytes((b, HEAD_DIM), F32) + 2 * _nbytes((bt, HEAD_DIM, HEAD_DIM), F32)
    return pl.pallas_call(
        functools.partial(_ret_sample_kernel, bt=bt),
        out_shape=(jax.ShapeDtypeStruct((b, RET_W), F32), jax.ShapeDtypeStruct(state.shape, F32)),
        grid=(RET_HEADS, b // bt),
        in_specs=[pl.BlockSpec(memory_space=pltpu.SMEM),
                  col(COL_RQ), col(COL_RK), col(COL_RV), col(COL_RG), row, row,
                  pl.BlockSpec((1, HEAD_DIM), lambda h, t: (0, h)),
                  st_spec],
        out_specs=(pl.BlockSpec((b, HEAD_DIM), lambda h, t: (0, h)), st_spec),
        scratch_shapes=[pltpu.VMEM((HEAD_DIM, b), F32), pltpu.VMEM((HEAD_DIM, b), F32)],
        compiler_params=_params(("parallel", "arbitrary"), blocks, 2 * _nbytes((HEAD_DIM, b), F32)),
        name="ret_sample",
    )(gammas, proj, proj, proj, proj, c_tab, s_tab, ret_gn.reshape(1, RET_W), state)


def _swa_sample_kernel(sink_ref, q_ref, kn_ref, vn_ref, ck_ref, cv_ref, o_ref, nk_ref, nv_ref, *, bt):
    sink = sink_ref[...]
    row = lax.broadcasted_iota(jnp.int32, (SWA_HEADS, HEAD_DIM), 0)
    first_group = row < SWA_GROUP

    def body(ib, carry):
        q = q_ref[ib]
        kn = kn_ref[ib]
        vn = vn_ref[ib]
        qb = q.astype(BF16)
        k_new = jnp.where(first_group, kn[0:1, :], kn[1:2, :])
        v_new = jnp.where(first_group, vn[0:1, :], vn[1:2, :])
        s_new = jnp.sum(q * k_new, axis=-1, keepdims=True) * ATT_SCALE
        ck = ck_ref[ib]
        cv = cv_ref[ib]
        s0 = _dot_nt(qb, ck[:, :HEAD_DIM].astype(BF16))
        s1 = _dot_nt(qb, ck[:, HEAD_DIM:].astype(BF16))
        s = jnp.where(first_group, s0, s1) * ATT_SCALE
        mx = jnp.maximum(jnp.maximum(jnp.max(s, axis=-1, keepdims=True), s_new), sink)
        e = jnp.exp(s - mx)
        e_new = jnp.exp(s_new - mx)
        den = jnp.sum(e, axis=-1, keepdims=True) + e_new + jnp.exp(sink - mx)
        p = (e / den).astype(BF16)
        o0 = _dot(p, cv[:, :HEAD_DIM].astype(BF16))
        o1 = _dot(p, cv[:, HEAD_DIM:].astype(BF16))
        o_ref[ib] = jnp.where(first_group, o0, o1) + (e_new / den) * v_new
        nk_ref[ib, 0:WINDOW - 1, :] = ck[1:WINDOW, :]
        nv_ref[ib, 0:WINDOW - 1, :] = cv[1:WINDOW, :]
        nk_ref[ib, WINDOW - 1:WINDOW, :] = jnp.concatenate([kn[0:1, :], kn[1:2, :]], axis=1)
        nv_ref[ib, WINDOW - 1:WINDOW, :] = jnp.concatenate([vn[0:1, :], vn[1:2, :]], axis=1)
        return carry

    lax.fori_loop(0, bt, body, 0)


def _swa_sample(q, k_new, v_new, cache_k, cache_v, sinks):
    b = q.shape[0]
    bt = 16
    kvw = SWA_KV_HEADS * HEAD_DIM
    blocks = (2 * _nbytes((bt, SWA_HEADS, HEAD_DIM), F32) + 2 * _nbytes((bt, 8, HEAD_DIM), F32)
              + 4 * _nbytes((bt, WINDOW, kvw), F32))
    cache_spec = pl.BlockSpec((bt, WINDOW, kvw), lambda t: (t, 0, 0))
    new_spec = pl.BlockSpec((bt, SWA_KV_HEADS, HEAD_DIM), lambda t: (t, 0, 0))
    head_spec = pl.BlockSpec((bt, SWA_HEADS, HEAD_DIM), lambda t: (t, 0, 0))
    return pl.pallas_call(
        functools.partial(_swa_sample_kernel, bt=bt),
        out_shape=(jax.ShapeDtypeStruct((b, SWA_HEADS, HEAD_DIM), F32),
                   jax.ShapeDtypeStruct(cache_k.shape, F32), jax.ShapeDtypeStruct(cache_v.shape, F32)),
        grid=(b // bt,),
        in_specs=[pl.BlockSpec((SWA_HEADS, 1), lambda t: (0, 0)),
                  head_spec, new_spec, new_spec, cache_spec, cache_spec],
        out_specs=(head_spec, cache_spec, cache_spec),
        compiler_params=_params(("parallel",), blocks),
        name="swa_sample",
    )(sinks.reshape(SWA_HEADS, 1), q, k_new, v_new, cache_k, cache_v)


def _cross_sample_kernel(q_ref, mk_ref, mv_ref, o_ref, *, bt):
    row = lax.broadcasted_iota(jnp.int32, (MEM_HEADS, MEM_LEN), 0)
    orow = lax.broadcasted_iota(jnp.int32, (MEM_HEADS, HEAD_DIM), 0)

    def body(ib, carry):
        qb = q_ref[ib].astype(BF16)
        mk = mk_ref[ib]
        mv = mv_ref[ib]
        s = jnp.zeros((MEM_HEADS, MEM_LEN), F32)
        for h in range(MEM_HEADS):
            cols = slice(h * HEAD_DIM, (h + 1) * HEAD_DIM)
            s = jnp.where(row == h, _dot_nt(qb, mk[:, cols].astype(BF16)), s)
        s = s * ATT_SCALE
        e = jnp.exp(s - jnp.max(s, axis=-1, keepdims=True))
        p = (e / jnp.sum(e, axis=-1, keepdims=True)).astype(BF16)
        o = jnp.zeros((MEM_HEADS, HEAD_DIM), F32)
        for h in range(MEM_HEADS):
            cols = slice(h * HEAD_DIM, (h + 1) * HEAD_DIM)
            o = jnp.where(orow == h, _dot(p, mv[:, cols].astype(BF16)), o)
        o_ref[ib] = o
        return carry

    lax.fori_loop(0, bt, body, 0)


def _cross_sample(q, cache_k, cache_v):
    b = q.shape[0]
    bt = 8
    blocks = 2 * _nbytes((bt, 8, HEAD_DIM), F32) + 2 * _nbytes((bt, MEM_LEN, MEM_W), F32)
    cache_spec = pl.BlockSpec((bt, MEM_LEN, MEM_W), lambda t: (t, 0, 0))
    head_spec = pl.BlockSpec((bt, MEM_HEADS, HEAD_DIM), lambda t: (t, 0, 0))
    return pl.pallas_call(
        functools.partial(_cross_sample_kernel, bt=bt),
        out_shape=jax.ShapeDtypeStruct((b, MEM_HEADS, HEAD_DIM), F32),
        grid=(b // bt,),
        in_specs=[head_spec, cache_spec, cache_spec],
        out_specs=head_spec,
        compiler_params=_params(("parallel",), blocks),
        name="cross_sample",
    )(q, cache_k, cache_v)


def kernel(x_prompt, x_sample, mem_prompt, state_ret, cache_win_k, cache_win_v, cache_mem_k, cache_mem_v,
           g_mix, w_in, ret_gn, sinks, w_out, g_cross, g_mem, wq_c, wk_c, wv_c, wo_c,
           g_ffn, w_gate, w_up, w_down, g_final):
    seq = x_prompt.shape[1]
    nb = x_sample.shape[0]
    w_in_b, w_out_b = w_in.astype(BF16), w_out.astype(BF16)
    wq_b, wk_b, wv_b, wo_b = (w.astype(BF16) for w in (wq_c, wk_c, wv_c, wo_c))
    wg_b, wu_b, wd_b = (w.astype(BF16) for w in (w_gate, w_up, w_down))

    cp, sp = (jnp.asarray(t) for t in _rope_tables(np.arange(seq)))
    cs, ss = (jnp.asarray(t) for t in _rope_tables(np.array([PAST_LEN])))
    dmask, xi, zeta = (jnp.asarray(t) for t in _decay_tables())
    gammas = jnp.asarray((1.0 - np.exp2(-5.0 - np.arange(RET_HEADS))).astype(np.float32))

    xp = x_prompt[0]
    xs = x_sample[:, 0]
    mem = mem_prompt[0]
    kvw = SWA_KV_HEADS * HEAD_DIM

    ret_p, wink_p, winv_p, memk_p, memv_p = [], [], [], [], []
    ret_s, wink_s, winv_s = [], [], []
    for l in range(DEPTH):
        last = l == DEPTH - 1
        proj = _in_proj(xp, g_mix[l], w_in_b[l])
        ro, r_last = _ret_prompt(proj, cp, sp, dmask, xi, zeta, ret_gn[l])
        so = _swa_prompt(proj, sinks[l])
        x1, qc = _out_proj(ro, so, xp, w_out_b[l], g_cross[l], wq_b[l])
        mk, mv = _mem_kv(mem, g_mem[l], wk_b[l], wv_b[l])
        x2 = _cross_prompt(qc, mk, mv, wo_b[l], x1)
        xp = _ffn(x2, g_ffn[l], wg_b[l], wu_b[l], wd_b[l], g_final, last)
        ret_p.append(r_last[None])
        tail = proj[seq - WINDOW:]
        wink_p.append(tail[:, COL_SK * LANES:COL_SK * LANES + kvw].reshape(1, WINDOW, SWA_KV_HEADS, HEAD_DIM))
        winv_p.append(tail[:, COL_SV * LANES:COL_SV * LANES + kvw].reshape(1, WINDOW, SWA_KV_HEADS, HEAD_DIM))
        memk_p.append(mk.reshape(1, MEM_LEN, MEM_HEADS, HEAD_DIM))
        memv_p.append(mv.reshape(1, MEM_LEN, MEM_HEADS, HEAD_DIM))

        proj_s = _in_proj(xs, g_mix[l], w_in_b[l])
        ro_s, st_new = _ret_sample(proj_s, state_ret[l], gammas, cs, ss, ret_gn[l])
        sq = proj_s[:, COL_SQ * LANES:COL_SQ * LANES + SWA_W].reshape(nb, SWA_HEADS, HEAD_DIM)
        sk = proj_s[:, COL_SK * LANES:COL_SK * LANES + kvw].reshape(nb, SWA_KV_HEADS, HEAD_DIM)
        sv = proj_s[:, COL_SV * LANES:COL_SV * LANES + kvw].reshape(nb, SWA_KV_HEADS, HEAD_DIM)
        so_s, nk, nv = _swa_sample(sq, sk, sv, cache_win_k[l].reshape(nb, WINDOW, kvw),
                                   cache_win_v[l].reshape(nb, WINDOW, kvw), sinks[l])
        x1_s, qc_s = _out_proj(ro_s, so_s.reshape(nb, SWA_W), xs, w_out_b[l], g_cross[l], wq_b[l])
        oc_s = _cross_sample(qc_s.reshape(nb, MEM_HEADS, HEAD_DIM),
                             cache_mem_k[l].reshape(nb, MEM_LEN, MEM_W),
                             cache_mem_v[l].reshape(nb, MEM_LEN, MEM_W))
        x2_s = _matmul_residual(oc_s.reshape(nb, MEM_W), wo_b[l], x1_s)
        xs = _ffn(x2_s, g_ffn[l], wg_b[l], wu_b[l], wd_b[l], g_final, last)
        ret_s.append(st_new)
        wink_s.append(nk.reshape(nb, WINDOW, SWA_KV_HEADS, HEAD_DIM))
        winv_s.append(nv.reshape(nb, WINDOW, SWA_KV_HEADS, HEAD_DIM))

    return (xp[None], xs[:, None],
            jnp.stack(ret_p), jnp.stack(wink_p), jnp.stack(winv_p), jnp.stack(memk_p), jnp.stack(memv_p),
            jnp.stack(ret_s), jnp.stack(wink_s), jnp.stack(winv_s))
```

```python
import functools

import numpy as np
import jax
import jax.numpy as jnp
from jax import lax
from jax.experimental import pallas as pl
from jax.experimental.pallas import tpu as pltpu

D_MODEL = 2048
DEPTH = 2
PAST_LEN = 8192
HEAD_DIM = 128
RET_HEADS = 8
RET_W = RET_HEADS * HEAD_DIM
RET_CHUNK = 128
SWA_HEADS = 8
SWA_KV_HEADS = 2
SWA_GROUP = SWA_HEADS // SWA_KV_HEADS
SWA_W = SWA_HEADS * HEAD_DIM
WINDOW = 128
MEM_LEN = 256
MEM_HEADS = 4
MEM_W = MEM_HEADS * HEAD_DIM
IN_W = 4 * RET_W + SWA_W + 2 * SWA_KV_HEADS * HEAD_DIM
D_FF = 5632
ROPE_BASE = 10000.0
EPS = 1e-6
ATT_SCALE = HEAD_DIM ** -0.5

COL_RQ, COL_RK, COL_RV, COL_RG = 0, 8, 16, 24
COL_SQ, COL_SK, COL_SV = 32, 40, 42

V7X_VMEM_BYTES = 64 * 1024 * 1024
V7X_VMEM_BUDGET = V7X_VMEM_BYTES - 8 * 1024 * 1024
LANES = 128

BF16 = jnp.bfloat16
F32 = jnp.float32


def _vmem_limit(block_bytes, scratch_bytes=0, temp_bytes=0):
    need = 2 * block_bytes + scratch_bytes + temp_bytes + 4 * 1024 * 1024
    return int(min(max(need, 16 * 1024 * 1024), V7X_VMEM_BUDGET))


def _nbytes(shape, dtype):
    return int(np.prod(shape)) * jnp.dtype(dtype).itemsize


def _params(semantics, block_bytes, scratch_bytes=0, temp_bytes=0):
    return pltpu.CompilerParams(
        dimension_semantics=semantics,
        vmem_limit_bytes=_vmem_limit(block_bytes, scratch_bytes, temp_bytes))


def _rms_rows(x, g):
    ms = jnp.mean(x * x, axis=-1, keepdims=True)
    return x * lax.rsqrt(ms + EPS) * g


def _rms_to(dst_ref, src_ref, g_ref, rows):
    g = g_ref[...]
    chunk = min(rows, 128)

    def body(c, carry):
        r = pl.multiple_of(c * chunk, chunk)
        x = src_ref[pl.ds(r, chunk), :]
        dst_ref[pl.ds(r, chunk), :] = _rms_rows(x, g).astype(dst_ref.dtype)
        return carry

    lax.fori_loop(0, rows // chunk, body, 0)


def _sigmoid(x):
    return 1.0 / (1.0 + jnp.exp(-x))


def _dot(a, b):
    return jnp.dot(a, b, preferred_element_type=F32)


def _dot_nt(a, b):
    return lax.dot_general(a, b, (((1,), (1,)), ((), ())), preferred_element_type=F32)


def _gain_spec(l, ngrid):
    zeros = (0,) * 2
    return pl.BlockSpec((None, 1, D_MODEL), lambda *_: (l,) + zeros)


def _in_proj_kernel(x_ref, g_ref, w_ref, o_ref, h_ref, *, tm):
    @pl.when(pl.program_id(1) == 0)
    def _():
        _rms_to(h_ref, x_ref, g_ref, tm)

    o_ref[...] = _dot(h_ref[...], w_ref[...])


def _in_proj(x, g, w, l):
    m = x.shape[0]
    tm = min(m, 1024)
    tn = 512
    blocks = (_nbytes((tm, D_MODEL), F32) + _nbytes((D_MODEL, tn), BF16) + _nbytes((tm, tn), F32))
    return pl.pallas_call(
        functools.partial(_in_proj_kernel, tm=tm),
        out_shape=jax.ShapeDtypeStruct((m, IN_W), F32),
        grid=(m // tm, IN_W // tn),
        in_specs=[
            pl.BlockSpec((tm, D_MODEL), lambda i, j: (i, 0)),
            _gain_spec(l, 2),
            pl.BlockSpec((None, D_MODEL, tn), lambda i, j: (l, 0, j)),
        ],
        out_specs=pl.BlockSpec((tm, tn), lambda i, j: (i, j)),
        scratch_shapes=[pltpu.VMEM((tm, D_MODEL), BF16)],
        compiler_params=_params(("parallel", "arbitrary"), blocks, _nbytes((tm, D_MODEL), BF16)),
        name="in_proj",
    )(x, g, w)


def _out_proj_kernel(ro_ref, so_ref, x_ref, wa_ref, wb_ref, gc_ref, wq_ref, x1_ref, q_ref, h_ref, *, tm):
    y = _dot(ro_ref[...].astype(BF16), wa_ref[...]) + _dot(so_ref[...].astype(BF16), wb_ref[...])
    x1_ref[...] = x_ref[...] + y
    _rms_to(h_ref, x1_ref, gc_ref, tm)
    q_ref[...] = _dot(h_ref[...], wq_ref[...])


def _out_proj(ro, so, x, w_out, g_cross, wq, l):
    m = x.shape[0]
    tm = min(m, 512)
    blocks = (_nbytes((tm, RET_W), ro.dtype) + _nbytes((tm, SWA_W), so.dtype)
              + 2 * _nbytes((tm, D_MODEL), F32) + _nbytes((D_MODEL, D_MODEL), BF16)
              + _nbytes((D_MODEL, MEM_W), BF16) + _nbytes((tm, MEM_W), F32))
    return pl.pallas_call(
        functools.partial(_out_proj_kernel, tm=tm),
        out_shape=(jax.ShapeDtypeStruct((m, D_MODEL), F32), jax.ShapeDtypeStruct((m, MEM_W), F32)),
        grid=(m // tm,),
        in_specs=[
            pl.BlockSpec((tm, RET_W), lambda i: (i, 0)),
            pl.BlockSpec((tm, SWA_W), lambda i: (i, 0)),
            pl.BlockSpec((tm, D_MODEL), lambda i: (i, 0)),
            pl.BlockSpec((None, RET_W, D_MODEL), lambda i: (l, 0, 0)),
            pl.BlockSpec((None, SWA_W, D_MODEL), lambda i: (l, 1, 0)),
            _gain_spec(l, 1),
            pl.BlockSpec((None, D_MODEL, MEM_W), lambda i: (l, 0, 0)),
        ],
        out_specs=(pl.BlockSpec((tm, D_MODEL), lambda i: (i, 0)),
                   pl.BlockSpec((tm, MEM_W), lambda i: (i, 0))),
        scratch_shapes=[pltpu.VMEM((tm, D_MODEL), BF16)],
        compiler_params=_params(("parallel",), blocks, _nbytes((tm, D_MODEL), BF16),
                                _nbytes((tm, D_MODEL), F32)),
        name="out_proj",
    )(ro, so, x, w_out, w_out, g_cross, wq)


def _ffn_kernel(x_ref, g_ref, wg_ref, wu_ref, wd_ref, gf_ref, o_ref, h_ref, acc_ref, *, tm, nf, final_norm):
    j = pl.program_id(1)

    @pl.when(j == 0)
    def _():
        _rms_to(h_ref, x_ref, g_ref, tm)
        acc_ref[...] = jnp.zeros_like(acc_ref)

    h = h_ref[...]
    a = _dot(h, wg_ref[...])
    u = _dot(h, wu_ref[...])
    act = (a * _sigmoid(a) * u).astype(BF16)
    acc_ref[...] += _dot(act, wd_ref[...])

    @pl.when(j == nf - 1)
    def _():
        o_ref[...] = x_ref[...] + acc_ref[...]
        if final_norm:
            _rms_to(o_ref, o_ref, gf_ref, tm)


def _ffn(x, g, wg, wu, wd, g_final, l):
    m = x.shape[0]
    tm = min(m, 512)
    tf = 512
    nf = D_FF // tf
    blocks = (2 * _nbytes((tm, D_MODEL), F32) + 3 * _nbytes((D_MODEL, tf), BF16))
    scratch = _nbytes((tm, D_MODEL), BF16) + _nbytes((tm, D_MODEL), F32)
    return pl.pallas_call(
        functools.partial(_ffn_kernel, tm=tm, nf=nf, final_norm=(l == DEPTH - 1)),
        out_shape=jax.ShapeDtypeStruct((m, D_MODEL), F32),
        grid=(m // tm, nf),
        in_specs=[
            pl.BlockSpec((tm, D_MODEL), lambda i, j: (i, 0)),
            _gain_spec(l, 2),
            pl.BlockSpec((None, D_MODEL, tf), lambda i, j: (l, 0, j)),
            pl.BlockSpec((None, D_MODEL, tf), lambda i, j: (l, 0, j)),
            pl.BlockSpec((None, tf, D_MODEL), lambda i, j: (l, j, 0)),
            pl.BlockSpec((1, D_MODEL), lambda i, j: (0, 0)),
        ],
        out_specs=pl.BlockSpec((tm, D_MODEL), lambda i, j: (i, 0)),
        scratch_shapes=[pltpu.VMEM((tm, D_MODEL), BF16), pltpu.VMEM((tm, D_MODEL), F32)],
        compiler_params=_params(("parallel", "arbitrary"), blocks, scratch,
                                3 * _nbytes((tm, tf), F32) + _nbytes((tm, D_MODEL), F32)),
        name="ffn",
    )(x, g, wg, wu, wd, g_final.reshape(1, D_MODEL))


def _matmul_residual_kernel(a_ref, w_ref, x_ref, o_ref):
    o_ref[...] = x_ref[...] + _dot(a_ref[...].astype(BF16), w_ref[...])


def _matmul_residual(a, w, x, l):
    m, k = a.shape
    n = w.shape[2]
    blocks = _nbytes((m, k), a.dtype) + _nbytes((k, n), BF16) + 2 * _nbytes((m, n), F32)
    return pl.pallas_call(
        _matmul_residual_kernel,
        out_shape=jax.ShapeDtypeStruct((m, n), F32),
        grid=(1,),
        in_specs=[pl.BlockSpec((m, k), lambda i: (0, 0)),
                  pl.BlockSpec((None, k, n), lambda i: (l, 0, 0)),
                  pl.BlockSpec((m, n), lambda i: (0, 0))],
        out_specs=pl.BlockSpec((m, n), lambda i: (0, 0)),
        compiler_params=_params(("arbitrary",), blocks),
        name="matmul_residual",
    )(a, w, x)


def _rope_tables(positions):
    half = HEAD_DIM // 2
    inv = ROPE_BASE ** (-np.arange(half, dtype=np.float64) / half)
    ang = np.asarray(positions, np.float64)[:, None] * inv[None, :]
    cos, sin = np.cos(ang), np.sin(ang)
    c = np.concatenate([cos, cos], axis=-1).astype(np.float32)
    s = np.concatenate([-sin, sin], axis=-1).astype(np.float32)
    return c, s


def _decay_tables():
    c = RET_CHUNK
    lg = np.log1p(-np.exp2(-5.0 - np.arange(RET_HEADS, dtype=np.float64)))
    idx = np.arange(c, dtype=np.float64)
    diff = idx[:, None] - idx[None, :]
    dmask = np.where(diff[None] >= 0, np.exp(lg[:, None, None] * np.maximum(diff, 0.0)[None]), 0.0)
    xi = np.exp(lg[:, None] * (idx + 1.0)[None])
    zeta = np.exp(lg[:, None] * (c - 1.0 - idx)[None])
    bcast = lambda t: np.broadcast_to(t[:, :, None], (RET_HEADS, c, LANES)).astype(np.float32)
    return dmask.astype(np.float32), bcast(xi), bcast(zeta)


def _rotary(x, c, s):
    return x * c + pltpu.roll(x, HEAD_DIM // 2, 1) * s


def _ret_prompt_kernel(q_ref, k_ref, v_ref, g_ref, c_ref, s_ref, dm_ref, xi_ref, zt_ref, gn_ref,
                       o_ref, st_ref, *, nchunk):
    @pl.when(pl.program_id(1) == 0)
    def _():
        st_ref[...] = jnp.zeros_like(st_ref)

    c_tab = c_ref[...]
    s_tab = s_ref[...]
    qr = _rotary(q_ref[...], c_tab, s_tab)
    kr = _rotary(k_ref[...], c_tab, s_tab) * ATT_SCALE
    dmask = dm_ref[...]
    xi = xi_ref[...]
    zeta = zt_ref[...]
    g_chunk = xi[RET_CHUNK - 1:RET_CHUNK, :]
    gn = gn_ref[...]
    state = st_ref[...]
    for c in range(nchunk):
        rows = slice(c * RET_CHUNK, (c + 1) * RET_CHUNK)
        qc = qr[rows]
        kc = kr[rows]
        vc = v_ref[rows, :].astype(BF16)
        s = _dot_nt(qc.astype(BF16), kc.astype(BF16)) * dmask
        inner = _dot(s.astype(BF16), vc)
        cross = _dot((qc * xi).astype(BF16), state.astype(BF16))
        kv = _dot((kc * zeta).T.astype(BF16), vc)
        state = g_chunk * state + kv
        out = inner + cross
        out = out * lax.rsqrt(jnp.mean(out * out, axis=-1, keepdims=True) + EPS) * gn
        gate = g_ref[rows, :]
        o_ref[rows, :] = (gate * _sigmoid(gate) * out).astype(o_ref.dtype)
    st_ref[...] = state


def _ret_prompt(proj, c_tab, s_tab, dmask, xi, zeta, ret_gn, l):
    m = proj.shape[0]
    r = 1024
    nchunk = r // RET_CHUNK
    col = lambda off: pl.BlockSpec((r, HEAD_DIM), lambda h, i: (i, off + h))
    head_tab = pl.BlockSpec((None, RET_CHUNK, LANES), lambda h, i: (h, 0, 0))
    blocks = (6 * _nbytes((r, HEAD_DIM), F32) + 3 * _nbytes((RET_CHUNK, LANES), F32)
              + _nbytes((r, HEAD_DIM), BF16) + _nbytes((HEAD_DIM, HEAD_DIM), F32))
    return pl.pallas_call(
        functools.partial(_ret_prompt_kernel, nchunk=nchunk),
        out_shape=(jax.ShapeDtypeStruct((m, RET_W), BF16),
                   jax.ShapeDtypeStruct((RET_HEADS, HEAD_DIM, HEAD_DIM), F32)),
        grid=(RET_HEADS, m // r),
        in_specs=[col(COL_RQ), col(COL_RK), col(COL_RV), col(COL_RG),
                  pl.BlockSpec((r, HEAD_DIM), lambda h, i: (i, 0)),
                  pl.BlockSpec((r, HEAD_DIM), lambda h, i: (i, 0)),
                  head_tab, head_tab, head_tab,
                  pl.BlockSpec((None, 1, HEAD_DIM), lambda h, i: (l, 0, h))],
        out_specs=(pl.BlockSpec((r, HEAD_DIM), lambda h, i: (i, h)),
                   pl.BlockSpec((None, HEAD_DIM, HEAD_DIM), lambda h, i: (h, 0, 0))),
        compiler_params=_params(("parallel", "arbitrary"), blocks, 0, 4 * _nbytes((r, HEAD_DIM), F32)),
        name="ret_prompt",
    )(proj, proj, proj, proj, c_tab, s_tab, dmask, xi, zeta, ret_gn)


def _swa_prompt_kernel(sink_ref, q_ref, k_ref, v_ref, kp_ref, vp_ref, o_ref, *, nblk, l):
    kvh = pl.program_id(0)
    i = pl.program_id(1)
    kfull = jnp.concatenate([kp_ref[...], k_ref[...]], axis=0).astype(BF16)
    vfull = jnp.concatenate([vp_ref[...], v_ref[...]], axis=0).astype(BF16)
    row = lax.broadcasted_iota(jnp.int32, (WINDOW, 2 * WINDOW), 0)
    col = lax.broadcasted_iota(jnp.int32, (WINDOW, 2 * WINDOW), 1)
    band = (col >= row) & (col <= row + WINDOW)
    for j in range(nblk):
        rows = slice(j * WINDOW, (j + 1) * WINDOW)
        kk = kfull[j * WINDOW:(j + 2) * WINDOW]
        vv = vfull[j * WINDOW:(j + 2) * WINDOW]
        if j == 0:
            mask = band & (col >= jnp.where(i > 0, 0, WINDOW))
        else:
            mask = band
        qj = q_ref[rows, :].astype(BF16)
        qs = jnp.concatenate([qj[:, h * HEAD_DIM:(h + 1) * HEAD_DIM] for h in range(SWA_GROUP)], axis=0)
        s = _dot_nt(qs, kk) * ATT_SCALE
        ps = []
        for h in range(SWA_GROUP):
            sh = jnp.where(mask, s[h * WINDOW:(h + 1) * WINDOW], -jnp.inf)
            sink = sink_ref[l, kvh * SWA_GROUP + h]
            mx = jnp.maximum(jnp.max(sh, axis=-1, keepdims=True), sink)
            e = jnp.exp(sh - mx)
            den = jnp.sum(e, axis=-1, keepdims=True) + jnp.exp(sink - mx)
            ps.append((e / den).astype(BF16))
        o = _dot(jnp.concatenate(ps, axis=0), vv)
        for h in range(SWA_GROUP):
            o_ref[rows, h * HEAD_DIM:(h + 1) * HEAD_DIM] = o[h * WINDOW:(h + 1) * WINDOW].astype(o_ref.dtype)


def _swa_prompt(proj, sinks, l):
    m = proj.shape[0]
    r = 512
    nblk = r // WINDOW
    gw = SWA_GROUP * HEAD_DIM
    prev = lambda off: pl.BlockSpec(
        (WINDOW, HEAD_DIM), lambda kv, i: (jnp.maximum(i * nblk - 1, 0), off + kv))
    blocks = (_nbytes((r, gw), F32) + 2 * _nbytes((r, HEAD_DIM), F32) + 2 * _nbytes((WINDOW, HEAD_DIM), F32)
              + _nbytes((r, gw), BF16))
    return pl.pallas_call(
        functools.partial(_swa_prompt_kernel, nblk=nblk, l=l),
        out_shape=jax.ShapeDtypeStruct((m, SWA_W), BF16),
        grid=(SWA_KV_HEADS, m // r),
        in_specs=[pl.BlockSpec(memory_space=pltpu.SMEM),
                  pl.BlockSpec((r, gw), lambda kv, i: (i, COL_SQ // SWA_GROUP + kv)),
                  pl.BlockSpec((r, HEAD_DIM), lambda kv, i: (i, COL_SK + kv)),
                  pl.BlockSpec((r, HEAD_DIM), lambda kv, i: (i, COL_SV + kv)),
                  prev(COL_SK), prev(COL_SV)],
        out_specs=pl.BlockSpec((r, gw), lambda kv, i: (i, kv)),
        compiler_params=_params(("parallel", "parallel"), blocks, 0, 8 * _nbytes((gw, 2 * WINDOW), F32)),
        name="swa_prompt",
    )(sinks, proj, proj, proj, proj, proj)


def _mem_kv_kernel(m_ref, g_ref, wk_ref, wv_ref, k_ref, v_ref):
    h = _rms_rows(m_ref[...], g_ref[...]).astype(BF16)
    k_ref[...] = _dot(h, wk_ref[...])
    v_ref[...] = _dot(h, wv_ref[...])


def _mem_kv(mem, g, wk, wv, l):
    full = lambda shape: pl.BlockSpec(shape, lambda i: (0,) * len(shape))
    layer = lambda shape: pl.BlockSpec((None,) + shape, lambda i: (l,) + (0,) * len(shape))
    blocks = (_nbytes((MEM_LEN, D_MODEL), F32) + 2 * _nbytes((D_MODEL, MEM_W), BF16)
              + 2 * _nbytes((MEM_LEN, MEM_W), F32))
    return pl.pallas_call(
        _mem_kv_kernel,
        out_shape=(jax.ShapeDtypeStruct((MEM_LEN, MEM_W), F32),) * 2,
        grid=(1,),
        in_specs=[full((MEM_LEN, D_MODEL)), _gain_spec(l, 1), layer((D_MODEL, MEM_W)), layer((D_MODEL, MEM_W))],
        out_specs=(full((MEM_LEN, MEM_W)),) * 2,
        compiler_params=_params(("arbitrary",), blocks, 0, _nbytes((MEM_LEN, D_MODEL), F32)),
        name="mem_kv",
    )(mem, g, wk, wv)


def _cross_prompt_kernel(q_ref, mk_ref, mv_ref, wo_ref, x_ref, o_ref):
    q = q_ref[...].astype(BF16)
    mk = mk_ref[...].astype(BF16)
    mv = mv_ref[...].astype(BF16)
    outs = []
    for h in range(MEM_HEADS):
        cols = slice(h * HEAD_DIM, (h + 1) * HEAD_DIM)
        s = _dot_nt(q[:, cols], mk[:, cols]) * ATT_SCALE
        e = jnp.exp(s - jnp.max(s, axis=-1, keepdims=True))
        p = e / jnp.sum(e, axis=-1, keepdims=True)
        outs.append(_dot(p.astype(BF16), mv[:, cols]).astype(BF16))
    o_ref[...] = x_ref[...] + _dot(jnp.concatenate(outs, axis=1), wo_ref[...])


def _cross_prompt(q, mk, mv, wo, x, l):
    m = x.shape[0]
    tm = 512
    blocks = (_nbytes((tm, MEM_W), F32) + 2 * _nbytes((MEM_LEN, MEM_W), F32) + _nbytes((MEM_W, D_MODEL), BF16)
              + 2 * _nbytes((tm, D_MODEL), F32))
    return pl.pallas_call(
        _cross_prompt_kernel,
        out_shape=jax.ShapeDtypeStruct((m, D_MODEL), F32),
        grid=(m // tm,),
        in_specs=[pl.BlockSpec((tm, MEM_W), lambda i: (i, 0)),
                  pl.BlockSpec((MEM_LEN, MEM_W), lambda i: (0, 0)),
                  pl.BlockSpec((MEM_LEN, MEM_W), lambda i: (0, 0)),
                  pl.BlockSpec((None, MEM_W, D_MODEL), lambda i: (l, 0, 0)),
                  pl.BlockSpec((tm, D_MODEL), lambda i: (i, 0))],
        out_specs=pl.BlockSpec((tm, D_MODEL), lambda i: (i, 0)),
        compiler_params=_params(("parallel",), blocks, 0, 2 * _nbytes((tm, D_MODEL), F32)),
        name="cross_prompt",
    )(q, mk, mv, wo, x)


def _chain(prev):
    if prev is None:
        return (), ()
    return (prev,), (pl.BlockSpec(memory_space=pl.ANY),)


def _ret_sample_kernel(gam_ref, q_ref, k_ref, v_ref, g_ref, c_ref, s_ref, gn_ref, st_ref, *rest, bt, nt):
    o_ref, nst_ref, qt_ref, kt_ref = rest[-4:]
    h = pl.program_id(0)
    t = pl.program_id(1)

    @pl.when(t == 0)
    def _():
        c_tab = c_ref[...]
        s_tab = s_ref[...]
        qt = _rotary(q_ref[...], c_tab, s_tab).T
        kt = (_rotary(k_ref[...], c_tab, s_tab) * ATT_SCALE).T
        for tt in range(nt):
            shift = (LANES - tt * bt) % LANES
            qt_ref[tt] = pltpu.roll(qt, shift, 1) if shift else qt
            kt_ref[tt] = pltpu.roll(kt, shift, 1) if shift else kt

    gamma = gam_ref[h]
    qt = qt_ref[t]
    kt = kt_ref[t]
    outs = []
    for ib in range(bt):
        new = gamma * st_ref[ib] + kt[:, ib:ib + 1] * v_ref[ib:ib + 1, :]
        nst_ref[ib] = new
        outs.append(jnp.sum(qt[:, ib:ib + 1] * new, axis=0, keepdims=True))
    out = jnp.concatenate(outs, axis=0)
    out = out * lax.rsqrt(jnp.mean(out * out, axis=-1, keepdims=True) + EPS) * gn_ref[...]
    gate = g_ref[...]
    o_ref[...] = gate * _sigmoid(gate) * out


def _ret_sample(proj, state, prev, gammas, c_tab, s_tab, ret_gn, l):
    b = proj.shape[0]
    bt = 32
    nt = b // bt
    col = lambda off: pl.BlockSpec((b, HEAD_DIM), lambda h, t: (0, off + h))
    tile = lambda off: pl.BlockSpec((bt, HEAD_DIM), lambda h, t: (t, off + h))
    row = pl.BlockSpec((1, HEAD_DIM), lambda h, t: (0, 0))
    st_spec = pl.BlockSpec((None, bt, None, HEAD_DIM, HEAD_DIM), lambda h, t: (l, t, h, 0, 0))
    extra, extra_specs = _chain(prev)
    n_in = 9 + len(extra)
    blocks = (2 * _nbytes((b, HEAD_DIM), F32) + 3 * _nbytes((bt, HEAD_DIM), F32)
              + 2 * _nbytes((bt, HEAD_DIM, HEAD_DIM), F32))
    scratch = 2 * _nbytes((nt, HEAD_DIM, LANES), F32)
    return pl.pallas_call(
        functools.partial(_ret_sample_kernel, bt=bt, nt=nt),
        out_shape=(jax.ShapeDtypeStruct((b, RET_W), F32), jax.ShapeDtypeStruct(state.shape, F32)),
        grid=(RET_HEADS, nt),
        in_specs=[pl.BlockSpec(memory_space=pltpu.SMEM),
                  col(COL_RQ), col(COL_RK), tile(COL_RV), tile(COL_RG), row, row,
                  pl.BlockSpec((None, 1, HEAD_DIM), lambda h, t: (l, 0, h)),
                  st_spec, *extra_specs],
        out_specs=(tile(0), st_spec),
        scratch_shapes=[pltpu.VMEM((nt, HEAD_DIM, LANES), F32), pltpu.VMEM((nt, HEAD_DIM, LANES), F32)],
        input_output_aliases=({n_in - 1: 1} if extra else {}),
        compiler_params=_params(("parallel", "arbitrary"), blocks, scratch, 4 * _nbytes((bt, HEAD_DIM, HEAD_DIM), F32)),
        name="ret_sample",
    )(gammas, proj, proj, proj, proj, c_tab, s_tab, ret_gn, state, *extra)


def _swa_sample_kernel(sink_ref, q_ref, kn_ref, vn_ref, ck_ref, cv_ref, *rest, bt):
    o_ref, nk_ref, nv_ref = rest[-3:]
    rows = SWA_KV_HEADS * WINDOW
    sink = sink_ref[...]
    hrow = lax.broadcasted_iota(jnp.int32, (SWA_HEADS, HEAD_DIM), 0)
    first_group = hrow < SWA_GROUP
    srow = lax.broadcasted_iota(jnp.int32, (SWA_HEADS, rows), 0)
    scol = lax.broadcasted_iota(jnp.int32, (SWA_HEADS, rows), 1)
    own = (scol & (SWA_KV_HEADS - 1)) == (srow // SWA_GROUP)

    def body(ib, carry):
        q = q_ref[ib]
        kn = kn_ref[ib]
        vn = vn_ref[ib]
        k_new = jnp.where(first_group, kn[0:1, :], kn[1:2, :])
        v_new = jnp.where(first_group, vn[0:1, :], vn[1:2, :])
        s_new = jnp.sum(q * k_new, axis=-1, keepdims=True) * ATT_SCALE
        ck = ck_ref[ib]
        cv = cv_ref[ib]
        s = jnp.where(own, _dot_nt(q.astype(BF16), ck.astype(BF16)) * ATT_SCALE, -jnp.inf)
        mx = jnp.maximum(jnp.maximum(jnp.max(s, axis=-1, keepdims=True), s_new), sink)
        e = jnp.exp(s - mx)
        e_new = jnp.exp(s_new - mx)
        den = jnp.sum(e, axis=-1, keepdims=True) + e_new + jnp.exp(sink - mx)
        o_ref[ib] = _dot((e / den).astype(BF16), cv.astype(BF16)) + (e_new / den) * v_new
        nk_ref[ib, 0:rows - SWA_KV_HEADS, :] = ck[SWA_KV_HEADS:rows, :]
        nv_ref[ib, 0:rows - SWA_KV_HEADS, :] = cv[SWA_KV_HEADS:rows, :]
        nk_ref[ib, rows - SWA_KV_HEADS:rows, :] = kn
        nv_ref[ib, rows - SWA_KV_HEADS:rows, :] = vn
        return carry

    lax.fori_loop(0, bt, body, 0)


def _swa_sample(q, k_new, v_new, cache_k, cache_v, prev_k, prev_v, sinks, l):
    b = q.shape[0]
    bt = 16
    rows = SWA_KV_HEADS * WINDOW
    blocks = (2 * _nbytes((bt, SWA_HEADS, HEAD_DIM), F32) + 2 * _nbytes((bt, 8, HEAD_DIM), F32)
              + 4 * _nbytes((bt, rows, HEAD_DIM), F32))
    cache_spec = pl.BlockSpec((None, bt, rows, HEAD_DIM), lambda t: (l, t, 0, 0))
    new_spec = pl.BlockSpec((bt, SWA_KV_HEADS, HEAD_DIM), lambda t: (t, 0, 0))
    head_spec = pl.BlockSpec((bt, SWA_HEADS, HEAD_DIM), lambda t: (t, 0, 0))
    extra_k, spec_k = _chain(prev_k)
    extra_v, spec_v = _chain(prev_v)
    aliases = {6: 1, 7: 2} if extra_k else {}
    return pl.pallas_call(
        functools.partial(_swa_sample_kernel, bt=bt),
        out_shape=(jax.ShapeDtypeStruct((b, SWA_HEADS, HEAD_DIM), F32),
                   jax.ShapeDtypeStruct(cache_k.shape, F32), jax.ShapeDtypeStruct(cache_v.shape, F32)),
        grid=(b // bt,),
        in_specs=[pl.BlockSpec((None, SWA_HEADS, 1), lambda t: (l, 0, 0)),
                  head_spec, new_spec, new_spec, cache_spec, cache_spec, *spec_k, *spec_v],
        out_specs=(head_spec, cache_spec, cache_spec),
        input_output_aliases=aliases,
        compiler_params=_params(("parallel",), blocks),
        name="swa_sample",
    )(sinks.reshape(DEPTH, SWA_HEADS, 1), q, k_new, v_new, cache_k, cache_v, *extra_k, *extra_v)


def _cross_sample_kernel(q_ref, mk_ref, mv_ref, o_ref, *, bt):
    rows = MEM_HEADS * MEM_LEN
    srow = lax.broadcasted_iota(jnp.int32, (MEM_HEADS, rows), 0)
    scol = lax.broadcasted_iota(jnp.int32, (MEM_HEADS, rows), 1)
    own = (scol & (MEM_HEADS - 1)) == srow

    def body(ib, carry):
        qb = q_ref[ib].astype(BF16)
        s = jnp.where(own, _dot_nt(qb, mk_ref[ib].astype(BF16)) * ATT_SCALE, -jnp.inf)
        e = jnp.exp(s - jnp.max(s, axis=-1, keepdims=True))
        p = (e / jnp.sum(e, axis=-1, keepdims=True)).astype(BF16)
        o_ref[ib] = _dot(p, mv_ref[ib].astype(BF16))
        return carry

    lax.fori_loop(0, bt, body, 0)


def _cross_sample(q, cache_k, cache_v, l):
    b = q.shape[0]
    bt = 8
    rows = MEM_HEADS * MEM_LEN
    blocks = 2 * _nbytes((bt, 8, HEAD_DIM), F32) + 2 * _nbytes((bt, rows, HEAD_DIM), F32)
    cache_spec = pl.BlockSpec((None, bt, rows, HEAD_DIM), lambda t: (l, t, 0, 0))
    head_spec = pl.BlockSpec((bt, MEM_HEADS, HEAD_DIM), lambda t: (t, 0, 0))
    return pl.pallas_call(
        functools.partial(_cross_sample_kernel, bt=bt),
        out_shape=jax.ShapeDtypeStruct((b, MEM_HEADS, HEAD_DIM), F32),
        grid=(b // bt,),
        in_specs=[head_spec, cache_spec, cache_spec],
        out_specs=head_spec,
        compiler_params=_params(("parallel",), blocks),
        name="cross_sample",
    )(q, cache_k, cache_v)


def kernel(x_prompt, x_sample, mem_prompt, state_ret, cache_win_k, cache_win_v, cache_mem_k, cache_mem_v,
           g_mix, w_in, ret_gn, sinks, w_out, g_cross, g_mem, wq_c, wk_c, wv_c, wo_c,
           g_ffn, w_gate, w_up, w_down, g_final):
    seq = x_prompt.shape[1]
    nb = x_sample.shape[0]
    w_in_b, w_out_b = w_in.astype(BF16), w_out.astype(BF16)
    wq_b, wk_b, wv_b, wo_b = (w.astype(BF16) for w in (wq_c, wk_c, wv_c, wo_c))
    wg_b, wu_b, wd_b = (w.astype(BF16) for w in (w_gate, w_up, w_down))
    gain = lambda g: g.reshape(DEPTH, 1, g.shape[-1])
    g_mix, g_cross, g_mem, g_ffn, ret_gn = (gain(g) for g in (g_mix, g_cross, g_mem, g_ffn, ret_gn))

    cp, sp = (jnp.asarray(t) for t in _rope_tables(np.arange(seq)))
    cs, ss = (jnp.asarray(t) for t in _rope_tables(np.array([PAST_LEN])))
    dmask, xi, zeta = (jnp.asarray(t) for t in _decay_tables())
    gammas = jnp.asarray((1.0 - np.exp2(-5.0 - np.arange(RET_HEADS))).astype(np.float32))

    xp = x_prompt[0]
    xs = x_sample[:, 0]
    mem = mem_prompt[0]
    kvw = SWA_KV_HEADS * HEAD_DIM
    win_k = cache_win_k.reshape(DEPTH, nb, WINDOW * SWA_KV_HEADS, HEAD_DIM)
    win_v = cache_win_v.reshape(DEPTH, nb, WINDOW * SWA_KV_HEADS, HEAD_DIM)
    mem_k = cache_mem_k.reshape(DEPTH, nb, MEM_LEN * MEM_HEADS, HEAD_DIM)
    mem_v = cache_mem_v.reshape(DEPTH, nb, MEM_LEN * MEM_HEADS, HEAD_DIM)

    ret_p, wink_p, winv_p, memk_p, memv_p = [], [], [], [], []
    ret_s = wink_s = winv_s = None
    for l in range(DEPTH):
        proj = _in_proj(xp, g_mix, w_in_b, l)
        ro, r_last = _ret_prompt(proj, cp, sp, dmask, xi, zeta, ret_gn, l)
        so = _swa_prompt(proj, sinks, l)
        x1, qc = _out_proj(ro, so, xp, w_out_b, g_cross, wq_b, l)
        mk, mv = _mem_kv(mem, g_mem, wk_b, wv_b, l)
        x2 = _cross_prompt(qc, mk, mv, wo_b, x1, l)
        xp = _ffn(x2, g_ffn, wg_b, wu_b, wd_b, g_final, l)
        ret_p.append(r_last[None])
        tail = proj[seq - WINDOW:]
        wink_p.append(tail[:, COL_SK * LANES:COL_SK * LANES + kvw].reshape(1, WINDOW, SWA_KV_HEADS, HEAD_DIM))
        winv_p.append(tail[:, COL_SV * LANES:COL_SV * LANES + kvw].reshape(1, WINDOW, SWA_KV_HEADS, HEAD_DIM))
        memk_p.append(mk.reshape(1, MEM_LEN, MEM_HEADS, HEAD_DIM))
        memv_p.append(mv.reshape(1, MEM_LEN, MEM_HEADS, HEAD_DIM))

        proj_s = _in_proj(xs, g_mix, w_in_b, l)
        ro_s, ret_s = _ret_sample(proj_s, state_ret, ret_s, gammas, cs, ss, ret_gn, l)
        sq = proj_s[:, COL_SQ * LANES:COL_SQ * LANES + SWA_W].reshape(nb, SWA_HEADS, HEAD_DIM)
        sk = proj_s[:, COL_SK * LANES:COL_SK * LANES + kvw].reshape(nb, SWA_KV_HEADS, HEAD_DIM)
        sv = proj_s[:, COL_SV * LANES:COL_SV * LANES + kvw].reshape(nb, SWA_KV_HEADS, HEAD_DIM)
        so_s, wink_s, winv_s = _swa_sample(sq, sk, sv, win_k, win_v, wink_s, winv_s, sinks, l)
        x1_s, qc_s = _out_proj(ro_s, so_s.reshape(nb, SWA_W), xs, w_out_b, g_cross, wq_b, l)
        oc_s = _cross_sample(qc_s.reshape(nb, MEM_HEADS, HEAD_DIM), mem_k, mem_v, l)
        x2_s = _matmul_residual(oc_s.reshape(nb, MEM_W), wo_b, x1_s, l)
        xs = _ffn(x2_s, g_ffn, wg_b, wu_b, wd_b, g_final, l)

    cache_shape = (DEPTH, nb, WINDOW, SWA_KV_HEADS, HEAD_DIM)
    return (xp[None], xs[:, None],
            jnp.stack(ret_p), jnp.stack(wink_p), jnp.stack(winv_p), jnp.stack(memk_p), jnp.stack(memv_p),
            ret_s, wink_s.reshape(cache_shape), winv_s.reshape(cache_shape))
```

```python
import functools

import numpy as np
import jax
import jax.numpy as jnp
from jax import lax
from jax.experimental import pallas as pl
from jax.experimental.pallas import tpu as pltpu

D_MODEL = 2048
DEPTH = 2
PAST_LEN = 8192
HEAD_DIM = 128
RET_HEADS = 8
RET_W = RET_HEADS * HEAD_DIM
RET_CHUNK = 128
SWA_HEADS = 8
SWA_KV_HEADS = 2
SWA_GROUP = SWA_HEADS // SWA_KV_HEADS
SWA_W = SWA_HEADS * HEAD_DIM
WINDOW = 128
MEM_LEN = 256
MEM_HEADS = 4
MEM_W = MEM_HEADS * HEAD_DIM
IN_W = 4 * RET_W + SWA_W + 2 * SWA_KV_HEADS * HEAD_DIM
D_FF = 5632
ROPE_BASE = 10000.0
EPS = 1e-6
ATT_SCALE = HEAD_DIM ** -0.5

COL_RQ, COL_RK, COL_RV, COL_RG = 0, 8, 16, 24
COL_SQ, COL_SK, COL_SV = 32, 40, 42

V7X_VMEM_BYTES = 64 * 1024 * 1024
V7X_VMEM_BUDGET = V7X_VMEM_BYTES - 8 * 1024 * 1024
LANES = 128

BF16 = jnp.bfloat16
F32 = jnp.float32


def _vmem_limit(block_bytes, scratch_bytes=0, temp_bytes=0):
    need = 2 * block_bytes + scratch_bytes + temp_bytes + 4 * 1024 * 1024
    return int(min(max(need, 16 * 1024 * 1024), V7X_VMEM_BUDGET))


def _nbytes(shape, dtype):
    return int(np.prod(shape)) * jnp.dtype(dtype).itemsize


def _params(semantics, block_bytes, scratch_bytes=0, temp_bytes=0):
    return pltpu.CompilerParams(
        dimension_semantics=semantics,
        vmem_limit_bytes=_vmem_limit(block_bytes, scratch_bytes, temp_bytes))


def _rms_rows(x, g):
    ms = jnp.mean(x * x, axis=-1, keepdims=True)
    return x * lax.rsqrt(ms + EPS) * g


def _rms_to(dst_ref, src_ref, g_ref, rows):
    g = g_ref[...]
    chunk = min(rows, 128)

    def body(c, carry):
        r = pl.multiple_of(c * chunk, chunk)
        x = src_ref[pl.ds(r, chunk), :]
        dst_ref[pl.ds(r, chunk), :] = _rms_rows(x, g).astype(dst_ref.dtype)
        return carry

    lax.fori_loop(0, rows // chunk, body, 0)


def _sigmoid(x):
    return 1.0 / (1.0 + jnp.exp(-x))


def _dot(a, b):
    return jnp.dot(a, b, preferred_element_type=F32)


def _dot_nt(a, b):
    return lax.dot_general(a, b, (((1,), (1,)), ((), ())), preferred_element_type=F32)


def _gain_spec(l, ngrid):
    zeros = (0,) * 2
    return pl.BlockSpec((None, 1, D_MODEL), lambda *_: (l,) + zeros)


def _in_proj_kernel(x_ref, g_ref, w_ref, o_ref, h_ref, *, tm):
    @pl.when(pl.program_id(1) == 0)
    def _():
        _rms_to(h_ref, x_ref, g_ref, tm)

    o_ref[...] = _dot(h_ref[...], w_ref[...].astype(BF16))


def _in_proj(x, g, w, l):
    m = x.shape[0]
    tm = min(m, 1024)
    tn = 512
    blocks = (_nbytes((tm, D_MODEL), F32) + _nbytes((D_MODEL, tn), F32) + _nbytes((tm, tn), F32))
    return pl.pallas_call(
        functools.partial(_in_proj_kernel, tm=tm),
        out_shape=jax.ShapeDtypeStruct((m, IN_W), F32),
        grid=(m // tm, IN_W // tn),
        in_specs=[
            pl.BlockSpec((tm, D_MODEL), lambda i, j: (i, 0)),
            _gain_spec(l, 2),
            pl.BlockSpec((None, D_MODEL, tn), lambda i, j: (l, 0, j)),
        ],
        out_specs=pl.BlockSpec((tm, tn), lambda i, j: (i, j)),
        scratch_shapes=[pltpu.VMEM((tm, D_MODEL), BF16)],
        compiler_params=_params(("parallel", "arbitrary"), blocks, _nbytes((tm, D_MODEL), BF16),
                                _nbytes((D_MODEL, tn), BF16)),
        name="in_proj",
    )(x, g, w)


def _out_proj_kernel(ro_ref, so_ref, x_ref, wa_ref, wb_ref, gc_ref, wq_ref, x1_ref, q_ref, h_ref, *, tm):
    y = _dot(ro_ref[...].astype(BF16), wa_ref[...]) + _dot(so_ref[...].astype(BF16), wb_ref[...])
    x1_ref[...] = x_ref[...] + y
    _rms_to(h_ref, x1_ref, gc_ref, tm)
    q_ref[...] = _dot(h_ref[...], wq_ref[...])


def _out_proj(ro, so, x, w_out, g_cross, wq, l):
    m = x.shape[0]
    tm = min(m, 512)
    blocks = (_nbytes((tm, RET_W), ro.dtype) + _nbytes((tm, SWA_W), so.dtype)
              + 2 * _nbytes((tm, D_MODEL), F32) + _nbytes((D_MODEL, D_MODEL), BF16)
              + _nbytes((D_MODEL, MEM_W), BF16) + _nbytes((tm, MEM_W), F32))
    return pl.pallas_call(
        functools.partial(_out_proj_kernel, tm=tm),
        out_shape=(jax.ShapeDtypeStruct((m, D_MODEL), F32), jax.ShapeDtypeStruct((m, MEM_W), F32)),
        grid=(m // tm,),
        in_specs=[
            pl.BlockSpec((tm, RET_W), lambda i: (i, 0)),
            pl.BlockSpec((tm, SWA_W), lambda i: (i, 0)),
            pl.BlockSpec((tm, D_MODEL), lambda i: (i, 0)),
            pl.BlockSpec((None, RET_W, D_MODEL), lambda i: (l, 0, 0)),
            pl.BlockSpec((None, SWA_W, D_MODEL), lambda i: (l, 1, 0)),
            _gain_spec(l, 1),
            pl.BlockSpec((None, D_MODEL, MEM_W), lambda i: (l, 0, 0)),
        ],
        out_specs=(pl.BlockSpec((tm, D_MODEL), lambda i: (i, 0)),
                   pl.BlockSpec((tm, MEM_W), lambda i: (i, 0))),
        scratch_shapes=[pltpu.VMEM((tm, D_MODEL), BF16)],
        compiler_params=_params(("parallel",), blocks, _nbytes((tm, D_MODEL), BF16),
                                _nbytes((tm, D_MODEL), F32)),
        name="out_proj",
    )(ro, so, x, w_out, w_out, g_cross, wq)


def _ffn_kernel(x_ref, g_ref, wgu_ref, wd_ref, gf_ref, o_ref, h_ref, acc_ref, *, tm, tf, nf, final_norm):
    j = pl.program_id(1)

    @pl.when(j == 0)
    def _():
        _rms_to(h_ref, x_ref, g_ref, tm)
        acc_ref[...] = jnp.zeros_like(acc_ref)

    au = _dot(h_ref[...], wgu_ref[...])
    a = au[:, :tf]
    u = au[:, tf:]
    act = (a * _sigmoid(a) * u).astype(BF16)
    acc_ref[...] += _dot(act, wd_ref[...])

    @pl.when(j == nf - 1)
    def _():
        o_ref[...] = x_ref[...] + acc_ref[...]
        if final_norm:
            _rms_to(o_ref, o_ref, gf_ref, tm)


FFN_TF = 512


def _pair_gate_up(w_gate, w_up):
    nf = D_FF // FFN_TF
    split = lambda w: w.reshape(DEPTH, D_MODEL, nf, 1, FFN_TF)
    return jnp.concatenate([split(w_gate), split(w_up)], axis=3).astype(BF16).reshape(DEPTH, D_MODEL, 2 * D_FF)


def _ffn(x, g, wgu, wd, g_final, l):
    m = x.shape[0]
    tm = min(m, 512)
    tf = FFN_TF
    nf = D_FF // tf
    blocks = (2 * _nbytes((tm, D_MODEL), F32) + 3 * _nbytes((D_MODEL, tf), BF16))
    scratch = _nbytes((tm, D_MODEL), BF16) + _nbytes((tm, D_MODEL), F32)
    return pl.pallas_call(
        functools.partial(_ffn_kernel, tm=tm, tf=tf, nf=nf, final_norm=(l == DEPTH - 1)),
        out_shape=jax.ShapeDtypeStruct((m, D_MODEL), F32),
        grid=(m // tm, nf),
        in_specs=[
            pl.BlockSpec((tm, D_MODEL), lambda i, j: (i, 0)),
            _gain_spec(l, 2),
            pl.BlockSpec((None, D_MODEL, 2 * tf), lambda i, j: (l, 0, j)),
            pl.BlockSpec((None, tf, D_MODEL), lambda i, j: (l, j, 0)),
            pl.BlockSpec((1, D_MODEL), lambda i, j: (0, 0)),
        ],
        out_specs=pl.BlockSpec((tm, D_MODEL), lambda i, j: (i, 0)),
        scratch_shapes=[pltpu.VMEM((tm, D_MODEL), BF16), pltpu.VMEM((tm, D_MODEL), F32)],
        compiler_params=_params(("parallel", "arbitrary"), blocks, scratch,
                                3 * _nbytes((tm, tf), F32) + _nbytes((tm, D_MODEL), F32)),
        name="ffn",
    )(x, g, wgu, wd, g_final.reshape(1, D_MODEL))


def _matmul_residual_kernel(a_ref, w_ref, x_ref, o_ref):
    o_ref[...] = x_ref[...] + _dot(a_ref[...].astype(BF16), w_ref[...])


def _matmul_residual(a, w, x, l):
    m, k = a.shape
    n = w.shape[2]
    blocks = _nbytes((m, k), a.dtype) + _nbytes((k, n), BF16) + 2 * _nbytes((m, n), F32)
    return pl.pallas_call(
        _matmul_residual_kernel,
        out_shape=jax.ShapeDtypeStruct((m, n), F32),
        grid=(1,),
        in_specs=[pl.BlockSpec((m, k), lambda i: (0, 0)),
                  pl.BlockSpec((None, k, n), lambda i: (l, 0, 0)),
                  pl.BlockSpec((m, n), lambda i: (0, 0))],
        out_specs=pl.BlockSpec((m, n), lambda i: (0, 0)),
        compiler_params=_params(("arbitrary",), blocks),
        name="matmul_residual",
    )(a, w, x)


def _rope_tables(positions):
    half = HEAD_DIM // 2
    inv = ROPE_BASE ** (-np.arange(half, dtype=np.float64) / half)
    ang = np.asarray(positions, np.float64)[:, None] * inv[None, :]
    cos, sin = np.cos(ang), np.sin(ang)
    c = np.concatenate([cos, cos], axis=-1).astype(np.float32)
    s = np.concatenate([-sin, sin], axis=-1).astype(np.float32)
    return c, s


def _decay_tables():
    c = RET_CHUNK
    lg = np.log1p(-np.exp2(-5.0 - np.arange(RET_HEADS, dtype=np.float64)))
    idx = np.arange(c, dtype=np.float64)
    diff = idx[:, None] - idx[None, :]
    dmask = np.where(diff[None] >= 0, np.exp(lg[:, None, None] * np.maximum(diff, 0.0)[None]), 0.0)
    xi = np.exp(lg[:, None] * (idx + 1.0)[None])
    zeta = np.exp(lg[:, None] * (c - 1.0 - idx)[None])
    xi_b = np.broadcast_to(xi[:, :, None], (RET_HEADS, c, LANES)).astype(np.float32)
    zeta_t = np.broadcast_to(zeta[:, None, :], (RET_HEADS, HEAD_DIM, c)).astype(np.float32)
    return dmask.astype(np.float32), xi_b, zeta_t


def _rotary(x, c, s):
    return x * c + pltpu.roll(x, HEAD_DIM // 2, 1) * s


def _ret_prompt_kernel(q_ref, k_ref, v_ref, g_ref, c_ref, s_ref, dm_ref, xi_ref, zt_ref, gn_ref,
                       o_ref, st_ref, *, nchunk):
    @pl.when(pl.program_id(1) == 0)
    def _():
        st_ref[...] = jnp.zeros_like(st_ref)

    c_tab = c_ref[...]
    s_tab = s_ref[...]
    qr = _rotary(q_ref[...], c_tab, s_tab)
    kr = _rotary(k_ref[...], c_tab, s_tab) * ATT_SCALE
    dmask = dm_ref[...]
    xi = xi_ref[...]
    zeta_t = zt_ref[...]
    g_chunk = xi[RET_CHUNK - 1:RET_CHUNK, :]
    gn = gn_ref[...]
    state = st_ref[...]
    for c in range(nchunk):
        rows = slice(c * RET_CHUNK, (c + 1) * RET_CHUNK)
        qc = qr[rows]
        kct = kr[rows].T
        vc = v_ref[rows, :].astype(BF16)
        t = _dot(qc.astype(BF16), jnp.concatenate([kct.astype(BF16), state.astype(BF16)], axis=1))
        s = t[:, :RET_CHUNK] * dmask
        cross = t[:, RET_CHUNK:] * xi
        r = _dot(jnp.concatenate([s.astype(BF16), (kct * zeta_t).astype(BF16)], axis=0), vc)
        state = g_chunk * state + r[RET_CHUNK:]
        out = r[:RET_CHUNK] + cross
        out = out * lax.rsqrt(jnp.mean(out * out, axis=-1, keepdims=True) + EPS) * gn
        gate = g_ref[rows, :]
        o_ref[rows, :] = (gate * _sigmoid(gate) * out).astype(o_ref.dtype)
    st_ref[...] = state


def _ret_prompt(proj, c_tab, s_tab, dmask, xi, zeta, ret_gn, l):
    m = proj.shape[0]
    r = 1024
    nchunk = r // RET_CHUNK
    col = lambda off: pl.BlockSpec((r, HEAD_DIM), lambda h, i: (i, off + h))
    head_tab = pl.BlockSpec((None, RET_CHUNK, LANES), lambda h, i: (h, 0, 0))
    blocks = (6 * _nbytes((r, HEAD_DIM), F32) + 3 * _nbytes((RET_CHUNK, LANES), F32)
              + _nbytes((r, HEAD_DIM), BF16) + _nbytes((HEAD_DIM, HEAD_DIM), F32))
    return pl.pallas_call(
        functools.partial(_ret_prompt_kernel, nchunk=nchunk),
        out_shape=(jax.ShapeDtypeStruct((m, RET_W), BF16),
                   jax.ShapeDtypeStruct((RET_HEADS, HEAD_DIM, HEAD_DIM), F32)),
        grid=(RET_HEADS, m // r),
        in_specs=[col(COL_RQ), col(COL_RK), col(COL_RV), col(COL_RG),
                  pl.BlockSpec((r, HEAD_DIM), lambda h, i: (i, 0)),
                  pl.BlockSpec((r, HEAD_DIM), lambda h, i: (i, 0)),
                  head_tab, head_tab, head_tab,
                  pl.BlockSpec((None, 1, HEAD_DIM), lambda h, i: (l, 0, h))],
        out_specs=(pl.BlockSpec((r, HEAD_DIM), lambda h, i: (i, h)),
                   pl.BlockSpec((None, HEAD_DIM, HEAD_DIM), lambda h, i: (h, 0, 0))),
        compiler_params=_params(("parallel", "arbitrary"), blocks, 0, 4 * _nbytes((r, HEAD_DIM), F32)),
        name="ret_prompt",
    )(proj, proj, proj, proj, c_tab, s_tab, dmask, xi, zeta, ret_gn)


def _swa_prompt_kernel(sink_ref, q_ref, k_ref, v_ref, kp_ref, vp_ref, o_ref, *, nblk, l):
    kvh = pl.program_id(0)
    i = pl.program_id(1)
    kfull = jnp.concatenate([kp_ref[...], k_ref[...]], axis=0).astype(BF16)
    vfull = jnp.concatenate([vp_ref[...], v_ref[...]], axis=0).astype(BF16)
    row = lax.broadcasted_iota(jnp.int32, (WINDOW, 2 * WINDOW), 0)
    col = lax.broadcasted_iota(jnp.int32, (WINDOW, 2 * WINDOW), 1)
    band = (col >= row) & (col <= row + WINDOW)
    for j in range(nblk):
        rows = slice(j * WINDOW, (j + 1) * WINDOW)
        kk = kfull[j * WINDOW:(j + 2) * WINDOW]
        vv = vfull[j * WINDOW:(j + 2) * WINDOW]
        if j == 0:
            mask = band & (col >= jnp.where(i > 0, 0, WINDOW))
        else:
            mask = band
        qj = q_ref[rows, :].astype(BF16)
        qs = jnp.concatenate([qj[:, h * HEAD_DIM:(h + 1) * HEAD_DIM] for h in range(SWA_GROUP)], axis=0)
        s = _dot_nt(qs, kk) * ATT_SCALE
        ps = []
        for h in range(SWA_GROUP):
            sh = jnp.where(mask, s[h * WINDOW:(h + 1) * WINDOW], -jnp.inf)
            sink = sink_ref[l, kvh * SWA_GROUP + h]
            mx = jnp.maximum(jnp.max(sh, axis=-1, keepdims=True), sink)
            e = jnp.exp(sh - mx)
            den = jnp.sum(e, axis=-1, keepdims=True) + jnp.exp(sink - mx)
            ps.append((e / den).astype(BF16))
        o = _dot(jnp.concatenate(ps, axis=0), vv)
        for h in range(SWA_GROUP):
            o_ref[rows, h * HEAD_DIM:(h + 1) * HEAD_DIM] = o[h * WINDOW:(h + 1) * WINDOW].astype(o_ref.dtype)


def _swa_prompt(proj, sinks, l):
    m = proj.shape[0]
    r = 512
    nblk = r // WINDOW
    gw = SWA_GROUP * HEAD_DIM
    prev = lambda off: pl.BlockSpec(
        (WINDOW, HEAD_DIM), lambda kv, i: (jnp.maximum(i * nblk - 1, 0), off + kv))
    blocks = (_nbytes((r, gw), F32) + 2 * _nbytes((r, HEAD_DIM), F32) + 2 * _nbytes((WINDOW, HEAD_DIM), F32)
              + _nbytes((r, gw), BF16))
    return pl.pallas_call(
        functools.partial(_swa_prompt_kernel, nblk=nblk, l=l),
        out_shape=jax.ShapeDtypeStruct((m, SWA_W), BF16),
        grid=(SWA_KV_HEADS, m // r),
        in_specs=[pl.BlockSpec(memory_space=pltpu.SMEM),
                  pl.BlockSpec((r, gw), lambda kv, i: (i, COL_SQ // SWA_GROUP + kv)),
                  pl.BlockSpec((r, HEAD_DIM), lambda kv, i: (i, COL_SK + kv)),
                  pl.BlockSpec((r, HEAD_DIM), lambda kv, i: (i, COL_SV + kv)),
                  prev(COL_SK), prev(COL_SV)],
        out_specs=pl.BlockSpec((r, gw), lambda kv, i: (i, kv)),
        compiler_params=_params(("parallel", "parallel"), blocks, 0, 8 * _nbytes((gw, 2 * WINDOW), F32)),
        name="swa_prompt",
    )(sinks, proj, proj, proj, proj, proj)


def _mem_kv_kernel(m_ref, g_ref, wk_ref, wv_ref, k_ref, v_ref):
    h = _rms_rows(m_ref[...], g_ref[...]).astype(BF16)
    k_ref[...] = _dot(h, wk_ref[...])
    v_ref[...] = _dot(h, wv_ref[...])


def _mem_kv(mem, g, wk, wv, l):
    full = lambda shape: pl.BlockSpec(shape, lambda i: (0,) * len(shape))
    layer = lambda shape: pl.BlockSpec((None,) + shape, lambda i: (l,) + (0,) * len(shape))
    blocks = (_nbytes((MEM_LEN, D_MODEL), F32) + 2 * _nbytes((D_MODEL, MEM_W), BF16)
              + 2 * _nbytes((MEM_LEN, MEM_W), F32))
    return pl.pallas_call(
        _mem_kv_kernel,
        out_shape=(jax.ShapeDtypeStruct((MEM_LEN, MEM_W), F32),) * 2,
        grid=(1,),
        in_specs=[full((MEM_LEN, D_MODEL)), _gain_spec(l, 1), layer((D_MODEL, MEM_W)), layer((D_MODEL, MEM_W))],
        out_specs=(full((MEM_LEN, MEM_W)),) * 2,
        compiler_params=_params(("arbitrary",), blocks, 0, _nbytes((MEM_LEN, D_MODEL), F32)),
        name="mem_kv",
    )(mem, g, wk, wv)


def _cross_prompt_kernel(q_ref, mk_ref, mv_ref, wo_ref, x_ref, o_ref):
    q = q_ref[...].astype(BF16)
    mk = mk_ref[...].astype(BF16)
    mv = mv_ref[...].astype(BF16)
    outs = []
    for h in range(MEM_HEADS):
        cols = slice(h * HEAD_DIM, (h + 1) * HEAD_DIM)
        s = _dot_nt(q[:, cols], mk[:, cols]) * ATT_SCALE
        e = jnp.exp(s - jnp.max(s, axis=-1, keepdims=True))
        p = e / jnp.sum(e, axis=-1, keepdims=True)
        outs.append(_dot(p.astype(BF16), mv[:, cols]).astype(BF16))
    o_ref[...] = x_ref[...] + _dot(jnp.concatenate(outs, axis=1), wo_ref[...])


def _cross_prompt(q, mk, mv, wo, x, l):
    m = x.shape[0]
    tm = 512
    blocks = (_nbytes((tm, MEM_W), F32) + 2 * _nbytes((MEM_LEN, MEM_W), F32) + _nbytes((MEM_W, D_MODEL), BF16)
              + 2 * _nbytes((tm, D_MODEL), F32))
    return pl.pallas_call(
        _cross_prompt_kernel,
        out_shape=jax.ShapeDtypeStruct((m, D_MODEL), F32),
        grid=(m // tm,),
        in_specs=[pl.BlockSpec((tm, MEM_W), lambda i: (i, 0)),
                  pl.BlockSpec((MEM_LEN, MEM_W), lambda i: (0, 0)),
                  pl.BlockSpec((MEM_LEN, MEM_W), lambda i: (0, 0)),
                  pl.BlockSpec((None, MEM_W, D_MODEL), lambda i: (l, 0, 0)),
                  pl.BlockSpec((tm, D_MODEL), lambda i: (i, 0))],
        out_specs=pl.BlockSpec((tm, D_MODEL), lambda i: (i, 0)),
        compiler_params=_params(("parallel",), blocks, 0, 2 * _nbytes((tm, D_MODEL), F32)),
        name="cross_prompt",
    )(q, mk, mv, wo, x)


def _chain(prev):
    if prev is None:
        return (), ()
    return (prev,), (pl.BlockSpec(memory_space=pl.ANY),)


def _ret_sample_kernel(gam_ref, q_ref, k_ref, v_ref, g_ref, c_ref, s_ref, gn_ref, st_ref, *rest, bt, nt):
    o_ref, nst_ref, qt_ref, kt_ref = rest[-4:]
    h = pl.program_id(0)
    t = pl.program_id(1)

    @pl.when(t == 0)
    def _():
        c_tab = c_ref[...]
        s_tab = s_ref[...]
        qt = _rotary(q_ref[...], c_tab, s_tab).T
        kt = (_rotary(k_ref[...], c_tab, s_tab) * ATT_SCALE).T
        for tt in range(nt):
            shift = (LANES - tt * bt) % LANES
            qt_ref[tt] = pltpu.roll(qt, shift, 1) if shift else qt
            kt_ref[tt] = pltpu.roll(kt, shift, 1) if shift else kt

    gamma = gam_ref[h]
    qt = qt_ref[t]
    kt = kt_ref[t]
    outs = []
    for ib in range(bt):
        new = gamma * st_ref[ib] + kt[:, ib:ib + 1] * v_ref[ib:ib + 1, :]
        nst_ref[ib] = new
        outs.append(jnp.sum(qt[:, ib:ib + 1] * new, axis=0, keepdims=True))
    out = jnp.concatenate(outs, axis=0)
    out = out * lax.rsqrt(jnp.mean(out * out, axis=-1, keepdims=True) + EPS) * gn_ref[...]
    gate = g_ref[...]
    o_ref[...] = gate * _sigmoid(gate) * out


def _ret_sample(proj, state, prev, gammas, c_tab, s_tab, ret_gn, l):
    b = proj.shape[0]
    bt = 32
    nt = b // bt
    col = lambda off: pl.BlockSpec((b, HEAD_DIM), lambda h, t: (0, off + h))
    tile = lambda off: pl.BlockSpec((bt, HEAD_DIM), lambda h, t: (t, off + h))
    row = pl.BlockSpec((1, HEAD_DIM), lambda h, t: (0, 0))
    st_spec = pl.BlockSpec((None, bt, None, HEAD_DIM, HEAD_DIM), lambda h, t: (l, t, h, 0, 0))
    extra, extra_specs = _chain(prev)
    n_in = 9 + len(extra)
    blocks = (2 * _nbytes((b, HEAD_DIM), F32) + 3 * _nbytes((bt, HEAD_DIM), F32)
              + 2 * _nbytes((bt, HEAD_DIM, HEAD_DIM), F32))
    scratch = 2 * _nbytes((nt, HEAD_DIM, LANES), F32)
    return pl.pallas_call(
        functools.partial(_ret_sample_kernel, bt=bt, nt=nt),
        out_shape=(jax.ShapeDtypeStruct((b, RET_W), F32), jax.ShapeDtypeStruct(state.shape, F32)),
        grid=(RET_HEADS, nt),
        in_specs=[pl.BlockSpec(memory_space=pltpu.SMEM),
                  col(COL_RQ), col(COL_RK), tile(COL_RV), tile(COL_RG), row, row,
                  pl.BlockSpec((None, 1, HEAD_DIM), lambda h, t: (l, 0, h)),
                  st_spec, *extra_specs],
        out_specs=(tile(0), st_spec),
        scratch_shapes=[pltpu.VMEM((nt, HEAD_DIM, LANES), F32), pltpu.VMEM((nt, HEAD_DIM, LANES), F32)],
        input_output_aliases=({n_in - 1: 1} if extra else {}),
        compiler_params=_params(("parallel", "arbitrary"), blocks, scratch, 4 * _nbytes((bt, HEAD_DIM, HEAD_DIM), F32)),
        name="ret_sample",
    )(gammas, proj, proj, proj, proj, c_tab, s_tab, ret_gn, state, *extra)


def _swa_sample_kernel(sink_ref, q_ref, kn_ref, vn_ref, ck_ref, cv_ref, *rest, bt):
    o_ref, nk_ref, nv_ref = rest[-3:]
    rows = SWA_KV_HEADS * WINDOW
    sink = sink_ref[...]
    hrow = lax.broadcasted_iota(jnp.int32, (SWA_HEADS, HEAD_DIM), 0)
    first_group = hrow < SWA_GROUP
    srow = lax.broadcasted_iota(jnp.int32, (SWA_HEADS, rows), 0)
    scol = lax.broadcasted_iota(jnp.int32, (SWA_HEADS, rows), 1)
    own = (scol & (SWA_KV_HEADS - 1)) == (srow // SWA_GROUP)

    def body(ib, carry):
        q = q_ref[ib]
        kn = kn_ref[ib]
        vn = vn_ref[ib]
        k_new = jnp.where(first_group, kn[0:1, :], kn[1:2, :])
        v_new = jnp.where(first_group, vn[0:1, :], vn[1:2, :])
        s_new = jnp.sum(q * k_new, axis=-1, keepdims=True) * ATT_SCALE
        ck = ck_ref[ib]
        cv = cv_ref[ib]
        s = jnp.where(own, _dot_nt(q.astype(BF16), ck.astype(BF16)) * ATT_SCALE, -jnp.inf)
        mx = jnp.maximum(jnp.maximum(jnp.max(s, axis=-1, keepdims=True), s_new), sink)
        e = jnp.exp(s - mx)
        e_new = jnp.exp(s_new - mx)
        den = jnp.sum(e, axis=-1, keepdims=True) + e_new + jnp.exp(sink - mx)
        o_ref[ib] = _dot((e / den).astype(BF16), cv.astype(BF16)) + (e_new / den) * v_new
        nk_ref[ib, 0:rows - SWA_KV_HEADS, :] = ck[SWA_KV_HEADS:rows, :]
        nv_ref[ib, 0:rows - SWA_KV_HEADS, :] = cv[SWA_KV_HEADS:rows, :]
        nk_ref[ib, rows - SWA_KV_HEADS:rows, :] = kn
        nv_ref[ib, rows - SWA_KV_HEADS:rows, :] = vn
        return carry

    lax.fori_loop(0, bt, body, 0, unroll=4)


def _swa_sample(q, k_new, v_new, cache_k, cache_v, prev_k, prev_v, sinks, l):
    b = q.shape[0]
    bt = 16
    rows = SWA_KV_HEADS * WINDOW
    blocks = (2 * _nbytes((bt, SWA_HEADS, HEAD_DIM), F32) + 2 * _nbytes((bt, 8, HEAD_DIM), F32)
              + 4 * _nbytes((bt, rows, HEAD_DIM), F32))
    cache_spec = pl.BlockSpec((None, bt, rows, HEAD_DIM), lambda t: (l, t, 0, 0))
    new_spec = pl.BlockSpec((bt, SWA_KV_HEADS, HEAD_DIM), lambda t: (t, 0, 0))
    head_spec = pl.BlockSpec((bt, SWA_HEADS, HEAD_DIM), lambda t: (t, 0, 0))
    extra_k, spec_k = _chain(prev_k)
    extra_v, spec_v = _chain(prev_v)
    aliases = {6: 1, 7: 2} if extra_k else {}
    return pl.pallas_call(
        functools.partial(_swa_sample_kernel, bt=bt),
        out_shape=(jax.ShapeDtypeStruct((b, SWA_HEADS, HEAD_DIM), F32),
                   jax.ShapeDtypeStruct(cache_k.shape, F32), jax.ShapeDtypeStruct(cache_v.shape, F32)),
        grid=(b // bt,),
        in_specs=[pl.BlockSpec((None, SWA_HEADS, 1), lambda t: (l, 0, 0)),
                  head_spec, new_spec, new_spec, cache_spec, cache_spec, *spec_k, *spec_v],
        out_specs=(head_spec, cache_spec, cache_spec),
        input_output_aliases=aliases,
        compiler_params=_params(("parallel",), blocks),
        name="swa_sample",
    )(sinks.reshape(DEPTH, SWA_HEADS, 1), q, k_new, v_new, cache_k, cache_v, *extra_k, *extra_v)


def _cross_sample_kernel(q_ref, mk_ref, mv_ref, o_ref, *, bt):
    rows = MEM_HEADS * MEM_LEN
    srow = lax.broadcasted_iota(jnp.int32, (MEM_HEADS, rows), 0)
    scol = lax.broadcasted_iota(jnp.int32, (MEM_HEADS, rows), 1)
    own = (scol & (MEM_HEADS - 1)) == srow

    def body(ib, carry):
        qb = q_ref[ib].astype(BF16)
        s = jnp.where(own, _dot_nt(qb, mk_ref[ib].astype(BF16)) * ATT_SCALE, -jnp.inf)
        e = jnp.exp(s - jnp.max(s, axis=-1, keepdims=True))
        p = (e / jnp.sum(e, axis=-1, keepdims=True)).astype(BF16)
        o_ref[ib] = _dot(p, mv_ref[ib].astype(BF16))
        return carry

    lax.fori_loop(0, bt, body, 0, unroll=4)


def _cross_sample(q, cache_k, cache_v, l):
    b = q.shape[0]
    bt = 8
    rows = MEM_HEADS * MEM_LEN
    blocks = 2 * _nbytes((bt, 8, HEAD_DIM), F32) + 2 * _nbytes((bt, rows, HEAD_DIM), F32)
    cache_spec = pl.BlockSpec((None, bt, rows, HEAD_DIM), lambda t: (l, t, 0, 0))
    head_spec = pl.BlockSpec((bt, MEM_HEADS, HEAD_DIM), lambda t: (t, 0, 0))
    return pl.pallas_call(
        functools.partial(_cross_sample_kernel, bt=bt),
        out_shape=jax.ShapeDtypeStruct((b, MEM_HEADS, HEAD_DIM), F32),
        grid=(b // bt,),
        in_specs=[head_spec, cache_spec, cache_spec],
        out_specs=head_spec,
        compiler_params=_params(("parallel",), blocks),
        name="cross_sample",
    )(q, cache_k, cache_v)


def kernel(x_prompt, x_sample, mem_prompt, state_ret, cache_win_k, cache_win_v, cache_mem_k, cache_mem_v,
           g_mix, w_in, ret_gn, sinks, w_out, g_cross, g_mem, wq_c, wk_c, wv_c, wo_c,
           g_ffn, w_gate, w_up, w_down, g_final):
    seq = x_prompt.shape[1]
    nb = x_sample.shape[0]
    w_out_b = w_out.astype(BF16)
    wq_b, wk_b, wv_b, wo_b = (w.astype(BF16) for w in (wq_c, wk_c, wv_c, wo_c))
    wgu_b, wd_b = _pair_gate_up(w_gate, w_up), w_down.astype(BF16)
    gain = lambda g: g.reshape(DEPTH, 1, g.shape[-1])
    g_mix, g_cross, g_mem, g_ffn, ret_gn = (gain(g) for g in (g_mix, g_cross, g_mem, g_ffn, ret_gn))

    cp, sp = (jnp.asarray(t) for t in _rope_tables(np.arange(seq)))
    cs, ss = (jnp.asarray(t) for t in _rope_tables(np.array([PAST_LEN])))
    dmask, xi, zeta = (jnp.asarray(t) for t in _decay_tables())
    gammas = jnp.asarray((1.0 - np.exp2(-5.0 - np.arange(RET_HEADS))).astype(np.float32))

    xp = x_prompt[0]
    xs = x_sample[:, 0]
    mem = mem_prompt[0]
    kvw = SWA_KV_HEADS * HEAD_DIM
    win_k = cache_win_k.reshape(DEPTH, nb, WINDOW * SWA_KV_HEADS, HEAD_DIM)
    win_v = cache_win_v.reshape(DEPTH, nb, WINDOW * SWA_KV_HEADS, HEAD_DIM)
    mem_k = cache_mem_k.reshape(DEPTH, nb, MEM_LEN * MEM_HEADS, HEAD_DIM)
    mem_v = cache_mem_v.reshape(DEPTH, nb, MEM_LEN * MEM_HEADS, HEAD_DIM)

    ret_p, wink_p, winv_p, memk_p, memv_p = [], [], [], [], []
    ret_s = wink_s = winv_s = None
    for l in range(DEPTH):
        proj = _in_proj(xp, g_mix, w_in, l)
        ro, r_last = _ret_prompt(proj, cp, sp, dmask, xi, zeta, ret_gn, l)
        so = _swa_prompt(proj, sinks, l)
        x1, qc = _out_proj(ro, so, xp, w_out_b, g_cross, wq_b, l)
        mk, mv = _mem_kv(mem, g_mem, wk_b, wv_b, l)
        x2 = _cross_prompt(qc, mk, mv, wo_b, x1, l)
        xp = _ffn(x2, g_ffn, wgu_b, wd_b, g_final, l)
        ret_p.append(r_last[None])
        tail = proj[seq - WINDOW:]
        wink_p.append(tail[:, COL_SK * LANES:COL_SK * LANES + kvw].reshape(1, WINDOW, SWA_KV_HEADS, HEAD_DIM))
        winv_p.append(tail[:, COL_SV * LANES:COL_SV * LANES + kvw].reshape(1, WINDOW, SWA_KV_HEADS, HEAD_DIM))
        memk_p.append(mk.reshape(1, MEM_LEN, MEM_HEADS, HEAD_DIM))
        memv_p.append(mv.reshape(1, MEM_LEN, MEM_HEADS, HEAD_DIM))

        proj_s = _in_proj(xs, g_mix, w_in, l)
        ro_s, ret_s = _ret_sample(proj_s, state_ret, ret_s, gammas, cs, ss, ret_gn, l)
        sq = proj_s[:, COL_SQ * LANES:COL_SQ * LANES + SWA_W].reshape(nb, SWA_HEADS, HEAD_DIM)
        sk = proj_s[:, COL_SK * LANES:COL_SK * LANES + kvw].reshape(nb, SWA_KV_HEADS, HEAD_DIM)
        sv = proj_s[:, COL_SV * LANES:COL_SV * LANES + kvw].reshape(nb, SWA_KV_HEADS, HEAD_DIM)
        so_s, wink_s, winv_s = _swa_sample(sq, sk, sv, win_k, win_v, wink_s, winv_s, sinks, l)
        x1_s, qc_s = _out_proj(ro_s, so_s.reshape(nb, SWA_W), xs, w_out_b, g_cross, wq_b, l)
        oc_s = _cross_sample(qc_s.reshape(nb, MEM_HEADS, HEAD_DIM), mem_k, mem_v, l)
        x2_s = _matmul_residual(oc_s.reshape(nb, MEM_W), wo_b, x1_s, l)
        xs = _ffn(x2_s, g_ffn, wgu_b, wd_b, g_final, l)

    cache_shape = (DEPTH, nb, WINDOW, SWA_KV_HEADS, HEAD_DIM)
    return (xp[None], xs[:, None],
            jnp.stack(ret_p), jnp.stack(wink_p), jnp.stack(winv_p), jnp.stack(memk_p), jnp.stack(memv_p),
            ret_s, wink_s.reshape(cache_shape), winv_s.reshape(cache_shape))
```

```python
import functools

import numpy as np
import jax
import jax.numpy as jnp
from jax import lax
from jax.experimental import pallas as pl
from jax.experimental.pallas import tpu as pltpu

D_MODEL = 2048
DEPTH = 2
PAST_LEN = 8192
HEAD_DIM = 128
RET_HEADS = 8
RET_W = RET_HEADS * HEAD_DIM
RET_CHUNK = 128
SWA_HEADS = 8
SWA_KV_HEADS = 2
SWA_GROUP = SWA_HEADS // SWA_KV_HEADS
SWA_W = SWA_HEADS * HEAD_DIM
WINDOW = 128
MEM_LEN = 256
MEM_HEADS = 4
MEM_W = MEM_HEADS * HEAD_DIM
IN_W = 4 * RET_W + SWA_W + 2 * SWA_KV_HEADS * HEAD_DIM
D_FF = 5632
ROPE_BASE = 10000.0
EPS = 1e-6
ATT_SCALE = HEAD_DIM ** -0.5

COL_RQ, COL_RK, COL_RV, COL_RG = 0, 8, 16, 24
COL_SQ, COL_SK, COL_SV = 32, 40, 42

V7X_VMEM_BYTES = 64 * 1024 * 1024
V7X_VMEM_BUDGET = V7X_VMEM_BYTES - 8 * 1024 * 1024
LANES = 128
CAST_BLOCK_BYTES = 4 * 1024 * 1024

BF16 = jnp.bfloat16
F32 = jnp.float32


def _vmem_limit(block_bytes, scratch_bytes=0, temp_bytes=0):
    need = 2 * block_bytes + scratch_bytes + temp_bytes + 4 * 1024 * 1024
    return int(min(max(need, 16 * 1024 * 1024), V7X_VMEM_BUDGET))


def _nbytes(shape, dtype):
    return int(np.prod(shape)) * jnp.dtype(dtype).itemsize


def _params(semantics, block_bytes, scratch_bytes=0, temp_bytes=0):
    return pltpu.CompilerParams(
        dimension_semantics=semantics,
        vmem_limit_bytes=_vmem_limit(block_bytes, scratch_bytes, temp_bytes))


def _rms_rows(x, g):
    ms = jnp.mean(x * x, axis=-1, keepdims=True)
    return x * lax.rsqrt(ms + EPS) * g


def _rms_to(dst_ref, src_ref, g_ref, rows):
    g = g_ref[...]
    chunk = min(rows, 128)

    def body(c, carry):
        r = pl.multiple_of(c * chunk, chunk)
        x = src_ref[pl.ds(r, chunk), :]
        dst_ref[pl.ds(r, chunk), :] = _rms_rows(x, g).astype(dst_ref.dtype)
        return carry

    lax.fori_loop(0, rows // chunk, body, 0)


def _sigmoid(x):
    return 1.0 / (1.0 + jnp.exp(-x))


def _dot(a, b):
    return jnp.dot(a, b, preferred_element_type=F32)


def _dot_nt(a, b):
    return lax.dot_general(a, b, (((1,), (1,)), ((), ())), preferred_element_type=F32)


def _gain_spec(l, ngrid):
    zeros = (0,) * 2
    return pl.BlockSpec((None, 1, D_MODEL), lambda *_: (l,) + zeros)


def _cast_kernel(x_ref, o_ref):
    o_ref[...] = x_ref[...].astype(o_ref.dtype)


def _to_bf16(w):
    depth, r, c = w.shape
    tr = r
    while _nbytes((tr, c), F32) > CAST_BLOCK_BYTES and tr % 32 == 0:
        tr //= 2
    spec = pl.BlockSpec((None, tr, c), lambda d, i: (d, i, 0))
    return pl.pallas_call(
        _cast_kernel,
        out_shape=jax.ShapeDtypeStruct(w.shape, BF16),
        grid=(depth, r // tr),
        in_specs=[spec],
        out_specs=spec,
        compiler_params=_params(("parallel", "parallel"), _nbytes((tr, c), F32) + _nbytes((tr, c), BF16)),
        name="to_bf16",
    )(w)


def _in_proj_kernel(x_ref, g_ref, w_ref, o_ref, h_ref, *, tm):
    @pl.when(pl.program_id(1) == 0)
    def _():
        _rms_to(h_ref, x_ref, g_ref, tm)

    o_ref[...] = _dot(h_ref[...], w_ref[...])


def _in_proj(x, g, w, l):
    m = x.shape[0]
    tm = min(m, 1024)
    tn = 512
    blocks = (_nbytes((tm, D_MODEL), F32) + _nbytes((D_MODEL, tn), BF16) + _nbytes((tm, tn), F32))
    return pl.pallas_call(
        functools.partial(_in_proj_kernel, tm=tm),
        out_shape=jax.ShapeDtypeStruct((m, IN_W), F32),
        grid=(m // tm, IN_W // tn),
        in_specs=[
            pl.BlockSpec((tm, D_MODEL), lambda i, j: (i, 0)),
            _gain_spec(l, 2),
            pl.BlockSpec((None, D_MODEL, tn), lambda i, j: (l, 0, j)),
        ],
        out_specs=pl.BlockSpec((tm, tn), lambda i, j: (i, j)),
        scratch_shapes=[pltpu.VMEM((tm, D_MODEL), BF16)],
        compiler_params=_params(("parallel", "arbitrary"), blocks, _nbytes((tm, D_MODEL), BF16)),
        name="in_proj",
    )(x, g, w)


def _out_proj_kernel(ro_ref, so_ref, x_ref, wa_ref, wb_ref, gc_ref, wq_ref, x1_ref, q_ref, h_ref, *, tm):
    y = _dot(ro_ref[...].astype(BF16), wa_ref[...]) + _dot(so_ref[...].astype(BF16), wb_ref[...])
    x1_ref[...] = x_ref[...] + y
    _rms_to(h_ref, x1_ref, gc_ref, tm)
    q_ref[...] = _dot(h_ref[...], wq_ref[...])


def _out_proj(ro, so, x, w_out, g_cross, wq, l):
    m = x.shape[0]
    tm = min(m, 512)
    blocks = (_nbytes((tm, RET_W), ro.dtype) + _nbytes((tm, SWA_W), so.dtype)
              + 2 * _nbytes((tm, D_MODEL), F32) + _nbytes((D_MODEL, D_MODEL), BF16)
              + _nbytes((D_MODEL, MEM_W), BF16) + _nbytes((tm, MEM_W), F32))
    return pl.pallas_call(
        functools.partial(_out_proj_kernel, tm=tm),
        out_shape=(jax.ShapeDtypeStruct((m, D_MODEL), F32), jax.ShapeDtypeStruct((m, MEM_W), F32)),
        grid=(m // tm,),
        in_specs=[
            pl.BlockSpec((tm, RET_W), lambda i: (i, 0)),
            pl.BlockSpec((tm, SWA_W), lambda i: (i, 0)),
            pl.BlockSpec((tm, D_MODEL), lambda i: (i, 0)),
            pl.BlockSpec((None, RET_W, D_MODEL), lambda i: (l, 0, 0)),
            pl.BlockSpec((None, SWA_W, D_MODEL), lambda i: (l, 1, 0)),
            _gain_spec(l, 1),
            pl.BlockSpec((None, D_MODEL, MEM_W), lambda i: (l, 0, 0)),
        ],
        out_specs=(pl.BlockSpec((tm, D_MODEL), lambda i: (i, 0)),
                   pl.BlockSpec((tm, MEM_W), lambda i: (i, 0))),
        scratch_shapes=[pltpu.VMEM((tm, D_MODEL), BF16)],
        compiler_params=_params(("parallel",), blocks, _nbytes((tm, D_MODEL), BF16),
                                _nbytes((tm, D_MODEL), F32)),
        name="out_proj",
    )(ro, so, x, w_out, w_out, g_cross, wq)


def _ffn_kernel(x_ref, g_ref, wg_ref, wu_ref, wd_ref, gf_ref, o_ref, h_ref, acc_ref, *, tm, nf, final_norm):
    j = pl.program_id(1)

    @pl.when(j == 0)
    def _():
        _rms_to(h_ref, x_ref, g_ref, tm)
        acc_ref[...] = jnp.zeros_like(acc_ref)

    h = h_ref[...]
    a = _dot(h, wg_ref[...])
    u = _dot(h, wu_ref[...])
    act = (a * _sigmoid(a) * u).astype(BF16)
    acc_ref[...] += _dot(act, wd_ref[...])

    @pl.when(j == nf - 1)
    def _():
        o_ref[...] = x_ref[...] + acc_ref[...]
        if final_norm:
            _rms_to(o_ref, o_ref, gf_ref, tm)


def _ffn(x, g, wg, wu, wd, g_final, l):
    m = x.shape[0]
    tm = min(m, 512)
    tf = 512
    nf = D_FF // tf
    blocks = (2 * _nbytes((tm, D_MODEL), F32) + 3 * _nbytes((D_MODEL, tf), BF16))
    scratch = _nbytes((tm, D_MODEL), BF16) + _nbytes((tm, D_MODEL), F32)
    return pl.pallas_call(
        functools.partial(_ffn_kernel, tm=tm, nf=nf, final_norm=(l == DEPTH - 1)),
        out_shape=jax.ShapeDtypeStruct((m, D_MODEL), F32),
        grid=(m // tm, nf),
        in_specs=[
            pl.BlockSpec((tm, D_MODEL), lambda i, j: (i, 0)),
            _gain_spec(l, 2),
            pl.BlockSpec((None, D_MODEL, tf), lambda i, j: (l, 0, j)),
            pl.BlockSpec((None, D_MODEL, tf), lambda i, j: (l, 0, j)),
            pl.BlockSpec((None, tf, D_MODEL), lambda i, j: (l, j, 0)),
            pl.BlockSpec((1, D_MODEL), lambda i, j: (0, 0)),
        ],
        out_specs=pl.BlockSpec((tm, D_MODEL), lambda i, j: (i, 0)),
        scratch_shapes=[pltpu.VMEM((tm, D_MODEL), BF16), pltpu.VMEM((tm, D_MODEL), F32)],
        compiler_params=_params(("parallel", "arbitrary"), blocks, scratch,
                                3 * _nbytes((tm, tf), F32) + _nbytes((tm, D_MODEL), F32)),
        name="ffn",
    )(x, g, wg, wu, wd, g_final.reshape(1, D_MODEL))


def _matmul_residual_kernel(a_ref, w_ref, x_ref, o_ref):
    o_ref[...] = x_ref[...] + _dot(a_ref[...].astype(BF16), w_ref[...])


def _matmul_residual(a, w, x, l):
    m, k = a.shape
    n = w.shape[2]
    blocks = _nbytes((m, k), a.dtype) + _nbytes((k, n), BF16) + 2 * _nbytes((m, n), F32)
    return pl.pallas_call(
        _matmul_residual_kernel,
        out_shape=jax.ShapeDtypeStruct((m, n), F32),
        grid=(1,),
        in_specs=[pl.BlockSpec((m, k), lambda i: (0, 0)),
                  pl.BlockSpec((None, k, n), lambda i: (l, 0, 0)),
                  pl.BlockSpec((m, n), lambda i: (0, 0))],
        out_specs=pl.BlockSpec((m, n), lambda i: (0, 0)),
        compiler_params=_params(("arbitrary",), blocks),
        name="matmul_residual",
    )(a, w, x)


def _rope_tables(positions):
    half = HEAD_DIM // 2
    inv = ROPE_BASE ** (-np.arange(half, dtype=np.float64) / half)
    ang = np.asarray(positions, np.float64)[:, None] * inv[None, :]
    cos, sin = np.cos(ang), np.sin(ang)
    c = np.concatenate([cos, cos], axis=-1).astype(np.float32)
    s = np.concatenate([-sin, sin], axis=-1).astype(np.float32)
    return c, s


def _decay_tables():
    c = RET_CHUNK
    lg = np.log1p(-np.exp2(-5.0 - np.arange(RET_HEADS, dtype=np.float64)))
    idx = np.arange(c, dtype=np.float64)
    diff = idx[:, None] - idx[None, :]
    dmask = np.where(diff[None] >= 0, np.exp(lg[:, None, None] * np.maximum(diff, 0.0)[None]), 0.0)
    xi = np.exp(lg[:, None] * (idx + 1.0)[None])
    zeta = np.exp(lg[:, None] * (c - 1.0 - idx)[None])
    bcast = lambda t: np.broadcast_to(t[:, :, None], (RET_HEADS, c, LANES)).astype(np.float32)
    return dmask.astype(np.float32), bcast(xi), bcast(zeta)


def _rotary(x, c, s):
    return x * c + pltpu.roll(x, HEAD_DIM // 2, 1) * s


def _ret_prompt_kernel(q_ref, k_ref, v_ref, g_ref, c_ref, s_ref, dm_ref, xi_ref, zt_ref, gn_ref,
                       o_ref, st_ref, *, nchunk):
    @pl.when(pl.program_id(1) == 0)
    def _():
        st_ref[...] = jnp.zeros_like(st_ref)

    c_tab = c_ref[...]
    s_tab = s_ref[...]
    qr = _rotary(q_ref[...], c_tab, s_tab)
    kr = _rotary(k_ref[...], c_tab, s_tab) * ATT_SCALE
    dmask = dm_ref[...]
    xi = xi_ref[...]
    zeta = zt_ref[...]
    g_chunk = xi[RET_CHUNK - 1:RET_CHUNK, :]
    gn = gn_ref[...]
    state = st_ref[...]
    for c in range(nchunk):
        rows = slice(c * RET_CHUNK, (c + 1) * RET_CHUNK)
        qc = qr[rows]
        kc = kr[rows]
        vc = v_ref[rows, :].astype(BF16)
        s = _dot_nt(qc.astype(BF16), kc.astype(BF16)) * dmask
        inner = _dot(s.astype(BF16), vc)
        cross = _dot((qc * xi).astype(BF16), state.astype(BF16))
        kv = _dot((kc * zeta).T.astype(BF16), vc)
        state = g_chunk * state + kv
        out = inner + cross
        out = out * lax.rsqrt(jnp.mean(out * out, axis=-1, keepdims=True) + EPS) * gn
        gate = g_ref[rows, :]
        o_ref[rows, :] = (gate * _sigmoid(gate) * out).astype(o_ref.dtype)
    st_ref[...] = state


def _ret_prompt(proj, c_tab, s_tab, dmask, xi, zeta, ret_gn, l):
    m = proj.shape[0]
    r = 1024
    nchunk = r // RET_CHUNK
    col = lambda off: pl.BlockSpec((r, HEAD_DIM), lambda h, i: (i, off + h))
    head_tab = pl.BlockSpec((None, RET_CHUNK, LANES), lambda h, i: (h, 0, 0))
    blocks = (6 * _nbytes((r, HEAD_DIM), F32) + 3 * _nbytes((RET_CHUNK, LANES), F32)
              + _nbytes((r, HEAD_DIM), BF16) + _nbytes((HEAD_DIM, HEAD_DIM), F32))
    return pl.pallas_call(
        functools.partial(_ret_prompt_kernel, nchunk=nchunk),
        out_shape=(jax.ShapeDtypeStruct((m, RET_W), BF16),
                   jax.ShapeDtypeStruct((RET_HEADS, HEAD_DIM, HEAD_DIM), F32)),
        grid=(RET_HEADS, m // r),
        in_specs=[col(COL_RQ), col(COL_RK), col(COL_RV), col(COL_RG),
                  pl.BlockSpec((r, HEAD_DIM), lambda h, i: (i, 0)),
                  pl.BlockSpec((r, HEAD_DIM), lambda h, i: (i, 0)),
                  head_tab, head_tab, head_tab,
                  pl.BlockSpec((None, 1, HEAD_DIM), lambda h, i: (l, 0, h))],
        out_specs=(pl.BlockSpec((r, HEAD_DIM), lambda h, i: (i, h)),
                   pl.BlockSpec((None, HEAD_DIM, HEAD_DIM), lambda h, i: (h, 0, 0))),
        compiler_params=_params(("parallel", "arbitrary"), blocks, 0, 4 * _nbytes((r, HEAD_DIM), F32)),
        name="ret_prompt",
    )(proj, proj, proj, proj, c_tab, s_tab, dmask, xi, zeta, ret_gn)


def _swa_prompt_kernel(sink_ref, q_ref, k_ref, v_ref, kp_ref, vp_ref, o_ref, *, nblk, l):
    kvh = pl.program_id(0)
    i = pl.program_id(1)
    kfull = jnp.concatenate([kp_ref[...], k_ref[...]], axis=0).astype(BF16)
    vfull = jnp.concatenate([vp_ref[...], v_ref[...]], axis=0).astype(BF16)
    row = lax.broadcasted_iota(jnp.int32, (WINDOW, 2 * WINDOW), 0)
    col = lax.broadcasted_iota(jnp.int32, (WINDOW, 2 * WINDOW), 1)
    band = (col >= row) & (col <= row + WINDOW)
    for j in range(nblk):
        rows = slice(j * WINDOW, (j + 1) * WINDOW)
        kk = kfull[j * WINDOW:(j + 2) * WINDOW]
        vv = vfull[j * WINDOW:(j + 2) * WINDOW]
        if j == 0:
            mask = band & (col >= jnp.where(i > 0, 0, WINDOW))
        else:
            mask = band
        qj = q_ref[rows, :].astype(BF16)
        qs = jnp.concatenate([qj[:, h * HEAD_DIM:(h + 1) * HEAD_DIM] for h in range(SWA_GROUP)], axis=0)
        s = _dot_nt(qs, kk) * ATT_SCALE
        ps = []
        for h in range(SWA_GROUP):
            sh = jnp.where(mask, s[h * WINDOW:(h + 1) * WINDOW], -jnp.inf)
            sink = sink_ref[l, kvh * SWA_GROUP + h]
            mx = jnp.maximum(jnp.max(sh, axis=-1, keepdims=True), sink)
            e = jnp.exp(sh - mx)
            den = jnp.sum(e, axis=-1, keepdims=True) + jnp.exp(sink - mx)
            ps.append((e / den).astype(BF16))
        o = _dot(jnp.concatenate(ps, axis=0), vv)
        for h in range(SWA_GROUP):
            o_ref[rows, h * HEAD_DIM:(h + 1) * HEAD_DIM] = o[h * WINDOW:(h + 1) * WINDOW].astype(o_ref.dtype)


def _swa_prompt(proj, sinks, l):
    m = proj.shape[0]
    r = 512
    nblk = r // WINDOW
    gw = SWA_GROUP * HEAD_DIM
    prev = lambda off: pl.BlockSpec(
        (WINDOW, HEAD_DIM), lambda kv, i: (jnp.maximum(i * nblk - 1, 0), off + kv))
    blocks = (_nbytes((r, gw), F32) + 2 * _nbytes((r, HEAD_DIM), F32) + 2 * _nbytes((WINDOW, HEAD_DIM), F32)
              + _nbytes((r, gw), BF16))
    return pl.pallas_call(
        functools.partial(_swa_prompt_kernel, nblk=nblk, l=l),
        out_shape=jax.ShapeDtypeStruct((m, SWA_W), BF16),
        grid=(SWA_KV_HEADS, m // r),
        in_specs=[pl.BlockSpec(memory_space=pltpu.SMEM),
                  pl.BlockSpec((r, gw), lambda kv, i: (i, COL_SQ // SWA_GROUP + kv)),
                  pl.BlockSpec((r, HEAD_DIM), lambda kv, i: (i, COL_SK + kv)),
                  pl.BlockSpec((r, HEAD_DIM), lambda kv, i: (i, COL_SV + kv)),
                  prev(COL_SK), prev(COL_SV)],
        out_specs=pl.BlockSpec((r, gw), lambda kv, i: (i, kv)),
        compiler_params=_params(("parallel", "parallel"), blocks, 0, 8 * _nbytes((gw, 2 * WINDOW), F32)),
        name="swa_prompt",
    )(sinks, proj, proj, proj, proj, proj)


def _mem_kv_kernel(m_ref, g_ref, wk_ref, wv_ref, k_ref, v_ref):
    h = _rms_rows(m_ref[...], g_ref[...]).astype(BF16)
    k_ref[...] = _dot(h, wk_ref[...])
    v_ref[...] = _dot(h, wv_ref[...])


def _mem_kv(mem, g, wk, wv, l):
    full = lambda shape: pl.BlockSpec(shape, lambda i: (0,) * len(shape))
    layer = lambda shape: pl.BlockSpec((None,) + shape, lambda i: (l,) + (0,) * len(shape))
    blocks = (_nbytes((MEM_LEN, D_MODEL), F32) + 2 * _nbytes((D_MODEL, MEM_W), BF16)
              + 2 * _nbytes((MEM_LEN, MEM_W), F32))
    return pl.pallas_call(
        _mem_kv_kernel,
        out_shape=(jax.ShapeDtypeStruct((MEM_LEN, MEM_W), F32),) * 2,
        grid=(1,),
        in_specs=[full((MEM_LEN, D_MODEL)), _gain_spec(l, 1), layer((D_MODEL, MEM_W)), layer((D_MODEL, MEM_W))],
        out_specs=(full((MEM_LEN, MEM_W)),) * 2,
        compiler_params=_params(("arbitrary",), blocks, 0, _nbytes((MEM_LEN, D_MODEL), F32)),
        name="mem_kv",
    )(mem, g, wk, wv)


def _cross_prompt_kernel(q_ref, mk_ref, mv_ref, wo_ref, x_ref, o_ref):
    q = q_ref[...].astype(BF16)
    mk = mk_ref[...].astype(BF16)
    mv = mv_ref[...].astype(BF16)
    outs = []
    for h in range(MEM_HEADS):
        cols = slice(h * HEAD_DIM, (h + 1) * HEAD_DIM)
        s = _dot_nt(q[:, cols], mk[:, cols]) * ATT_SCALE
        e = jnp.exp(s - jnp.max(s, axis=-1, keepdims=True))
        p = e / jnp.sum(e, axis=-1, keepdims=True)
        outs.append(_dot(p.astype(BF16), mv[:, cols]).astype(BF16))
    o_ref[...] = x_ref[...] + _dot(jnp.concatenate(outs, axis=1), wo_ref[...])


def _cross_prompt(q, mk, mv, wo, x, l):
    m = x.shape[0]
    tm = 512
    blocks = (_nbytes((tm, MEM_W), F32) + 2 * _nbytes((MEM_LEN, MEM_W), F32) + _nbytes((MEM_W, D_MODEL), BF16)
              + 2 * _nbytes((tm, D_MODEL), F32))
    return pl.pallas_call(
        _cross_prompt_kernel,
        out_shape=jax.ShapeDtypeStruct((m, D_MODEL), F32),
        grid=(m // tm,),
        in_specs=[pl.BlockSpec((tm, MEM_W), lambda i: (i, 0)),
                  pl.BlockSpec((MEM_LEN, MEM_W), lambda i: (0, 0)),
                  pl.BlockSpec((MEM_LEN, MEM_W), lambda i: (0, 0)),
                  pl.BlockSpec((None, MEM_W, D_MODEL), lambda i: (l, 0, 0)),
                  pl.BlockSpec((tm, D_MODEL), lambda i: (i, 0))],
        out_specs=pl.BlockSpec((tm, D_MODEL), lambda i: (i, 0)),
        compiler_params=_params(("parallel",), blocks, 0, 2 * _nbytes((tm, D_MODEL), F32)),
        name="cross_prompt",
    )(q, mk, mv, wo, x)


def _chain(prev):
    if prev is None:
        return (), ()
    return (prev,), (pl.BlockSpec(memory_space=pl.ANY),)


def _ret_sample_kernel(gam_ref, q_ref, k_ref, v_ref, g_ref, c_ref, s_ref, gn_ref, st_ref, *rest, bt, nt):
    o_ref, nst_ref, qt_ref, kt_ref = rest[-4:]
    h = pl.program_id(0)
    t = pl.program_id(1)

    @pl.when(t == 0)
    def _():
        c_tab = c_ref[...]
        s_tab = s_ref[...]
        qt = _rotary(q_ref[...], c_tab, s_tab).T
        kt = (_rotary(k_ref[...], c_tab, s_tab) * ATT_SCALE).T
        for tt in range(nt):
            shift = (LANES - tt * bt) % LANES
            qt_ref[tt] = pltpu.roll(qt, shift, 1) if shift else qt
            kt_ref[tt] = pltpu.roll(kt, shift, 1) if shift else kt

    gamma = gam_ref[h]
    qt = qt_ref[t]
    kt = kt_ref[t]
    outs = []
    for ib in range(bt):
        new = gamma * st_ref[ib] + kt[:, ib:ib + 1] * v_ref[ib:ib + 1, :]
        nst_ref[ib] = new
        outs.append(jnp.sum(qt[:, ib:ib + 1] * new, axis=0, keepdims=True))
    out = jnp.concatenate(outs, axis=0)
    out = out * lax.rsqrt(jnp.mean(out * out, axis=-1, keepdims=True) + EPS) * gn_ref[...]
    gate = g_ref[...]
    o_ref[...] = gate * _sigmoid(gate) * out


def _ret_sample(proj, state, prev, gammas, c_tab, s_tab, ret_gn, l):
    b = proj.shape[0]
    bt = 32
    nt = b // bt
    col = lambda off: pl.BlockSpec((b, HEAD_DIM), lambda h, t: (0, off + h))
    tile = lambda off: pl.BlockSpec((bt, HEAD_DIM), lambda h, t: (t, off + h))
    row = pl.BlockSpec((1, HEAD_DIM), lambda h, t: (0, 0))
    st_spec = pl.BlockSpec((None, bt, None, HEAD_DIM, HEAD_DIM), lambda h, t: (l, t, h, 0, 0))
    extra, extra_specs = _chain(prev)
    n_in = 9 + len(extra)
    blocks = (2 * _nbytes((b, HEAD_DIM), F32) + 3 * _nbytes((bt, HEAD_DIM), F32)
              + 2 * _nbytes((bt, HEAD_DIM, HEAD_DIM), F32))
    scratch = 2 * _nbytes((nt, HEAD_DIM, LANES), F32)
    return pl.pallas_call(
        functools.partial(_ret_sample_kernel, bt=bt, nt=nt),
        out_shape=(jax.ShapeDtypeStruct((b, RET_W), F32), jax.ShapeDtypeStruct(state.shape, F32)),
        grid=(RET_HEADS, nt),
        in_specs=[pl.BlockSpec(memory_space=pltpu.SMEM),
                  col(COL_RQ), col(COL_RK), tile(COL_RV), tile(COL_RG), row, row,
                  pl.BlockSpec((None, 1, HEAD_DIM), lambda h, t: (l, 0, h)),
                  st_spec, *extra_specs],
        out_specs=(tile(0), st_spec),
        scratch_shapes=[pltpu.VMEM((nt, HEAD_DIM, LANES), F32), pltpu.VMEM((nt, HEAD_DIM, LANES), F32)],
        input_output_aliases=({n_in - 1: 1} if extra else {}),
        compiler_params=_params(("parallel", "arbitrary"), blocks, scratch, 4 * _nbytes((bt, HEAD_DIM, HEAD_DIM), F32)),
        name="ret_sample",
    )(gammas, proj, proj, proj, proj, c_tab, s_tab, ret_gn, state, *extra)


def _swa_sample_kernel(sink_ref, q_ref, kn_ref, vn_ref, ck_ref, cv_ref, *rest, bt):
    o_ref, nk_ref, nv_ref = rest[-3:]
    rows = SWA_KV_HEADS * WINDOW
    sink = sink_ref[...]
    hrow = lax.broadcasted_iota(jnp.int32, (SWA_HEADS, HEAD_DIM), 0)
    first_group = hrow < SWA_GROUP
    srow = lax.broadcasted_iota(jnp.int32, (SWA_HEADS, rows), 0)
    scol = lax.broadcasted_iota(jnp.int32, (SWA_HEADS, rows), 1)
    own = (scol & (SWA_KV_HEADS - 1)) == (srow // SWA_GROUP)

    def body(ib, carry):
        q = q_ref[ib]
        kn = kn_ref[ib]
        vn = vn_ref[ib]
        k_new = jnp.where(first_group, kn[0:1, :], kn[1:2, :])
        v_new = jnp.where(first_group, vn[0:1, :], vn[1:2, :])
        s_new = jnp.sum(q * k_new, axis=-1, keepdims=True) * ATT_SCALE
        ck = ck_ref[ib]
        cv = cv_ref[ib]
        s = jnp.where(own, _dot_nt(q.astype(BF16), ck.astype(BF16)) * ATT_SCALE, -jnp.inf)
        mx = jnp.maximum(jnp.maximum(jnp.max(s, axis=-1, keepdims=True), s_new), sink)
        e = jnp.exp(s - mx)
        e_new = jnp.exp(s_new - mx)
        den = jnp.sum(e, axis=-1, keepdims=True) + e_new + jnp.exp(sink - mx)
        o_ref[ib] = _dot((e / den).astype(BF16), cv.astype(BF16)) + (e_new / den) * v_new
        nk_ref[ib, 0:rows - SWA_KV_HEADS, :] = ck[SWA_KV_HEADS:rows, :]
        nv_ref[ib, 0:rows - SWA_KV_HEADS, :] = cv[SWA_KV_HEADS:rows, :]
        nk_ref[ib, rows - SWA_KV_HEADS:rows, :] = kn
        nv_ref[ib, rows - SWA_KV_HEADS:rows, :] = vn
        return carry

    lax.fori_loop(0, bt, body, 0, unroll=4)


def _swa_sample(q, k_new, v_new, cache_k, cache_v, prev_k, prev_v, sinks, l):
    b = q.shape[0]
    bt = 16
    rows = SWA_KV_HEADS * WINDOW
    blocks = (2 * _nbytes((bt, SWA_HEADS, HEAD_DIM), F32) + 2 * _nbytes((bt, 8, HEAD_DIM), F32)
              + 4 * _nbytes((bt, rows, HEAD_DIM), F32))
    cache_spec = pl.BlockSpec((None, bt, rows, HEAD_DIM), lambda t: (l, t, 0, 0))
    new_spec = pl.BlockSpec((bt, SWA_KV_HEADS, HEAD_DIM), lambda t: (t, 0, 0))
    head_spec = pl.BlockSpec((bt, SWA_HEADS, HEAD_DIM), lambda t: (t, 0, 0))
    extra_k, spec_k = _chain(prev_k)
    extra_v, spec_v = _chain(prev_v)
    aliases = {6: 1, 7: 2} if extra_k else {}
    return pl.pallas_call(
        functools.partial(_swa_sample_kernel, bt=bt),
        out_shape=(jax.ShapeDtypeStruct((b, SWA_HEADS, HEAD_DIM), F32),
                   jax.ShapeDtypeStruct(cache_k.shape, F32), jax.ShapeDtypeStruct(cache_v.shape, F32)),
        grid=(b // bt,),
        in_specs=[pl.BlockSpec((None, SWA_HEADS, 1), lambda t: (l, 0, 0)),
                  head_spec, new_spec, new_spec, cache_spec, cache_spec, *spec_k, *spec_v],
        out_specs=(head_spec, cache_spec, cache_spec),
        input_output_aliases=aliases,
        compiler_params=_params(("parallel",), blocks),
        name="swa_sample",
    )(sinks.reshape(DEPTH, SWA_HEADS, 1), q, k_new, v_new, cache_k, cache_v, *extra_k, *extra_v)


def _cross_sample_kernel(q_ref, mk_ref, mv_ref, o_ref, *, bt):
    rows = MEM_HEADS * MEM_LEN
    srow = lax.broadcasted_iota(jnp.int32, (MEM_HEADS, rows), 0)
    scol = lax.broadcasted_iota(jnp.int32, (MEM_HEADS, rows), 1)
    own = (scol & (MEM_HEADS - 1)) == srow

    def body(ib, carry):
        qb = q_ref[ib].astype(BF16)
        s = jnp.where(own, _dot_nt(qb, mk_ref[ib].astype(BF16)) * ATT_SCALE, -jnp.inf)
        e = jnp.exp(s - jnp.max(s, axis=-1, keepdims=True))
        p = (e / jnp.sum(e, axis=-1, keepdims=True)).astype(BF16)
        o_ref[ib] = _dot(p, mv_ref[ib].astype(BF16))
        return carry

    lax.fori_loop(0, bt, body, 0, unroll=4)


def _cross_sample(q, cache_k, cache_v, l):
    b = q.shape[0]
    bt = 8
    rows = MEM_HEADS * MEM_LEN
    blocks = 2 * _nbytes((bt, 8, HEAD_DIM), F32) + 2 * _nbytes((bt, rows, HEAD_DIM), F32)
    cache_spec = pl.BlockSpec((None, bt, rows, HEAD_DIM), lambda t: (l, t, 0, 0))
    head_spec = pl.BlockSpec((bt, MEM_HEADS, HEAD_DIM), lambda t: (t, 0, 0))
    return pl.pallas_call(
        functools.partial(_cross_sample_kernel, bt=bt),
        out_shape=jax.ShapeDtypeStruct((b, MEM_HEADS, HEAD_DIM), F32),
        grid=(b // bt,),
        in_specs=[head_spec, cache_spec, cache_spec],
        out_specs=head_spec,
        compiler_params=_params(("parallel",), blocks),
        name="cross_sample",
    )(q, cache_k, cache_v)


def kernel(x_prompt, x_sample, mem_prompt, state_ret, cache_win_k, cache_win_v, cache_mem_k, cache_mem_v,
           g_mix, w_in, ret_gn, sinks, w_out, g_cross, g_mem, wq_c, wk_c, wv_c, wo_c,
           g_ffn, w_gate, w_up, w_down, g_final):
    seq = x_prompt.shape[1]
    nb = x_sample.shape[0]
    w_in_b, w_out_b = _to_bf16(w_in), _to_bf16(w_out)
    wq_b, wk_b, wv_b, wo_b = (_to_bf16(w) for w in (wq_c, wk_c, wv_c, wo_c))
    wg_b, wu_b, wd_b = (_to_bf16(w) for w in (w_gate, w_up, w_down))
    gain = lambda g: g.reshape(DEPTH, 1, g.shape[-1])
    g_mix, g_cross, g_mem, g_ffn, ret_gn = (gain(g) for g in (g_mix, g_cross, g_mem, g_ffn, ret_gn))

    cp, sp = (jnp.asarray(t) for t in _rope_tables(np.arange(seq)))
    cs, ss = (jnp.asarray(t) for t in _rope_tables(np.array([PAST_LEN])))
    dmask, xi, zeta = (jnp.asarray(t) for t in _decay_tables())
    gammas = jnp.asarray((1.0 - np.exp2(-5.0 - np.arange(RET_HEADS))).astype(np.float32))

    xp = x_prompt[0]
    xs = x_sample[:, 0]
    mem = mem_prompt[0]
    kvw = SWA_KV_HEADS * HEAD_DIM
    win_k = cache_win_k.reshape(DEPTH, nb, WINDOW * SWA_KV_HEADS, HEAD_DIM)
    win_v = cache_win_v.reshape(DEPTH, nb, WINDOW * SWA_KV_HEADS, HEAD_DIM)
    mem_k = cache_mem_k.reshape(DEPTH, nb, MEM_LEN * MEM_HEADS, HEAD_DIM)
    mem_v = cache_mem_v.reshape(DEPTH, nb, MEM_LEN * MEM_HEADS, HEAD_DIM)

    ret_p, wink_p, winv_p, memk_p, memv_p = [], [], [], [], []
    ret_s = wink_s = winv_s = None
    for l in range(DEPTH):
        proj = _in_proj(xp, g_mix, w_in_b, l)
        ro, r_last = _ret_prompt(proj, cp, sp, dmask, xi, zeta, ret_gn, l)
        so = _swa_prompt(proj, sinks, l)
        x1, qc = _out_proj(ro, so, xp, w_out_b, g_cross, wq_b, l)
        mk, mv = _mem_kv(mem, g_mem, wk_b, wv_b, l)
        x2 = _cross_prompt(qc, mk, mv, wo_b, x1, l)
        xp = _ffn(x2, g_ffn, wg_b, wu_b, wd_b, g_final, l)
        ret_p.append(r_last[None])
        tail = proj[seq - WINDOW:]
        wink_p.append(tail[:, COL_SK * LANES:COL_SK * LANES + kvw].reshape(1, WINDOW, SWA_KV_HEADS, HEAD_DIM))
        winv_p.append(tail[:, COL_SV * LANES:COL_SV * LANES + kvw].reshape(1, WINDOW, SWA_KV_HEADS, HEAD_DIM))
        memk_p.append(mk.reshape(1, MEM_LEN, MEM_HEADS, HEAD_DIM))
        memv_p.append(mv.reshape(1, MEM_LEN, MEM_HEADS, HEAD_DIM))

        proj_s = _in_proj(xs, g_mix, w_in_b, l)
        ro_s, ret_s = _ret_sample(proj_s, state_ret, ret_s, gammas, cs, ss, ret_gn, l)
        sq = proj_s[:, COL_SQ * LANES:COL_SQ * LANES + SWA_W].reshape(nb, SWA_HEADS, HEAD_DIM)
        sk = proj_s[:, COL_SK * LANES:COL_SK * LANES + kvw].reshape(nb, SWA_KV_HEADS, HEAD_DIM)
        sv = proj_s[:, COL_SV * LANES:COL_SV * LANES + kvw].reshape(nb, SWA_KV_HEADS, HEAD_DIM)
        so_s, wink_s, winv_s = _swa_sample(sq, sk, sv, win_k, win_v, wink_s, winv_s, sinks, l)
        x1_s, qc_s = _out_proj(ro_s, so_s.reshape(nb, SWA_W), xs, w_out_b, g_cross, wq_b, l)
        oc_s = _cross_sample(qc_s.reshape(nb, MEM_HEADS, HEAD_DIM), mem_k, mem_v, l)
        x2_s = _matmul_residual(oc_s.reshape(nb, MEM_W), wo_b, x1_s, l)
        xs = _ffn(x2_s, g_ffn, wg_b, wu_b, wd_b, g_final, l)

    cache_shape = (DEPTH, nb, WINDOW, SWA_KV_HEADS, HEAD_DIM)
    return (xp[None], xs[:, None],
            jnp.stack(ret_p), jnp.stack(wink_p), jnp.stack(winv_p), jnp.stack(memk_p), jnp.stack(memv_p),
            ret_s, wink_s.reshape(cache_shape), winv_s.reshape(cache_shape))
```

```python
import functools

import numpy as np
import jax
import jax.numpy as jnp
from jax import lax
from jax.experimental import pallas as pl
from jax.experimental.pallas import tpu as pltpu

D_MODEL = 2048
DEPTH = 2
PAST_LEN = 8192
HEAD_DIM = 128
RET_HEADS = 8
RET_W = RET_HEADS * HEAD_DIM
RET_CHUNK = 128
SWA_HEADS = 8
SWA_KV_HEADS = 2
SWA_GROUP = SWA_HEADS // SWA_KV_HEADS
SWA_W = SWA_HEADS * HEAD_DIM
WINDOW = 128
MEM_LEN = 256
MEM_HEADS = 4
MEM_W = MEM_HEADS * HEAD_DIM
IN_W = 4 * RET_W + SWA_W + 2 * SWA_KV_HEADS * HEAD_DIM
D_FF = 5632
ROPE_BASE = 10000.0
EPS = 1e-6
ATT_SCALE = HEAD_DIM ** -0.5

COL_RQ, COL_RK, COL_RV, COL_RG = 0, 8, 16, 24
COL_SQ, COL_SK, COL_SV = 32, 40, 42

V7X_VMEM_BYTES = 64 * 1024 * 1024
V7X_VMEM_BUDGET = V7X_VMEM_BYTES - 8 * 1024 * 1024
LANES = 128
CAST_BLOCK_BYTES = 4 * 1024 * 1024

BF16 = jnp.bfloat16
F32 = jnp.float32


def _vmem_limit(block_bytes, scratch_bytes=0, temp_bytes=0):
    need = 2 * block_bytes + scratch_bytes + temp_bytes + 4 * 1024 * 1024
    return int(min(max(need, 16 * 1024 * 1024), V7X_VMEM_BUDGET))


def _nbytes(shape, dtype):
    return int(np.prod(shape)) * jnp.dtype(dtype).itemsize


def _params(semantics, block_bytes, scratch_bytes=0, temp_bytes=0):
    return pltpu.CompilerParams(
        dimension_semantics=semantics,
        vmem_limit_bytes=_vmem_limit(block_bytes, scratch_bytes, temp_bytes))


def _rms_rows(x, g):
    ms = jnp.mean(x * x, axis=-1, keepdims=True)
    return x * lax.rsqrt(ms + EPS) * g


def _rms_to(dst_ref, src_ref, g_ref, rows):
    g = g_ref[...]
    chunk = min(rows, 128)

    def body(c, carry):
        r = pl.multiple_of(c * chunk, chunk)
        x = src_ref[pl.ds(r, chunk), :]
        dst_ref[pl.ds(r, chunk), :] = _rms_rows(x, g).astype(dst_ref.dtype)
        return carry

    lax.fori_loop(0, rows // chunk, body, 0)


def _sigmoid(x):
    return 1.0 / (1.0 + jnp.exp(-x))


def _dot(a, b):
    return jnp.dot(a, b, preferred_element_type=F32)


def _dot_nt(a, b):
    return lax.dot_general(a, b, (((1,), (1,)), ((), ())), preferred_element_type=F32)


def _gain_spec(l, ngrid):
    zeros = (0,) * 2
    return pl.BlockSpec((None, 1, D_MODEL), lambda *_: (l,) + zeros)


def _cast_kernel(x_ref, o_ref):
    o_ref[...] = x_ref[...].astype(o_ref.dtype)


def _to_bf16(w):
    depth, r, c = w.shape
    tr = r
    while _nbytes((tr, c), F32) > CAST_BLOCK_BYTES and tr % 32 == 0:
        tr //= 2
    spec = pl.BlockSpec((None, tr, c), lambda d, i: (d, i, 0))
    return pl.pallas_call(
        _cast_kernel,
        out_shape=jax.ShapeDtypeStruct(w.shape, BF16),
        grid=(depth, r // tr),
        in_specs=[spec],
        out_specs=spec,
        compiler_params=_params(("parallel", "parallel"), _nbytes((tr, c), F32) + _nbytes((tr, c), BF16)),
        name="to_bf16",
    )(w)


def _cast_tiles_kernel(x_ref, o_ref, *, nt, tn):
    for j in range(nt):
        o_ref[j] = x_ref[:, j * tn:(j + 1) * tn].astype(o_ref.dtype)


def _to_bf16_tiles(w, tn):
    depth, r, c = w.shape
    nt = c // tn
    tr = r
    while _nbytes((tr, c), F32) > CAST_BLOCK_BYTES and tr % 32 == 0:
        tr //= 2
    return pl.pallas_call(
        functools.partial(_cast_tiles_kernel, nt=nt, tn=tn),
        out_shape=jax.ShapeDtypeStruct((depth, nt, r, tn), BF16),
        grid=(depth, r // tr),
        in_specs=[pl.BlockSpec((None, tr, c), lambda d, i: (d, i, 0))],
        out_specs=pl.BlockSpec((None, nt, tr, tn), lambda d, i: (d, 0, i, 0)),
        compiler_params=_params(("parallel", "parallel"), _nbytes((tr, c), F32) + _nbytes((tr, c), BF16)),
        name="to_bf16_tiles",
    )(w)


def _in_proj_kernel(x_ref, g_ref, w_ref, o_ref, h_ref, *, tm):
    @pl.when(pl.program_id(1) == 0)
    def _():
        _rms_to(h_ref, x_ref, g_ref, tm)

    o_ref[...] = _dot(h_ref[...], w_ref[...])


IN_PROJ_TN = 512


def _in_proj(x, g, w, l):
    m = x.shape[0]
    tm = min(m, 1024)
    tn = IN_PROJ_TN
    blocks = (_nbytes((tm, D_MODEL), F32) + _nbytes((D_MODEL, tn), BF16) + _nbytes((tm, tn), F32))
    return pl.pallas_call(
        functools.partial(_in_proj_kernel, tm=tm),
        out_shape=jax.ShapeDtypeStruct((m, IN_W), F32),
        grid=(m // tm, IN_W // tn),
        in_specs=[
            pl.BlockSpec((tm, D_MODEL), lambda i, j: (i, 0)),
            _gain_spec(l, 2),
            pl.BlockSpec((None, None, D_MODEL, tn), lambda i, j: (l, j, 0, 0)),
        ],
        out_specs=pl.BlockSpec((tm, tn), lambda i, j: (i, j)),
        scratch_shapes=[pltpu.VMEM((tm, D_MODEL), BF16)],
        compiler_params=_params(("parallel", "arbitrary"), blocks, _nbytes((tm, D_MODEL), BF16)),
        name="in_proj",
    )(x, g, w)


def _out_proj_kernel(ro_ref, so_ref, x_ref, wa_ref, wb_ref, gc_ref, wq_ref, x1_ref, q_ref, h_ref, *, tm):
    y = _dot(ro_ref[...].astype(BF16), wa_ref[...]) + _dot(so_ref[...].astype(BF16), wb_ref[...])
    x1_ref[...] = x_ref[...] + y
    _rms_to(h_ref, x1_ref, gc_ref, tm)
    q_ref[...] = _dot(h_ref[...], wq_ref[...])


def _out_proj(ro, so, x, w_out, g_cross, wq, l):
    m = x.shape[0]
    tm = min(m, 512)
    blocks = (_nbytes((tm, RET_W), ro.dtype) + _nbytes((tm, SWA_W), so.dtype)
              + 2 * _nbytes((tm, D_MODEL), F32) + _nbytes((D_MODEL, D_MODEL), BF16)
              + _nbytes((D_MODEL, MEM_W), BF16) + _nbytes((tm, MEM_W), F32))
    return pl.pallas_call(
        functools.partial(_out_proj_kernel, tm=tm),
        out_shape=(jax.ShapeDtypeStruct((m, D_MODEL), F32), jax.ShapeDtypeStruct((m, MEM_W), F32)),
        grid=(m // tm,),
        in_specs=[
            pl.BlockSpec((tm, RET_W), lambda i: (i, 0)),
            pl.BlockSpec((tm, SWA_W), lambda i: (i, 0)),
            pl.BlockSpec((tm, D_MODEL), lambda i: (i, 0)),
            pl.BlockSpec((None, RET_W, D_MODEL), lambda i: (l, 0, 0)),
            pl.BlockSpec((None, SWA_W, D_MODEL), lambda i: (l, 1, 0)),
            _gain_spec(l, 1),
            pl.BlockSpec((None, D_MODEL, MEM_W), lambda i: (l, 0, 0)),
        ],
        out_specs=(pl.BlockSpec((tm, D_MODEL), lambda i: (i, 0)),
                   pl.BlockSpec((tm, MEM_W), lambda i: (i, 0))),
        scratch_shapes=[pltpu.VMEM((tm, D_MODEL), BF16)],
        compiler_params=_params(("parallel",), blocks, _nbytes((tm, D_MODEL), BF16),
                                _nbytes((tm, D_MODEL), F32)),
        name="out_proj",
    )(ro, so, x, w_out, w_out, g_cross, wq)


def _ffn_kernel(x_ref, g_ref, wg_ref, wu_ref, wd_ref, gf_ref, o_ref, h_ref, *, tm, nf, final_norm):
    j = pl.program_id(1)

    @pl.when(j == 0)
    def _():
        _rms_to(h_ref, x_ref, g_ref, tm)
        o_ref[...] = x_ref[...]

    h = h_ref[...]
    a = _dot(h, wg_ref[...])
    u = _dot(h, wu_ref[...])
    act = (a * _sigmoid(a) * u).astype(BF16)
    half = D_MODEL // 2
    for c in range(2):
        cols = slice(c * half, (c + 1) * half)
        o_ref[:, cols] += _dot(act, wd_ref[:, cols])

    if final_norm:
        @pl.when(j == nf - 1)
        def _():
            _rms_to(o_ref, o_ref, gf_ref, tm)


FFN_TF = 256


def _ffn(x, g, wg, wu, wd, g_final, l):
    m = x.shape[0]
    tm = min(m, 1024)
    tf = FFN_TF
    nf = D_FF // tf
    blocks = (2 * _nbytes((tm, D_MODEL), F32) + 3 * _nbytes((D_MODEL, tf), BF16))
    scratch = _nbytes((tm, D_MODEL), BF16)
    return pl.pallas_call(
        functools.partial(_ffn_kernel, tm=tm, nf=nf, final_norm=(l == DEPTH - 1)),
        out_shape=jax.ShapeDtypeStruct((m, D_MODEL), F32),
        grid=(m // tm, nf),
        in_specs=[
            pl.BlockSpec((tm, D_MODEL), lambda i, j: (i, 0)),
            _gain_spec(l, 2),
            pl.BlockSpec((None, None, D_MODEL, tf), lambda i, j: (l, j, 0, 0)),
            pl.BlockSpec((None, None, D_MODEL, tf), lambda i, j: (l, j, 0, 0)),
            pl.BlockSpec((None, tf, D_MODEL), lambda i, j: (l, j, 0)),
            pl.BlockSpec((1, D_MODEL), lambda i, j: (0, 0)),
        ],
        out_specs=pl.BlockSpec((tm, D_MODEL), lambda i, j: (i, 0)),
        scratch_shapes=[pltpu.VMEM((tm, D_MODEL), BF16)],
        compiler_params=_params(("parallel", "arbitrary"), blocks, scratch,
                                3 * _nbytes((tm, tf), F32) + _nbytes((tm, D_MODEL // 2), F32)),
        name="ffn",
    )(x, g, wg, wu, wd, g_final.reshape(1, D_MODEL))


def _matmul_residual_kernel(a_ref, w_ref, x_ref, o_ref):
    o_ref[...] = x_ref[...] + _dot(a_ref[...].astype(BF16), w_ref[...])


def _matmul_residual(a, w, x, l):
    m, k = a.shape
    n = w.shape[2]
    blocks = _nbytes((m, k), a.dtype) + _nbytes((k, n), BF16) + 2 * _nbytes((m, n), F32)
    return pl.pallas_call(
        _matmul_residual_kernel,
        out_shape=jax.ShapeDtypeStruct((m, n), F32),
        grid=(1,),
        in_specs=[pl.BlockSpec((m, k), lambda i: (0, 0)),
                  pl.BlockSpec((None, k, n), lambda i: (l, 0, 0)),
                  pl.BlockSpec((m, n), lambda i: (0, 0))],
        out_specs=pl.BlockSpec((m, n), lambda i: (0, 0)),
        compiler_params=_params(("arbitrary",), blocks),
        name="matmul_residual",
    )(a, w, x)


def _rope_tables(positions):
    half = HEAD_DIM // 2
    inv = ROPE_BASE ** (-np.arange(half, dtype=np.float64) / half)
    ang = np.asarray(positions, np.float64)[:, None] * inv[None, :]
    cos, sin = np.cos(ang), np.sin(ang)
    c = np.concatenate([cos, cos], axis=-1).astype(np.float32)
    s = np.concatenate([-sin, sin], axis=-1).astype(np.float32)
    return c, s


def _decay_tables():
    c = RET_CHUNK
    lg = np.log1p(-np.exp2(-5.0 - np.arange(RET_HEADS, dtype=np.float64)))
    idx = np.arange(c, dtype=np.float64)
    diff = idx[:, None] - idx[None, :]
    dmask = np.where(diff[None] >= 0, np.exp(lg[:, None, None] * np.maximum(diff, 0.0)[None]), 0.0)
    xi = np.exp(lg[:, None] * (idx + 1.0)[None])
    zeta = np.exp(lg[:, None] * (c - 1.0 - idx)[None])
    bcast = lambda t: np.broadcast_to(t[:, :, None], (RET_HEADS, c, LANES)).astype(np.float32)
    return dmask.astype(np.float32), bcast(xi), bcast(zeta)


def _rotary(x, c, s):
    return x * c + pltpu.roll(x, HEAD_DIM // 2, 1) * s


def _ret_prompt_kernel(q_ref, k_ref, v_ref, g_ref, c_ref, s_ref, dm_ref, xi_ref, zt_ref, gn_ref,
                       o_ref, st_ref, *, nchunk):
    h = pl.program_id(1)

    @pl.when(pl.program_id(0) == 0)
    def _():
        st_ref[h] = jnp.zeros((HEAD_DIM, HEAD_DIM), F32)

    c_tab = c_ref[...]
    s_tab = s_ref[...]
    qr = _rotary(q_ref[...], c_tab, s_tab)
    kr = _rotary(k_ref[...], c_tab, s_tab) * ATT_SCALE
    dmask = dm_ref[h]
    xi = xi_ref[h]
    zeta = zt_ref[h]
    g_chunk = xi[RET_CHUNK - 1:RET_CHUNK, :]
    gn = gn_ref[...]
    state = st_ref[h]
    for c in range(nchunk):
        rows = slice(c * RET_CHUNK, (c + 1) * RET_CHUNK)
        qc = qr[rows]
        kc = kr[rows]
        vc = v_ref[rows, :].astype(BF16)
        s = _dot_nt(qc.astype(BF16), kc.astype(BF16)) * dmask
        inner = _dot(s.astype(BF16), vc)
        cross = _dot((qc * xi).astype(BF16), state.astype(BF16))
        kv = _dot((kc * zeta).T.astype(BF16), vc)
        state = g_chunk * state + kv
        out = inner + cross
        out = out * lax.rsqrt(jnp.mean(out * out, axis=-1, keepdims=True) + EPS) * gn
        gate = g_ref[rows, :]
        o_ref[rows, :] = (gate * _sigmoid(gate) * out).astype(o_ref.dtype)
    st_ref[h] = state


def _ret_prompt(proj, c_tab, s_tab, dmask, xi, zeta, ret_gn, l):
    m = proj.shape[0]
    r = 1024
    nchunk = r // RET_CHUNK
    col = lambda off: pl.BlockSpec((r, HEAD_DIM), lambda i, h: (i, off + h))
    head_tab = pl.BlockSpec((RET_HEADS, RET_CHUNK, LANES), lambda i, h: (0, 0, 0))
    blocks = (6 * _nbytes((r, HEAD_DIM), F32) + 4 * _nbytes((RET_HEADS, RET_CHUNK, LANES), F32)
              + _nbytes((r, HEAD_DIM), BF16))
    return pl.pallas_call(
        functools.partial(_ret_prompt_kernel, nchunk=nchunk),
        out_shape=(jax.ShapeDtypeStruct((m, RET_W), BF16),
                   jax.ShapeDtypeStruct((RET_HEADS, HEAD_DIM, HEAD_DIM), F32)),
        grid=(m // r, RET_HEADS),
        in_specs=[col(COL_RQ), col(COL_RK), col(COL_RV), col(COL_RG),
                  pl.BlockSpec((r, HEAD_DIM), lambda i, h: (i, 0)),
                  pl.BlockSpec((r, HEAD_DIM), lambda i, h: (i, 0)),
                  head_tab, head_tab, head_tab,
                  pl.BlockSpec((None, 1, HEAD_DIM), lambda i, h: (l, 0, h))],
        out_specs=(pl.BlockSpec((r, HEAD_DIM), lambda i, h: (i, h)),
                   pl.BlockSpec((RET_HEADS, HEAD_DIM, HEAD_DIM), lambda i, h: (0, 0, 0))),
        compiler_params=_params(("arbitrary", "arbitrary"), blocks, 0, 4 * _nbytes((r, HEAD_DIM), F32)),
        name="ret_prompt",
    )(proj, proj, proj, proj, c_tab, s_tab, dmask, xi, zeta, ret_gn)


def _swa_prompt_kernel(sink_ref, q_ref, k_ref, v_ref, kp_ref, vp_ref, o_ref, *, nblk, l):
    kvh = pl.program_id(0)
    i = pl.program_id(1)
    kfull = jnp.concatenate([kp_ref[...], k_ref[...]], axis=0).astype(BF16)
    vfull = jnp.concatenate([vp_ref[...], v_ref[...]], axis=0).astype(BF16)
    row = lax.broadcasted_iota(jnp.int32, (WINDOW, 2 * WINDOW), 0)
    col = lax.broadcasted_iota(jnp.int32, (WINDOW, 2 * WINDOW), 1)
    band = (col >= row) & (col <= row + WINDOW)
    for j in range(nblk):
        rows = slice(j * WINDOW, (j + 1) * WINDOW)
        kk = kfull[j * WINDOW:(j + 2) * WINDOW]
        vv = vfull[j * WINDOW:(j + 2) * WINDOW]
        if j == 0:
            mask = band & (col >= jnp.where(i > 0, 0, WINDOW))
        else:
            mask = band
        qj = q_ref[rows, :].astype(BF16)
        qs = jnp.concatenate([qj[:, h * HEAD_DIM:(h + 1) * HEAD_DIM] for h in range(SWA_GROUP)], axis=0)
        s = _dot_nt(qs, kk) * ATT_SCALE
        ps = []
        for h in range(SWA_GROUP):
            sh = jnp.where(mask, s[h * WINDOW:(h + 1) * WINDOW], -jnp.inf)
            sink = sink_ref[l, kvh * SWA_GROUP + h]
            mx = jnp.maximum(jnp.max(sh, axis=-1, keepdims=True), sink)
            e = jnp.exp(sh - mx)
            den = jnp.sum(e, axis=-1, keepdims=True) + jnp.exp(sink - mx)
            ps.append((e / den).astype(BF16))
        o = _dot(jnp.concatenate(ps, axis=0), vv)
        for h in range(SWA_GROUP):
            o_ref[rows, h * HEAD_DIM:(h + 1) * HEAD_DIM] = o[h * WINDOW:(h + 1) * WINDOW].astype(o_ref.dtype)


def _swa_prompt(proj, sinks, l):
    m = proj.shape[0]
    r = 512
    nblk = r // WINDOW
    gw = SWA_GROUP * HEAD_DIM
    prev = lambda off: pl.BlockSpec(
        (WINDOW, HEAD_DIM), lambda kv, i: (jnp.maximum(i * nblk - 1, 0), off + kv))
    blocks = (_nbytes((r, gw), F32) + 2 * _nbytes((r, HEAD_DIM), F32) + 2 * _nbytes((WINDOW, HEAD_DIM), F32)
              + _nbytes((r, gw), BF16))
    return pl.pallas_call(
        functools.partial(_swa_prompt_kernel, nblk=nblk, l=l),
        out_shape=jax.ShapeDtypeStruct((m, SWA_W), BF16),
        grid=(SWA_KV_HEADS, m // r),
        in_specs=[pl.BlockSpec(memory_space=pltpu.SMEM),
                  pl.BlockSpec((r, gw), lambda kv, i: (i, COL_SQ // SWA_GROUP + kv)),
                  pl.BlockSpec((r, HEAD_DIM), lambda kv, i: (i, COL_SK + kv)),
                  pl.BlockSpec((r, HEAD_DIM), lambda kv, i: (i, COL_SV + kv)),
                  prev(COL_SK), prev(COL_SV)],
        out_specs=pl.BlockSpec((r, gw), lambda kv, i: (i, kv)),
        compiler_params=_params(("parallel", "parallel"), blocks, 0, 8 * _nbytes((gw, 2 * WINDOW), F32)),
        name="swa_prompt",
    )(sinks, proj, proj, proj, proj, proj)


def _mem_kv_kernel(m_ref, g_ref, wk_ref, wv_ref, k_ref, v_ref):
    h = _rms_rows(m_ref[...], g_ref[...]).astype(BF16)
    k_ref[...] = _dot(h, wk_ref[...])
    v_ref[...] = _dot(h, wv_ref[...])


def _mem_kv(mem, g, wk, wv, l):
    full = lambda shape: pl.BlockSpec(shape, lambda i: (0,) * len(shape))
    layer = lambda shape: pl.BlockSpec((None,) + shape, lambda i: (l,) + (0,) * len(shape))
    blocks = (_nbytes((MEM_LEN, D_MODEL), F32) + 2 * _nbytes((D_MODEL, MEM_W), BF16)
              + 2 * _nbytes((MEM_LEN, MEM_W), F32))
    return pl.pallas_call(
        _mem_kv_kernel,
        out_shape=(jax.ShapeDtypeStruct((MEM_LEN, MEM_W), F32),) * 2,
        grid=(1,),
        in_specs=[full((MEM_LEN, D_MODEL)), _gain_spec(l, 1), layer((D_MODEL, MEM_W)), layer((D_MODEL, MEM_W))],
        out_specs=(full((MEM_LEN, MEM_W)),) * 2,
        compiler_params=_params(("arbitrary",), blocks, 0, _nbytes((MEM_LEN, D_MODEL), F32)),
        name="mem_kv",
    )(mem, g, wk, wv)


def _cross_prompt_kernel(q_ref, mk_ref, mv_ref, wo_ref, x_ref, o_ref):
    q = q_ref[...].astype(BF16)
    mk = mk_ref[...].astype(BF16)
    mv = mv_ref[...].astype(BF16)
    outs = []
    for h in range(MEM_HEADS):
        cols = slice(h * HEAD_DIM, (h + 1) * HEAD_DIM)
        s = _dot_nt(q[:, cols], mk[:, cols]) * ATT_SCALE
        e = jnp.exp(s - jnp.max(s, axis=-1, keepdims=True))
        p = e / jnp.sum(e, axis=-1, keepdims=True)
        outs.append(_dot(p.astype(BF16), mv[:, cols]).astype(BF16))
    o_ref[...] = x_ref[...] + _dot(jnp.concatenate(outs, axis=1), wo_ref[...])


def _cross_prompt(q, mk, mv, wo, x, l):
    m = x.shape[0]
    tm = 512
    blocks = (_nbytes((tm, MEM_W), F32) + 2 * _nbytes((MEM_LEN, MEM_W), F32) + _nbytes((MEM_W, D_MODEL), BF16)
              + 2 * _nbytes((tm, D_MODEL), F32))
    return pl.pallas_call(
        _cross_prompt_kernel,
        out_shape=jax.ShapeDtypeStruct((m, D_MODEL), F32),
        grid=(m // tm,),
        in_specs=[pl.BlockSpec((tm, MEM_W), lambda i: (i, 0)),
                  pl.BlockSpec((MEM_LEN, MEM_W), lambda i: (0, 0)),
                  pl.BlockSpec((MEM_LEN, MEM_W), lambda i: (0, 0)),
                  pl.BlockSpec((None, MEM_W, D_MODEL), lambda i: (l, 0, 0)),
                  pl.BlockSpec((tm, D_MODEL), lambda i: (i, 0))],
        out_specs=pl.BlockSpec((tm, D_MODEL), lambda i: (i, 0)),
        compiler_params=_params(("parallel",), blocks, 0, 2 * _nbytes((tm, D_MODEL), F32)),
        name="cross_prompt",
    )(q, mk, mv, wo, x)


def _chain(prev):
    if prev is None:
        return (), ()
    return (prev,), (pl.BlockSpec(memory_space=pl.ANY),)


def _ret_sample_kernel(gam_ref, q_ref, k_ref, v_ref, g_ref, c_ref, s_ref, gn_ref, st_ref, *rest, bt, nt):
    o_ref, nst_ref, qt_ref, kt_ref = rest[-4:]
    h = pl.program_id(0)
    t = pl.program_id(1)

    @pl.when(t == 0)
    def _():
        c_tab = c_ref[...]
        s_tab = s_ref[...]
        qt = _rotary(q_ref[...], c_tab, s_tab).T
        kt = (_rotary(k_ref[...], c_tab, s_tab) * ATT_SCALE).T
        for tt in range(nt):
            shift = (LANES - tt * bt) % LANES
            qt_ref[tt] = pltpu.roll(qt, shift, 1) if shift else qt
            kt_ref[tt] = pltpu.roll(kt, shift, 1) if shift else kt

    gamma = gam_ref[h]
    qt = qt_ref[t]
    kt = kt_ref[t]
    outs = []
    for ib in range(bt):
        new = gamma * st_ref[ib] + kt[:, ib:ib + 1] * v_ref[ib:ib + 1, :]
        nst_ref[ib] = new
        outs.append(jnp.sum(qt[:, ib:ib + 1] * new, axis=0, keepdims=True))
    out = jnp.concatenate(outs, axis=0)
    out = out * lax.rsqrt(jnp.mean(out * out, axis=-1, keepdims=True) + EPS) * gn_ref[...]
    gate = g_ref[...]
    o_ref[...] = gate * _sigmoid(gate) * out


def _ret_sample(proj, state, prev, gammas, c_tab, s_tab, ret_gn, l):
    b = proj.shape[0]
    bt = 32
    nt = b // bt
    col = lambda off: pl.BlockSpec((b, HEAD_DIM), lambda h, t: (0, off + h))
    tile = lambda off: pl.BlockSpec((bt, HEAD_DIM), lambda h, t: (t, off + h))
    row = pl.BlockSpec((1, HEAD_DIM), lambda h, t: (0, 0))
    st_spec = pl.BlockSpec((None, bt, None, HEAD_DIM, HEAD_DIM), lambda h, t: (l, t, h, 0, 0))
    extra, extra_specs = _chain(prev)
    n_in = 9 + len(extra)
    blocks = (2 * _nbytes((b, HEAD_DIM), F32) + 3 * _nbytes((bt, HEAD_DIM), F32)
              + 2 * _nbytes((bt, HEAD_DIM, HEAD_DIM), F32))
    scratch = 2 * _nbytes((nt, HEAD_DIM, LANES), F32)
    return pl.pallas_call(
        functools.partial(_ret_sample_kernel, bt=bt, nt=nt),
        out_shape=(jax.ShapeDtypeStruct((b, RET_W), F32), jax.ShapeDtypeStruct(state.shape, F32)),
        grid=(RET_HEADS, nt),
        in_specs=[pl.BlockSpec(memory_space=pltpu.SMEM),
                  col(COL_RQ), col(COL_RK), tile(COL_RV), tile(COL_RG), row, row,
                  pl.BlockSpec((None, 1, HEAD_DIM), lambda h, t: (l, 0, h)),
                  st_spec, *extra_specs],
        out_specs=(tile(0), st_spec),
        scratch_shapes=[pltpu.VMEM((nt, HEAD_DIM, LANES), F32), pltpu.VMEM((nt, HEAD_DIM, LANES), F32)],
        input_output_aliases=({n_in - 1: 1} if extra else {}),
        compiler_params=_params(("parallel", "arbitrary"), blocks, scratch, 4 * _nbytes((bt, HEAD_DIM, HEAD_DIM), F32)),
        name="ret_sample",
    )(gammas, proj, proj, proj, proj, c_tab, s_tab, ret_gn, state, *extra)


def _swa_sample_kernel(sink_ref, q_ref, kn_ref, vn_ref, ck_ref, cv_ref, *rest, bt):
    o_ref, nk_ref, nv_ref = rest[-3:]
    rows = SWA_KV_HEADS * WINDOW
    sink = sink_ref[...]
    hrow = lax.broadcasted_iota(jnp.int32, (SWA_HEADS, HEAD_DIM), 0)
    first_group = hrow < SWA_GROUP
    srow = lax.broadcasted_iota(jnp.int32, (SWA_HEADS, rows), 0)
    scol = lax.broadcasted_iota(jnp.int32, (SWA_HEADS, rows), 1)
    own = (scol & (SWA_KV_HEADS - 1)) == (srow // SWA_GROUP)

    def body(ib, carry):
        q = q_ref[ib]
        kn = kn_ref[ib]
        vn = vn_ref[ib]
        k_new = jnp.where(first_group, kn[0:1, :], kn[1:2, :])
        v_new = jnp.where(first_group, vn[0:1, :], vn[1:2, :])
        s_new = jnp.sum(q * k_new, axis=-1, keepdims=True) * ATT_SCALE
        ck = ck_ref[ib]
        cv = cv_ref[ib]
        s = jnp.where(own, _dot_nt(q.astype(BF16), ck.astype(BF16)) * ATT_SCALE, -jnp.inf)
        mx = jnp.maximum(jnp.maximum(jnp.max(s, axis=-1, keepdims=True), s_new), sink)
        e = jnp.exp(s - mx)
        e_new = jnp.exp(s_new - mx)
        den = jnp.sum(e, axis=-1, keepdims=True) + e_new + jnp.exp(sink - mx)
        o_ref[ib] = _dot((e / den).astype(BF16), cv.astype(BF16)) + (e_new / den) * v_new
        nk_ref[ib, 0:rows - SWA_KV_HEADS, :] = ck[SWA_KV_HEADS:rows, :]
        nv_ref[ib, 0:rows - SWA_KV_HEADS, :] = cv[SWA_KV_HEADS:rows, :]
        nk_ref[ib, rows - SWA_KV_HEADS:rows, :] = kn
        nv_ref[ib, rows - SWA_KV_HEADS:rows, :] = vn
        return carry

    lax.fori_loop(0, bt, body, 0, unroll=4)


def _swa_sample(q, k_new, v_new, cache_k, cache_v, prev_k, prev_v, sinks, l):
    b = q.shape[0]
    bt = 16
    rows = SWA_KV_HEADS * WINDOW
    blocks = (2 * _nbytes((bt, SWA_HEADS, HEAD_DIM), F32) + 2 * _nbytes((bt, 8, HEAD_DIM), F32)
              + 4 * _nbytes((bt, rows, HEAD_DIM), F32))
    cache_spec = pl.BlockSpec((None, bt, rows, HEAD_DIM), lambda t: (l, t, 0, 0))
    new_spec = pl.BlockSpec((bt, SWA_KV_HEADS, HEAD_DIM), lambda t: (t, 0, 0))
    head_spec = pl.BlockSpec((bt, SWA_HEADS, HEAD_DIM), lambda t: (t, 0, 0))
    extra_k, spec_k = _chain(prev_k)
    extra_v, spec_v = _chain(prev_v)
    aliases = {6: 1, 7: 2} if extra_k else {}
    return pl.pallas_call(
        functools.partial(_swa_sample_kernel, bt=bt),
        out_shape=(jax.ShapeDtypeStruct((b, SWA_HEADS, HEAD_DIM), F32),
                   jax.ShapeDtypeStruct(cache_k.shape, F32), jax.ShapeDtypeStruct(cache_v.shape, F32)),
        grid=(b // bt,),
        in_specs=[pl.BlockSpec((None, SWA_HEADS, 1), lambda t: (l, 0, 0)),
                  head_spec, new_spec, new_spec, cache_spec, cache_spec, *spec_k, *spec_v],
        out_specs=(head_spec, cache_spec, cache_spec),
        input_output_aliases=aliases,
        compiler_params=_params(("parallel",), blocks),
        name="swa_sample",
    )(sinks.reshape(DEPTH, SWA_HEADS, 1), q, k_new, v_new, cache_k, cache_v, *extra_k, *extra_v)


def _cross_sample_kernel(q_ref, mk_ref, mv_ref, o_ref, *, bt):
    rows = MEM_HEADS * MEM_LEN
    srow = lax.broadcasted_iota(jnp.int32, (MEM_HEADS, rows), 0)
    scol = lax.broadcasted_iota(jnp.int32, (MEM_HEADS, rows), 1)
    own = (scol & (MEM_HEADS - 1)) == srow

    def body(ib, carry):
        qb = q_ref[ib].astype(BF16)
        s = jnp.where(own, _dot_nt(qb, mk_ref[ib].astype(BF16)) * ATT_SCALE, -jnp.inf)
        e = jnp.exp(s - jnp.max(s, axis=-1, keepdims=True))
        p = (e / jnp.sum(e, axis=-1, keepdims=True)).astype(BF16)
        o_ref[ib] = _dot(p, mv_ref[ib].astype(BF16))
        return carry

    lax.fori_loop(0, bt, body, 0, unroll=4)


def _cross_sample(q, cache_k, cache_v, l):
    b = q.shape[0]
    bt = 8
    rows = MEM_HEADS * MEM_LEN
    blocks = 2 * _nbytes((bt, 8, HEAD_DIM), F32) + 2 * _nbytes((bt, rows, HEAD_DIM), F32)
    cache_spec = pl.BlockSpec((None, bt, rows, HEAD_DIM), lambda t: (l, t, 0, 0))
    head_spec = pl.BlockSpec((bt, MEM_HEADS, HEAD_DIM), lambda t: (t, 0, 0))
    return pl.pallas_call(
        functools.partial(_cross_sample_kernel, bt=bt),
        out_shape=jax.ShapeDtypeStruct((b, MEM_HEADS, HEAD_DIM), F32),
        grid=(b // bt,),
        in_specs=[head_spec, cache_spec, cache_spec],
        out_specs=head_spec,
        compiler_params=_params(("parallel",), blocks),
        name="cross_sample",
    )(q, cache_k, cache_v)


def kernel(x_prompt, x_sample, mem_prompt, state_ret, cache_win_k, cache_win_v, cache_mem_k, cache_mem_v,
           g_mix, w_in, ret_gn, sinks, w_out, g_cross, g_mem, wq_c, wk_c, wv_c, wo_c,
           g_ffn, w_gate, w_up, w_down, g_final):
    seq = x_prompt.shape[1]
    nb = x_sample.shape[0]
    w_in_b, w_out_b = _to_bf16_tiles(w_in, IN_PROJ_TN), _to_bf16(w_out)
    wq_b, wk_b, wv_b, wo_b = (_to_bf16(w) for w in (wq_c, wk_c, wv_c, wo_c))
    wg_b, wu_b, wd_b = _to_bf16_tiles(w_gate, FFN_TF), _to_bf16_tiles(w_up, FFN_TF), _to_bf16(w_down)
    gain = lambda g: g.reshape(DEPTH, 1, g.shape[-1])
    g_mix, g_cross, g_mem, g_ffn, ret_gn = (gain(g) for g in (g_mix, g_cross, g_mem, g_ffn, ret_gn))

    cp, sp = (jnp.asarray(t) for t in _rope_tables(np.arange(seq)))
    cs, ss = (jnp.asarray(t) for t in _rope_tables(np.array([PAST_LEN])))
    dmask, xi, zeta = (jnp.asarray(t) for t in _decay_tables())
    gammas = jnp.asarray((1.0 - np.exp2(-5.0 - np.arange(RET_HEADS))).astype(np.float32))

    xp = x_prompt[0]
    xs = x_sample[:, 0]
    mem = mem_prompt[0]
    kvw = SWA_KV_HEADS * HEAD_DIM
    win_k = cache_win_k.reshape(DEPTH, nb, WINDOW * SWA_KV_HEADS, HEAD_DIM)
    win_v = cache_win_v.reshape(DEPTH, nb, WINDOW * SWA_KV_HEADS, HEAD_DIM)
    mem_k = cache_mem_k.reshape(DEPTH, nb, MEM_LEN * MEM_HEADS, HEAD_DIM)
    mem_v = cache_mem_v.reshape(DEPTH, nb, MEM_LEN * MEM_HEADS, HEAD_DIM)

    ret_p, wink_p, winv_p, memk_p, memv_p = [], [], [], [], []
    ret_s = wink_s = winv_s = None
    for l in range(DEPTH):
        proj = _in_proj(xp, g_mix, w_in_b, l)
        ro, r_last = _ret_prompt(proj, cp, sp, dmask, xi, zeta, ret_gn, l)
        so = _swa_prompt(proj, sinks, l)
        x1, qc = _out_proj(ro, so, xp, w_out_b, g_cross, wq_b, l)
        mk, mv = _mem_kv(mem, g_mem, wk_b, wv_b, l)
        x2 = _cross_prompt(qc, mk, mv, wo_b, x1, l)
        xp = _ffn(x2, g_ffn, wg_b, wu_b, wd_b, g_final, l)
        ret_p.append(r_last[None])
        tail = proj[seq - WINDOW:]
        wink_p.append(tail[:, COL_SK * LANES:COL_SK * LANES + kvw].reshape(1, WINDOW, SWA_KV_HEADS, HEAD_DIM))
        winv_p.append(tail[:, COL_SV * LANES:COL_SV * LANES + kvw].reshape(1, WINDOW, SWA_KV_HEADS, HEAD_DIM))
        memk_p.append(mk.reshape(1, MEM_LEN, MEM_HEADS, HEAD_DIM))
        memv_p.append(mv.reshape(1, MEM_LEN, MEM_HEADS, HEAD_DIM))

        proj_s = _in_proj(xs, g_mix, w_in_b, l)
        ro_s, ret_s = _ret_sample(proj_s, state_ret, ret_s, gammas, cs, ss, ret_gn, l)
        sq = proj_s[:, COL_SQ * LANES:COL_SQ * LANES + SWA_W].reshape(nb, SWA_HEADS, HEAD_DIM)
        sk = proj_s[:, COL_SK * LANES:COL_SK * LANES + kvw].reshape(nb, SWA_KV_HEADS, HEAD_DIM)
        sv = proj_s[:, COL_SV * LANES:COL_SV * LANES + kvw].reshape(nb, SWA_KV_HEADS, HEAD_DIM)
        so_s, wink_s, winv_s = _swa_sample(sq, sk, sv, win_k, win_v, wink_s, winv_s, sinks, l)
        x1_s, qc_s = _out_proj(ro_s, so_s.reshape(nb, SWA_W), xs, w_out_b, g_cross, wq_b, l)
        oc_s = _cross_sample(qc_s.reshape(nb, MEM_HEADS, HEAD_DIM), mem_k, mem_v, l)
        x2_s = _matmul_residual(oc_s.reshape(nb, MEM_W), wo_b, x1_s, l)
        xs = _ffn(x2_s, g_ffn, wg_b, wu_b, wd_b, g_final, l)

    cache_shape = (DEPTH, nb, WINDOW, SWA_KV_HEADS, HEAD_DIM)
    return (xp[None], xs[:, None],
            jnp.stack(ret_p), jnp.stack(wink_p), jnp.stack(winv_p), jnp.stack(memk_p), jnp.stack(memv_p),
            ret_s, wink_s.reshape(cache_shape), winv_s.reshape(cache_shape))
```

```python
import functools

import numpy as np
import jax
import jax.numpy as jnp
from jax import lax
from jax.experimental import pallas as pl
from jax.experimental.pallas import tpu as pltpu

D_MODEL = 2048
DEPTH = 2
PAST_LEN = 8192
HEAD_DIM = 128
RET_HEADS = 8
RET_W = RET_HEADS * HEAD_DIM
RET_CHUNK = 128
SWA_HEADS = 8
SWA_KV_HEADS = 2
SWA_GROUP = SWA_HEADS // SWA_KV_HEADS
SWA_W = SWA_HEADS * HEAD_DIM
WINDOW = 128
MEM_LEN = 256
MEM_HEADS = 4
MEM_W = MEM_HEADS * HEAD_DIM
IN_W = 4 * RET_W + SWA_W + 2 * SWA_KV_HEADS * HEAD_DIM
D_FF = 5632
ROPE_BASE = 10000.0
EPS = 1e-6
ATT_SCALE = HEAD_DIM ** -0.5

COL_RQ, COL_RK, COL_RV, COL_RG = 0, 8, 16, 24
COL_SQ, COL_SK, COL_SV = 32, 40, 42

V7X_VMEM_BYTES = 64 * 1024 * 1024
V7X_VMEM_BUDGET = V7X_VMEM_BYTES - 8 * 1024 * 1024
LANES = 128
CAST_BLOCK_BYTES = 4 * 1024 * 1024

BF16 = jnp.bfloat16
F32 = jnp.float32


def _vmem_limit(block_bytes, scratch_bytes=0, temp_bytes=0):
    need = 2 * block_bytes + scratch_bytes + temp_bytes + 4 * 1024 * 1024
    return int(min(max(need, 16 * 1024 * 1024), V7X_VMEM_BUDGET))


def _nbytes(shape, dtype):
    return int(np.prod(shape)) * jnp.dtype(dtype).itemsize


def _params(semantics, block_bytes, scratch_bytes=0, temp_bytes=0):
    return pltpu.CompilerParams(
        dimension_semantics=semantics,
        vmem_limit_bytes=_vmem_limit(block_bytes, scratch_bytes, temp_bytes))


def _rms_rows(x, g):
    ms = jnp.mean(x * x, axis=-1, keepdims=True)
    return x * lax.rsqrt(ms + EPS) * g


def _rms_to(dst_ref, src_ref, g_ref, rows):
    g = g_ref[...]
    chunk = min(rows, 128)

    def body(c, carry):
        r = pl.multiple_of(c * chunk, chunk)
        x = src_ref[pl.ds(r, chunk), :]
        dst_ref[pl.ds(r, chunk), :] = _rms_rows(x, g).astype(dst_ref.dtype)
        return carry

    lax.fori_loop(0, rows // chunk, body, 0)


def _sigmoid(x):
    return 1.0 / (1.0 + jnp.exp(-x))


def _dot(a, b):
    return jnp.dot(a, b, preferred_element_type=F32)


def _dot_nt(a, b):
    return lax.dot_general(a, b, (((1,), (1,)), ((), ())), preferred_element_type=F32)


def _gain_spec(l, ngrid):
    zeros = (0,) * 2
    return pl.BlockSpec((None, 1, D_MODEL), lambda *_: (l,) + zeros)


def _cast_kernel(x_ref, o_ref):
    o_ref[...] = x_ref[...].astype(o_ref.dtype)


def _to_bf16(w):
    depth, r, c = w.shape
    tr = r
    while _nbytes((tr, c), F32) > CAST_BLOCK_BYTES and tr % 32 == 0:
        tr //= 2
    spec = pl.BlockSpec((None, tr, c), lambda d, i: (d, i, 0))
    return pl.pallas_call(
        _cast_kernel,
        out_shape=jax.ShapeDtypeStruct(w.shape, BF16),
        grid=(depth, r // tr),
        in_specs=[spec],
        out_specs=spec,
        compiler_params=_params(("parallel", "parallel"), _nbytes((tr, c), F32) + _nbytes((tr, c), BF16)),
        name="to_bf16",
    )(w)


def _cast_tiles_kernel(x_ref, o_ref, *, nt, tn):
    for j in range(nt):
        o_ref[j] = x_ref[:, j * tn:(j + 1) * tn].astype(o_ref.dtype)


def _to_bf16_tiles(w, tn):
    depth, r, c = w.shape
    nt = c // tn
    tr = r
    while _nbytes((tr, c), F32) > CAST_BLOCK_BYTES and tr % 32 == 0:
        tr //= 2
    return pl.pallas_call(
        functools.partial(_cast_tiles_kernel, nt=nt, tn=tn),
        out_shape=jax.ShapeDtypeStruct((depth, nt, r, tn), BF16),
        grid=(depth, r // tr),
        in_specs=[pl.BlockSpec((None, tr, c), lambda d, i: (d, i, 0))],
        out_specs=pl.BlockSpec((None, nt, tr, tn), lambda d, i: (d, 0, i, 0)),
        compiler_params=_params(("parallel", "parallel"), _nbytes((tr, c), F32) + _nbytes((tr, c), BF16)),
        name="to_bf16_tiles",
    )(w)


def _in_proj_kernel(x_ref, g_ref, w_ref, o_ref, h_ref, *, tm):
    @pl.when(pl.program_id(1) == 0)
    def _():
        _rms_to(h_ref, x_ref, g_ref, tm)

    o_ref[...] = _dot(h_ref[...], w_ref[...])


IN_PROJ_TN = 512


def _in_proj(x, g, w, l):
    m = x.shape[0]
    tm = min(m, 1024)
    tn = IN_PROJ_TN
    blocks = (_nbytes((tm, D_MODEL), F32) + _nbytes((D_MODEL, tn), BF16) + _nbytes((tm, tn), F32))
    return pl.pallas_call(
        functools.partial(_in_proj_kernel, tm=tm),
        out_shape=jax.ShapeDtypeStruct((m, IN_W), F32),
        grid=(m // tm, IN_W // tn),
        in_specs=[
            pl.BlockSpec((tm, D_MODEL), lambda i, j: (i, 0)),
            _gain_spec(l, 2),
            pl.BlockSpec((None, None, D_MODEL, tn), lambda i, j: (l, j, 0, 0)),
        ],
        out_specs=pl.BlockSpec((tm, tn), lambda i, j: (i, j)),
        scratch_shapes=[pltpu.VMEM((tm, D_MODEL), BF16)],
        compiler_params=_params(("parallel", "arbitrary"), blocks, _nbytes((tm, D_MODEL), BF16)),
        name="in_proj",
    )(x, g, w)


def _out_proj_kernel(ro_ref, so_ref, x_ref, wa_ref, wb_ref, gc_ref, wq_ref, x1_ref, q_ref, h_ref, *, tm):
    y = _dot(ro_ref[...].astype(BF16), wa_ref[...]) + _dot(so_ref[...].astype(BF16), wb_ref[...])
    x1_ref[...] = x_ref[...] + y
    _rms_to(h_ref, x1_ref, gc_ref, tm)
    q_ref[...] = _dot(h_ref[...], wq_ref[...])


def _out_proj(ro, so, x, w_out, g_cross, wq, l):
    m = x.shape[0]
    tm = min(m, 512)
    blocks = (_nbytes((tm, RET_W), ro.dtype) + _nbytes((tm, SWA_W), so.dtype)
              + 2 * _nbytes((tm, D_MODEL), F32) + _nbytes((D_MODEL, D_MODEL), BF16)
              + _nbytes((D_MODEL, MEM_W), BF16) + _nbytes((tm, MEM_W), F32))
    return pl.pallas_call(
        functools.partial(_out_proj_kernel, tm=tm),
        out_shape=(jax.ShapeDtypeStruct((m, D_MODEL), F32), jax.ShapeDtypeStruct((m, MEM_W), F32)),
        grid=(m // tm,),
        in_specs=[
            pl.BlockSpec((tm, RET_W), lambda i: (i, 0)),
            pl.BlockSpec((tm, SWA_W), lambda i: (i, 0)),
            pl.BlockSpec((tm, D_MODEL), lambda i: (i, 0)),
            pl.BlockSpec((None, RET_W, D_MODEL), lambda i: (l, 0, 0)),
            pl.BlockSpec((None, SWA_W, D_MODEL), lambda i: (l, 1, 0)),
            _gain_spec(l, 1),
            pl.BlockSpec((None, D_MODEL, MEM_W), lambda i: (l, 0, 0)),
        ],
        out_specs=(pl.BlockSpec((tm, D_MODEL), lambda i: (i, 0)),
                   pl.BlockSpec((tm, MEM_W), lambda i: (i, 0))),
        scratch_shapes=[pltpu.VMEM((tm, D_MODEL), BF16)],
        compiler_params=_params(("parallel",), blocks, _nbytes((tm, D_MODEL), BF16),
                                _nbytes((tm, D_MODEL), F32)),
        name="out_proj",
    )(ro, so, x, w_out, w_out, g_cross, wq)


def _ffn_kernel(x_ref, g_ref, wg_ref, wu_ref, wd_ref, gf_ref, o_ref, h_ref, *, tm, nf, final_norm):
    j = pl.program_id(1)

    @pl.when(j == 0)
    def _():
        _rms_to(h_ref, x_ref, g_ref, tm)
        o_ref[...] = x_ref[...]

    h = h_ref[...]
    a = _dot(h, wg_ref[...].astype(BF16))
    u = _dot(h, wu_ref[...].astype(BF16))
    act = (a * _sigmoid(a) * u).astype(BF16)
    half = D_MODEL // 2
    for c in range(2):
        cols = slice(c * half, (c + 1) * half)
        o_ref[:, cols] += _dot(act, wd_ref[:, cols].astype(BF16))

    if final_norm:
        @pl.when(j == nf - 1)
        def _():
            _rms_to(o_ref, o_ref, gf_ref, tm)


FFN_TF = 256


def _ffn(x, g, wg, wu, wd, g_final, l):
    m = x.shape[0]
    tm = min(m, 1024)
    tf = FFN_TF
    nf = D_FF // tf
    single_x = _nbytes((tm, D_MODEL), F32)
    blocks = _nbytes((tm, D_MODEL), F32) + 3 * _nbytes((D_MODEL, tf), F32)
    scratch = _nbytes((tm, D_MODEL), BF16)
    temps = (3 * _nbytes((D_MODEL, tf), BF16) + 3 * _nbytes((tm, tf), F32) + _nbytes((tm, D_MODEL // 2), F32))
    return pl.pallas_call(
        functools.partial(_ffn_kernel, tm=tm, nf=nf, final_norm=(l == DEPTH - 1)),
        out_shape=jax.ShapeDtypeStruct((m, D_MODEL), F32),
        grid=(m // tm, nf),
        in_specs=[
            pl.BlockSpec((tm, D_MODEL), lambda i, j: (i, 0), pipeline_mode=pl.Buffered(1)),
            _gain_spec(l, 2),
            pl.BlockSpec((None, D_MODEL, tf), lambda i, j: (l, 0, j)),
            pl.BlockSpec((None, D_MODEL, tf), lambda i, j: (l, 0, j)),
            pl.BlockSpec((None, tf, D_MODEL), lambda i, j: (l, j, 0)),
            pl.BlockSpec((1, D_MODEL), lambda i, j: (0, 0)),
        ],
        out_specs=pl.BlockSpec((tm, D_MODEL), lambda i, j: (i, 0)),
        scratch_shapes=[pltpu.VMEM((tm, D_MODEL), BF16)],
        compiler_params=_params(("parallel", "arbitrary"), blocks, scratch + single_x, temps),
        name="ffn",
    )(x, g, wg, wu, wd, g_final.reshape(1, D_MODEL))


def _matmul_residual_kernel(a_ref, w_ref, x_ref, o_ref):
    o_ref[...] = x_ref[...] + _dot(a_ref[...].astype(BF16), w_ref[...])


def _matmul_residual(a, w, x, l):
    m, k = a.shape
    n = w.shape[2]
    blocks = _nbytes((m, k), a.dtype) + _nbytes((k, n), BF16) + 2 * _nbytes((m, n), F32)
    return pl.pallas_call(
        _matmul_residual_kernel,
        out_shape=jax.ShapeDtypeStruct((m, n), F32),
        grid=(1,),
        in_specs=[pl.BlockSpec((m, k), lambda i: (0, 0)),
                  pl.BlockSpec((None, k, n), lambda i: (l, 0, 0)),
                  pl.BlockSpec((m, n), lambda i: (0, 0))],
        out_specs=pl.BlockSpec((m, n), lambda i: (0, 0)),
        compiler_params=_params(("arbitrary",), blocks),
        name="matmul_residual",
    )(a, w, x)


def _rope_tables(positions):
    half = HEAD_DIM // 2
    inv = ROPE_BASE ** (-np.arange(half, dtype=np.float64) / half)
    ang = np.asarray(positions, np.float64)[:, None] * inv[None, :]
    cos, sin = np.cos(ang), np.sin(ang)
    c = np.concatenate([cos, cos], axis=-1).astype(np.float32)
    s = np.concatenate([-sin, sin], axis=-1).astype(np.float32)
    return c, s


def _decay_tables():
    c = RET_CHUNK
    lg = np.log1p(-np.exp2(-5.0 - np.arange(RET_HEADS, dtype=np.float64)))
    idx = np.arange(c, dtype=np.float64)
    diff = idx[:, None] - idx[None, :]
    dmask = np.where(diff[None] >= 0, np.exp(lg[:, None, None] * np.maximum(diff, 0.0)[None]), 0.0)
    xi = np.exp(lg[:, None] * (idx + 1.0)[None])
    zeta = np.exp(lg[:, None] * (c - 1.0 - idx)[None])
    bcast = lambda t: np.broadcast_to(t[:, :, None], (RET_HEADS, c, LANES)).astype(np.float32)
    return dmask.astype(np.float32), bcast(xi), bcast(zeta)


def _rotary(x, c, s):
    return x * c + pltpu.roll(x, HEAD_DIM // 2, 1) * s


def _ret_prompt_kernel(q_ref, k_ref, v_ref, g_ref, c_ref, s_ref, dm_ref, xi_ref, zt_ref, gn_ref,
                       o_ref, st_ref, *, nchunk):
    h = pl.program_id(1)

    @pl.when(pl.program_id(0) == 0)
    def _():
        st_ref[h] = jnp.zeros((HEAD_DIM, HEAD_DIM), F32)

    c_tab = c_ref[...]
    s_tab = s_ref[...]
    qr = _rotary(q_ref[...], c_tab, s_tab)
    kr = _rotary(k_ref[...], c_tab, s_tab) * ATT_SCALE
    dmask = dm_ref[h]
    xi = xi_ref[h]
    zeta = zt_ref[h]
    g_chunk = xi[RET_CHUNK - 1:RET_CHUNK, :]
    gn = gn_ref[...]
    state = st_ref[h]
    for c in range(nchunk):
        rows = slice(c * RET_CHUNK, (c + 1) * RET_CHUNK)
        qc = qr[rows]
        kc = kr[rows]
        vc = v_ref[rows, :].astype(BF16)
        s = _dot_nt(qc.astype(BF16), kc.astype(BF16)) * dmask
        inner = _dot(s.astype(BF16), vc)
        cross = _dot((qc * xi).astype(BF16), state.astype(BF16))
        kv = _dot((kc * zeta).T.astype(BF16), vc)
        state = g_chunk * state + kv
        out = inner + cross
        out = out * lax.rsqrt(jnp.mean(out * out, axis=-1, keepdims=True) + EPS) * gn
        gate = g_ref[rows, :]
        o_ref[rows, :] = (gate * _sigmoid(gate) * out).astype(o_ref.dtype)
    st_ref[h] = state


def _ret_prompt(proj, c_tab, s_tab, dmask, xi, zeta, ret_gn, l):
    m = proj.shape[0]
    r = 1024
    nchunk = r // RET_CHUNK
    col = lambda off: pl.BlockSpec((r, HEAD_DIM), lambda i, h: (i, off + h))
    head_tab = pl.BlockSpec((RET_HEADS, RET_CHUNK, LANES), lambda i, h: (0, 0, 0))
    blocks = (6 * _nbytes((r, HEAD_DIM), F32) + 4 * _nbytes((RET_HEADS, RET_CHUNK, LANES), F32)
              + _nbytes((r, HEAD_DIM), BF16))
    return pl.pallas_call(
        functools.partial(_ret_prompt_kernel, nchunk=nchunk),
        out_shape=(jax.ShapeDtypeStruct((m, RET_W), BF16),
                   jax.ShapeDtypeStruct((RET_HEADS, HEAD_DIM, HEAD_DIM), F32)),
        grid=(m // r, RET_HEADS),
        in_specs=[col(COL_RQ), col(COL_RK), col(COL_RV), col(COL_RG),
                  pl.BlockSpec((r, HEAD_DIM), lambda i, h: (i, 0)),
                  pl.BlockSpec((r, HEAD_DIM), lambda i, h: (i, 0)),
                  head_tab, head_tab, head_tab,
                  pl.BlockSpec((None, 1, HEAD_DIM), lambda i, h: (l, 0, h))],
        out_specs=(pl.BlockSpec((r, HEAD_DIM), lambda i, h: (i, h)),
                   pl.BlockSpec((RET_HEADS, HEAD_DIM, HEAD_DIM), lambda i, h: (0, 0, 0))),
        compiler_params=_params(("arbitrary", "arbitrary"), blocks, 0, 4 * _nbytes((r, HEAD_DIM), F32)),
        name="ret_prompt",
    )(proj, proj, proj, proj, c_tab, s_tab, dmask, xi, zeta, ret_gn)


def _swa_prompt_kernel(sink_ref, q_ref, k_ref, v_ref, kp_ref, vp_ref, o_ref, *, nblk, l):
    kvh = pl.program_id(0)
    i = pl.program_id(1)
    kfull = jnp.concatenate([kp_ref[...], k_ref[...]], axis=0).astype(BF16)
    vfull = jnp.concatenate([vp_ref[...], v_ref[...]], axis=0).astype(BF16)
    row = lax.broadcasted_iota(jnp.int32, (WINDOW, 2 * WINDOW), 0)
    col = lax.broadcasted_iota(jnp.int32, (WINDOW, 2 * WINDOW), 1)
    band = (col >= row) & (col <= row + WINDOW)
    for j in range(nblk):
        rows = slice(j * WINDOW, (j + 1) * WINDOW)
        kk = kfull[j * WINDOW:(j + 2) * WINDOW]
        vv = vfull[j * WINDOW:(j + 2) * WINDOW]
        if j == 0:
            mask = band & (col >= jnp.where(i > 0, 0, WINDOW))
        else:
            mask = band
        qj = q_ref[rows, :].astype(BF16)
        qs = jnp.concatenate([qj[:, h * HEAD_DIM:(h + 1) * HEAD_DIM] for h in range(SWA_GROUP)], axis=0)
        s = _dot_nt(qs, kk) * ATT_SCALE
        ps = []
        for h in range(SWA_GROUP):
            sh = jnp.where(mask, s[h * WINDOW:(h + 1) * WINDOW], -jnp.inf)
            sink = sink_ref[l, kvh * SWA_GROUP + h]
            mx = jnp.maximum(jnp.max(sh, axis=-1, keepdims=True), sink)
            e = jnp.exp(sh - mx)
            den = jnp.sum(e, axis=-1, keepdims=True) + jnp.exp(sink - mx)
            ps.append((e / den).astype(BF16))
        o = _dot(jnp.concatenate(ps, axis=0), vv)
        for h in range(SWA_GROUP):
            o_ref[rows, h * HEAD_DIM:(h + 1) * HEAD_DIM] = o[h * WINDOW:(h + 1) * WINDOW].astype(o_ref.dtype)


def _swa_prompt(proj, sinks, l):
    m = proj.shape[0]
    r = 512
    nblk = r // WINDOW
    gw = SWA_GROUP * HEAD_DIM
    prev = lambda off: pl.BlockSpec(
        (WINDOW, HEAD_DIM), lambda kv, i: (jnp.maximum(i * nblk - 1, 0), off + kv))
    blocks = (_nbytes((r, gw), F32) + 2 * _nbytes((r, HEAD_DIM), F32) + 2 * _nbytes((WINDOW, HEAD_DIM), F32)
              + _nbytes((r, gw), BF16))
    return pl.pallas_call(
        functools.partial(_swa_prompt_kernel, nblk=nblk, l=l),
        out_shape=jax.ShapeDtypeStruct((m, SWA_W), BF16),
        grid=(SWA_KV_HEADS, m // r),
        in_specs=[pl.BlockSpec(memory_space=pltpu.SMEM),
                  pl.BlockSpec((r, gw), lambda kv, i: (i, COL_SQ // SWA_GROUP + kv)),
                  pl.BlockSpec((r, HEAD_DIM), lambda kv, i: (i, COL_SK + kv)),
                  pl.BlockSpec((r, HEAD_DIM), lambda kv, i: (i, COL_SV + kv)),
                  prev(COL_SK), prev(COL_SV)],
        out_specs=pl.BlockSpec((r, gw), lambda kv, i: (i, kv)),
        compiler_params=_params(("parallel", "parallel"), blocks, 0, 8 * _nbytes((gw, 2 * WINDOW), F32)),
        name="swa_prompt",
    )(sinks, proj, proj, proj, proj, proj)


def _mem_kv_kernel(m_ref, g_ref, wk_ref, wv_ref, k_ref, v_ref):
    h = _rms_rows(m_ref[...], g_ref[...]).astype(BF16)
    k_ref[...] = _dot(h, wk_ref[...])
    v_ref[...] = _dot(h, wv_ref[...])


def _mem_kv(mem, g, wk, wv, l):
    full = lambda shape: pl.BlockSpec(shape, lambda i: (0,) * len(shape))
    layer = lambda shape: pl.BlockSpec((None,) + shape, lambda i: (l,) + (0,) * len(shape))
    blocks = (_nbytes((MEM_LEN, D_MODEL), F32) + 2 * _nbytes((D_MODEL, MEM_W), BF16)
              + 2 * _nbytes((MEM_LEN, MEM_W), F32))
    return pl.pallas_call(
        _mem_kv_kernel,
        out_shape=(jax.ShapeDtypeStruct((MEM_LEN, MEM_W), F32),) * 2,
        grid=(1,),
        in_specs=[full((MEM_LEN, D_MODEL)), _gain_spec(l, 1), layer((D_MODEL, MEM_W)), layer((D_MODEL, MEM_W))],
        out_specs=(full((MEM_LEN, MEM_W)),) * 2,
        compiler_params=_params(("arbitrary",), blocks, 0, _nbytes((MEM_LEN, D_MODEL), F32)),
        name="mem_kv",
    )(mem, g, wk, wv)


def _cross_prompt_kernel(q_ref, mk_ref, mv_ref, wo_ref, x_ref, o_ref):
    q = q_ref[...].astype(BF16)
    mk = mk_ref[...].astype(BF16)
    mv = mv_ref[...].astype(BF16)
    outs = []
    for h in range(MEM_HEADS):
        cols = slice(h * HEAD_DIM, (h + 1) * HEAD_DIM)
        s = _dot_nt(q[:, cols], mk[:, cols]) * ATT_SCALE
        e = jnp.exp(s - jnp.max(s, axis=-1, keepdims=True))
        p = e / jnp.sum(e, axis=-1, keepdims=True)
        outs.append(_dot(p.astype(BF16), mv[:, cols]).astype(BF16))
    o_ref[...] = x_ref[...] + _dot(jnp.concatenate(outs, axis=1), wo_ref[...])


def _cross_prompt(q, mk, mv, wo, x, l):
    m = x.shape[0]
    tm = 512
    blocks = (_nbytes((tm, MEM_W), F32) + 2 * _nbytes((MEM_LEN, MEM_W), F32) + _nbytes((MEM_W, D_MODEL), BF16)
              + 2 * _nbytes((tm, D_MODEL), F32))
    return pl.pallas_call(
        _cross_prompt_kernel,
        out_shape=jax.ShapeDtypeStruct((m, D_MODEL), F32),
        grid=(m // tm,),
        in_specs=[pl.BlockSpec((tm, MEM_W), lambda i: (i, 0)),
                  pl.BlockSpec((MEM_LEN, MEM_W), lambda i: (0, 0)),
                  pl.BlockSpec((MEM_LEN, MEM_W), lambda i: (0, 0)),
                  pl.BlockSpec((None, MEM_W, D_MODEL), lambda i: (l, 0, 0)),
                  pl.BlockSpec((tm, D_MODEL), lambda i: (i, 0))],
        out_specs=pl.BlockSpec((tm, D_MODEL), lambda i: (i, 0)),
        compiler_params=_params(("parallel",), blocks, 0, 2 * _nbytes((tm, D_MODEL), F32)),
        name="cross_prompt",
    )(q, mk, mv, wo, x)


def _chain(prev):
    if prev is None:
        return (), ()
    return (prev,), (pl.BlockSpec(memory_space=pl.ANY),)


def _ret_sample_kernel(gam_ref, q_ref, k_ref, v_ref, g_ref, c_ref, s_ref, gn_ref, st_ref, *rest, bt, nt):
    o_ref, nst_ref, qt_ref, kt_ref = rest[-4:]
    h = pl.program_id(0)
    t = pl.program_id(1)

    @pl.when(t == 0)
    def _():
        c_tab = c_ref[...]
        s_tab = s_ref[...]
        qt = _rotary(q_ref[...], c_tab, s_tab).T
        kt = (_rotary(k_ref[...], c_tab, s_tab) * ATT_SCALE).T
        for tt in range(nt):
            shift = (LANES - tt * bt) % LANES
            qt_ref[tt] = pltpu.roll(qt, shift, 1) if shift else qt
            kt_ref[tt] = pltpu.roll(kt, shift, 1) if shift else kt

    gamma = gam_ref[h]
    qt = qt_ref[t]
    kt = kt_ref[t]
    outs = []
    for ib in range(bt):
        new = gamma * st_ref[ib] + kt[:, ib:ib + 1] * v_ref[ib:ib + 1, :]
        nst_ref[ib] = new
        outs.append(jnp.sum(qt[:, ib:ib + 1] * new, axis=0, keepdims=True))
    out = jnp.concatenate(outs, axis=0)
    out = out * lax.rsqrt(jnp.mean(out * out, axis=-1, keepdims=True) + EPS) * gn_ref[...]
    gate = g_ref[...]
    o_ref[...] = gate * _sigmoid(gate) * out


def _ret_sample(proj, state, prev, gammas, c_tab, s_tab, ret_gn, l):
    b = proj.shape[0]
    bt = 32
    nt = b // bt
    col = lambda off: pl.BlockSpec((b, HEAD_DIM), lambda h, t: (0, off + h))
    tile = lambda off: pl.BlockSpec((bt, HEAD_DIM), lambda h, t: (t, off + h))
    row = pl.BlockSpec((1, HEAD_DIM), lambda h, t: (0, 0))
    st_spec = pl.BlockSpec((None, bt, None, HEAD_DIM, HEAD_DIM), lambda h, t: (l, t, h, 0, 0))
    extra, extra_specs = _chain(prev)
    n_in = 9 + len(extra)
    blocks = (2 * _nbytes((b, HEAD_DIM), F32) + 3 * _nbytes((bt, HEAD_DIM), F32)
              + 2 * _nbytes((bt, HEAD_DIM, HEAD_DIM), F32))
    scratch = 2 * _nbytes((nt, HEAD_DIM, LANES), F32)
    return pl.pallas_call(
        functools.partial(_ret_sample_kernel, bt=bt, nt=nt),
        out_shape=(jax.ShapeDtypeStruct((b, RET_W), F32), jax.ShapeDtypeStruct(state.shape, F32)),
        grid=(RET_HEADS, nt),
        in_specs=[pl.BlockSpec(memory_space=pltpu.SMEM),
                  col(COL_RQ), col(COL_RK), tile(COL_RV), tile(COL_RG), row, row,
                  pl.BlockSpec((None, 1, HEAD_DIM), lambda h, t: (l, 0, h)),
                  st_spec, *extra_specs],
        out_specs=(tile(0), st_spec),
        scratch_shapes=[pltpu.VMEM((nt, HEAD_DIM, LANES), F32), pltpu.VMEM((nt, HEAD_DIM, LANES), F32)],
        input_output_aliases=({n_in - 1: 1} if extra else {}),
        compiler_params=_params(("parallel", "arbitrary"), blocks, scratch, 4 * _nbytes((bt, HEAD_DIM, HEAD_DIM), F32)),
        name="ret_sample",
    )(gammas, proj, proj, proj, proj, c_tab, s_tab, ret_gn, state, *extra)


def _swa_sample_kernel(sink_ref, q_ref, kn_ref, vn_ref, ck_ref, cv_ref, *rest, bt):
    o_ref, nk_ref, nv_ref = rest[-3:]
    rows = SWA_KV_HEADS * WINDOW
    sink = sink_ref[...]
    hrow = lax.broadcasted_iota(jnp.int32, (SWA_HEADS, HEAD_DIM), 0)
    first_group = hrow < SWA_GROUP
    srow = lax.broadcasted_iota(jnp.int32, (SWA_HEADS, rows), 0)
    scol = lax.broadcasted_iota(jnp.int32, (SWA_HEADS, rows), 1)
    own = (scol & (SWA_KV_HEADS - 1)) == (srow // SWA_GROUP)

    def body(ib, carry):
        q = q_ref[ib]
        kn = kn_ref[ib]
        vn = vn_ref[ib]
        k_new = jnp.where(first_group, kn[0:1, :], kn[1:2, :])
        v_new = jnp.where(first_group, vn[0:1, :], vn[1:2, :])
        s_new = jnp.sum(q * k_new, axis=-1, keepdims=True) * ATT_SCALE
        ck = ck_ref[ib]
        cv = cv_ref[ib]
        s = jnp.where(own, _dot_nt(q.astype(BF16), ck.astype(BF16)) * ATT_SCALE, -jnp.inf)
        mx = jnp.maximum(jnp.maximum(jnp.max(s, axis=-1, keepdims=True), s_new), sink)
        e = jnp.exp(s - mx)
        e_new = jnp.exp(s_new - mx)
        den = jnp.sum(e, axis=-1, keepdims=True) + e_new + jnp.exp(sink - mx)
        o_ref[ib] = _dot((e / den).astype(BF16), cv.astype(BF16)) + (e_new / den) * v_new
        nk_ref[ib, 0:rows - SWA_KV_HEADS, :] = ck[SWA_KV_HEADS:rows, :]
        nv_ref[ib, 0:rows - SWA_KV_HEADS, :] = cv[SWA_KV_HEADS:rows, :]
        nk_ref[ib, rows - SWA_KV_HEADS:rows, :] = kn
        nv_ref[ib, rows - SWA_KV_HEADS:rows, :] = vn
        return carry

    lax.fori_loop(0, bt, body, 0, unroll=4)


def _swa_sample(q, k_new, v_new, cache_k, cache_v, prev_k, prev_v, sinks, l):
    b = q.shape[0]
    bt = 16
    rows = SWA_KV_HEADS * WINDOW
    blocks = (2 * _nbytes((bt, SWA_HEADS, HEAD_DIM), F32) + 2 * _nbytes((bt, 8, HEAD_DIM), F32)
              + 4 * _nbytes((bt, rows, HEAD_DIM), F32))
    cache_spec = pl.BlockSpec((None, bt, rows, HEAD_DIM), lambda t: (l, t, 0, 0))
    new_spec = pl.BlockSpec((bt, SWA_KV_HEADS, HEAD_DIM), lambda t: (t, 0, 0))
    head_spec = pl.BlockSpec((bt, SWA_HEADS, HEAD_DIM), lambda t: (t, 0, 0))
    extra_k, spec_k = _chain(prev_k)
    extra_v, spec_v = _chain(prev_v)
    aliases = {6: 1, 7: 2} if extra_k else {}
    return pl.pallas_call(
        functools.partial(_swa_sample_kernel, bt=bt),
        out_shape=(jax.ShapeDtypeStruct((b, SWA_HEADS, HEAD_DIM), F32),
                   jax.ShapeDtypeStruct(cache_k.shape, F32), jax.ShapeDtypeStruct(cache_v.shape, F32)),
        grid=(b // bt,),
        in_specs=[pl.BlockSpec((None, SWA_HEADS, 1), lambda t: (l, 0, 0)),
                  head_spec, new_spec, new_spec, cache_spec, cache_spec, *spec_k, *spec_v],
        out_specs=(head_spec, cache_spec, cache_spec),
        input_output_aliases=aliases,
        compiler_params=_params(("parallel",), blocks),
        name="swa_sample",
    )(sinks.reshape(DEPTH, SWA_HEADS, 1), q, k_new, v_new, cache_k, cache_v, *extra_k, *extra_v)


def _cross_sample_kernel(q_ref, mk_ref, mv_ref, o_ref, *, bt):
    rows = MEM_HEADS * MEM_LEN
    srow = lax.broadcasted_iota(jnp.int32, (MEM_HEADS, rows), 0)
    scol = lax.broadcasted_iota(jnp.int32, (MEM_HEADS, rows), 1)
    own = (scol & (MEM_HEADS - 1)) == srow

    def body(ib, carry):
        qb = q_ref[ib].astype(BF16)
        s = jnp.where(own, _dot_nt(qb, mk_ref[ib].astype(BF16)) * ATT_SCALE, -jnp.inf)
        e = jnp.exp(s - jnp.max(s, axis=-1, keepdims=True))
        p = (e / jnp.sum(e, axis=-1, keepdims=True)).astype(BF16)
        o_ref[ib] = _dot(p, mv_ref[ib].astype(BF16))
        return carry

    lax.fori_loop(0, bt, body, 0, unroll=4)


def _cross_sample(q, cache_k, cache_v, l):
    b = q.shape[0]
    bt = 8
    rows = MEM_HEADS * MEM_LEN
    blocks = 2 * _nbytes((bt, 8, HEAD_DIM), F32) + 2 * _nbytes((bt, rows, HEAD_DIM), F32)
    cache_spec = pl.BlockSpec((None, bt, rows, HEAD_DIM), lambda t: (l, t, 0, 0))
    head_spec = pl.BlockSpec((bt, MEM_HEADS, HEAD_DIM), lambda t: (t, 0, 0))
    return pl.pallas_call(
        functools.partial(_cross_sample_kernel, bt=bt),
        out_shape=jax.ShapeDtypeStruct((b, MEM_HEADS, HEAD_DIM), F32),
        grid=(b // bt,),
        in_specs=[head_spec, cache_spec, cache_spec],
        out_specs=head_spec,
        compiler_params=_params(("parallel",), blocks),
        name="cross_sample",
    )(q, cache_k, cache_v)


def kernel(x_prompt, x_sample, mem_prompt, state_ret, cache_win_k, cache_win_v, cache_mem_k, cache_mem_v,
           g_mix, w_in, ret_gn, sinks, w_out, g_cross, g_mem, wq_c, wk_c, wv_c, wo_c,
           g_ffn, w_gate, w_up, w_down, g_final):
    seq = x_prompt.shape[1]
    nb = x_sample.shape[0]
    w_in_b, w_out_b = _to_bf16_tiles(w_in, IN_PROJ_TN), _to_bf16(w_out)
    wq_b, wk_b, wv_b, wo_b = (_to_bf16(w) for w in (wq_c, wk_c, wv_c, wo_c))
    wg_b, wu_b, wd_b = w_gate, w_up, w_down
    gain = lambda g: g.reshape(DEPTH, 1, g.shape[-1])
    g_mix, g_cross, g_mem, g_ffn, ret_gn = (gain(g) for g in (g_mix, g_cross, g_mem, g_ffn, ret_gn))

    cp, sp = (jnp.asarray(t) for t in _rope_tables(np.arange(seq)))
    cs, ss = (jnp.asarray(t) for t in _rope_tables(np.array([PAST_LEN])))
    dmask, xi, zeta = (jnp.asarray(t) for t in _decay_tables())
    gammas = jnp.asarray((1.0 - np.exp2(-5.0 - np.arange(RET_HEADS))).astype(np.float32))

    xp = x_prompt[0]
    xs = x_sample[:, 0]
    mem = mem_prompt[0]
    kvw = SWA_KV_HEADS * HEAD_DIM
    win_k = cache_win_k.reshape(DEPTH, nb, WINDOW * SWA_KV_HEADS, HEAD_DIM)
    win_v = cache_win_v.reshape(DEPTH, nb, WINDOW * SWA_KV_HEADS, HEAD_DIM)
    mem_k = cache_mem_k.reshape(DEPTH, nb, MEM_LEN * MEM_HEADS, HEAD_DIM)
    mem_v = cache_mem_v.reshape(DEPTH, nb, MEM_LEN * MEM_HEADS, HEAD_DIM)

    ret_p, wink_p, winv_p, memk_p, memv_p = [], [], [], [], []
    ret_s = wink_s = winv_s = None
    for l in range(DEPTH):
        proj = _in_proj(xp, g_mix, w_in_b, l)
        ro, r_last = _ret_prompt(proj, cp, sp, dmask, xi, zeta, ret_gn, l)
        so = _swa_prompt(proj, sinks, l)
        x1, qc = _out_proj(ro, so, xp, w_out_b, g_cross, wq_b, l)
        mk, mv = _mem_kv(mem, g_mem, wk_b, wv_b, l)
        x2 = _cross_prompt(qc, mk, mv, wo_b, x1, l)
        xp = _ffn(x2, g_ffn, wg_b, wu_b, wd_b, g_final, l)
        ret_p.append(r_last[None])
        tail = proj[seq - WINDOW:]
        wink_p.append(tail[:, COL_SK * LANES:COL_SK * LANES + kvw].reshape(1, WINDOW, SWA_KV_HEADS, HEAD_DIM))
        winv_p.append(tail[:, COL_SV * LANES:COL_SV * LANES + kvw].reshape(1, WINDOW, SWA_KV_HEADS, HEAD_DIM))
        memk_p.append(mk.reshape(1, MEM_LEN, MEM_HEADS, HEAD_DIM))
        memv_p.append(mv.reshape(1, MEM_LEN, MEM_HEADS, HEAD_DIM))

        proj_s = _in_proj(xs, g_mix, w_in_b, l)
        ro_s, ret_s = _ret_sample(proj_s, state_ret, ret_s, gammas, cs, ss, ret_gn, l)
        sq = proj_s[:, COL_SQ * LANES:COL_SQ * LANES + SWA_W].reshape(nb, SWA_HEADS, HEAD_DIM)
        sk = proj_s[:, COL_SK * LANES:COL_SK * LANES + kvw].reshape(nb, SWA_KV_HEADS, HEAD_DIM)
        sv = proj_s[:, COL_SV * LANES:COL_SV * LANES + kvw].reshape(nb, SWA_KV_HEADS, HEAD_DIM)
        so_s, wink_s, winv_s = _swa_sample(sq, sk, sv, win_k, win_v, wink_s, winv_s, sinks, l)
        x1_s, qc_s = _out_proj(ro_s, so_s.reshape(nb, SWA_W), xs, w_out_b, g_cross, wq_b, l)
        oc_s = _cross_sample(qc_s.reshape(nb, MEM_HEADS, HEAD_DIM), mem_k, mem_v, l)
        x2_s = _matmul_residual(oc_s.reshape(nb, MEM_W), wo_b, x1_s, l)
        xs = _ffn(x2_s, g_ffn, wg_b, wu_b, wd_b, g_final, l)

    cache_shape = (DEPTH, nb, WINDOW, SWA_KV_HEADS, HEAD_DIM)
    return (xp[None], xs[:, None],
            jnp.stack(ret_p), jnp.stack(wink_p), jnp.stack(winv_p), jnp.stack(memk_p), jnp.stack(memv_p),
            ret_s, wink_s.reshape(cache_shape), winv_s.reshape(cache_shape))
```

```python
import functools

import numpy as np
import jax
import jax.numpy as jnp
from jax import lax
from jax.experimental import pallas as pl
from jax.experimental.pallas import tpu as pltpu

D_MODEL = 2048
DEPTH = 2
PAST_LEN = 8192
HEAD_DIM = 128
RET_HEADS = 8
RET_W = RET_HEADS * HEAD_DIM
RET_CHUNK = 256
SWA_HEADS = 8
SWA_KV_HEADS = 2
SWA_GROUP = SWA_HEADS // SWA_KV_HEADS
SWA_W = SWA_HEADS * HEAD_DIM
WINDOW = 128
MEM_LEN = 256
MEM_HEADS = 4
MEM_W = MEM_HEADS * HEAD_DIM
IN_W = 4 * RET_W + SWA_W + 2 * SWA_KV_HEADS * HEAD_DIM
D_FF = 5632
ROPE_BASE = 10000.0
EPS = 1e-6
ATT_SCALE = HEAD_DIM ** -0.5

COL_RQ, COL_RK, COL_RV, COL_RG = 0, 8, 16, 24
COL_SQ, COL_SK, COL_SV = 32, 40, 42

V7X_VMEM_BYTES = 64 * 1024 * 1024
V7X_VMEM_BUDGET = V7X_VMEM_BYTES - 8 * 1024 * 1024
LANES = 128
CAST_BLOCK_BYTES = 4 * 1024 * 1024

BF16 = jnp.bfloat16
F32 = jnp.float32


def _vmem_limit(block_bytes, scratch_bytes=0, temp_bytes=0):
    need = 2 * block_bytes + scratch_bytes + temp_bytes + 4 * 1024 * 1024
    return int(min(max(need, 16 * 1024 * 1024), V7X_VMEM_BUDGET))


def _nbytes(shape, dtype):
    return int(np.prod(shape)) * jnp.dtype(dtype).itemsize


def _params(semantics, block_bytes, scratch_bytes=0, temp_bytes=0):
    return pltpu.CompilerParams(
        dimension_semantics=semantics,
        vmem_limit_bytes=_vmem_limit(block_bytes, scratch_bytes, temp_bytes))


def _rms_rows(x, g):
    ms = jnp.mean(x * x, axis=-1, keepdims=True)
    return x * lax.rsqrt(ms + EPS) * g


def _rms_to(dst_ref, src_ref, g_ref, rows):
    g = g_ref[...]
    chunk = min(rows, 128)

    def body(c, carry):
        r = pl.multiple_of(c * chunk, chunk)
        x = src_ref[pl.ds(r, chunk), :]
        dst_ref[pl.ds(r, chunk), :] = _rms_rows(x, g).astype(dst_ref.dtype)
        return carry

    lax.fori_loop(0, rows // chunk, body, 0)


def _sigmoid(x):
    return 1.0 / (1.0 + jnp.exp(-x))


def _dot(a, b):
    return jnp.dot(a, b, preferred_element_type=F32)


def _dot_nt(a, b):
    return lax.dot_general(a, b, (((1,), (1,)), ((), ())), preferred_element_type=F32)


def _gain_spec(l, ngrid):
    zeros = (0,) * 2
    return pl.BlockSpec((None, 1, D_MODEL), lambda *_: (l,) + zeros)


def _cast_kernel(x_ref, o_ref):
    o_ref[...] = x_ref[...].astype(o_ref.dtype)


def _to_bf16(w):
    depth, r, c = w.shape
    tr = r
    while _nbytes((tr, c), F32) > CAST_BLOCK_BYTES and tr % 32 == 0:
        tr //= 2
    spec = pl.BlockSpec((None, tr, c), lambda d, i: (d, i, 0))
    return pl.pallas_call(
        _cast_kernel,
        out_shape=jax.ShapeDtypeStruct(w.shape, BF16),
        grid=(depth, r // tr),
        in_specs=[spec],
        out_specs=spec,
        compiler_params=_params(("parallel", "parallel"), _nbytes((tr, c), F32) + _nbytes((tr, c), BF16)),
        name="to_bf16",
    )(w)


def _cast_tiles_kernel(x_ref, o_ref, *, nt, tn):
    for j in range(nt):
        o_ref[j] = x_ref[:, j * tn:(j + 1) * tn].astype(o_ref.dtype)


def _to_bf16_tiles(w, tn):
    depth, r, c = w.shape
    nt = c // tn
    tr = r
    while _nbytes((tr, c), F32) > CAST_BLOCK_BYTES and tr % 32 == 0:
        tr //= 2
    return pl.pallas_call(
        functools.partial(_cast_tiles_kernel, nt=nt, tn=tn),
        out_shape=jax.ShapeDtypeStruct((depth, nt, r, tn), BF16),
        grid=(depth, r // tr),
        in_specs=[pl.BlockSpec((None, tr, c), lambda d, i: (d, i, 0))],
        out_specs=pl.BlockSpec((None, nt, tr, tn), lambda d, i: (d, 0, i, 0)),
        compiler_params=_params(("parallel", "parallel"), _nbytes((tr, c), F32) + _nbytes((tr, c), BF16)),
        name="to_bf16_tiles",
    )(w)


def _in_proj_kernel(x_ref, g_ref, w_ref, o_ref, h_ref, *, tm):
    @pl.when(pl.program_id(1) == 0)
    def _():
        _rms_to(h_ref, x_ref, g_ref, tm)

    o_ref[...] = _dot(h_ref[...], w_ref[...])


IN_PROJ_TN = 512


def _in_proj(x, g, w, l):
    m = x.shape[0]
    tm = min(m, 1024)
    tn = IN_PROJ_TN
    blocks = (_nbytes((tm, D_MODEL), F32) + _nbytes((D_MODEL, tn), BF16) + _nbytes((tm, tn), F32))
    return pl.pallas_call(
        functools.partial(_in_proj_kernel, tm=tm),
        out_shape=jax.ShapeDtypeStruct((m, IN_W), F32),
        grid=(m // tm, IN_W // tn),
        in_specs=[
            pl.BlockSpec((tm, D_MODEL), lambda i, j: (i, 0)),
            _gain_spec(l, 2),
            pl.BlockSpec((None, None, D_MODEL, tn), lambda i, j: (l, j, 0, 0)),
        ],
        out_specs=pl.BlockSpec((tm, tn), lambda i, j: (i, j)),
        scratch_shapes=[pltpu.VMEM((tm, D_MODEL), BF16)],
        compiler_params=_params(("parallel", "arbitrary"), blocks, _nbytes((tm, D_MODEL), BF16)),
        name="in_proj",
    )(x, g, w)


def _out_proj_kernel(ro_ref, so_ref, x_ref, wa_ref, wb_ref, gc_ref, wq_ref, x1_ref, q_ref, h_ref, *, tm):
    y = _dot(ro_ref[...].astype(BF16), wa_ref[...]) + _dot(so_ref[...].astype(BF16), wb_ref[...])
    x1_ref[...] = x_ref[...] + y
    _rms_to(h_ref, x1_ref, gc_ref, tm)
    q_ref[...] = _dot(h_ref[...], wq_ref[...])


def _out_proj(ro, so, x, w_out, g_cross, wq, l):
    m = x.shape[0]
    tm = min(m, 512)
    blocks = (_nbytes((tm, RET_W), ro.dtype) + _nbytes((tm, SWA_W), so.dtype)
              + 2 * _nbytes((tm, D_MODEL), F32) + _nbytes((D_MODEL, D_MODEL), BF16)
              + _nbytes((D_MODEL, MEM_W), BF16) + _nbytes((tm, MEM_W), F32))
    return pl.pallas_call(
        functools.partial(_out_proj_kernel, tm=tm),
        out_shape=(jax.ShapeDtypeStruct((m, D_MODEL), F32), jax.ShapeDtypeStruct((m, MEM_W), F32)),
        grid=(m // tm,),
        in_specs=[
            pl.BlockSpec((tm, RET_W), lambda i: (i, 0)),
            pl.BlockSpec((tm, SWA_W), lambda i: (i, 0)),
            pl.BlockSpec((tm, D_MODEL), lambda i: (i, 0)),
            pl.BlockSpec((None, RET_W, D_MODEL), lambda i: (l, 0, 0)),
            pl.BlockSpec((None, SWA_W, D_MODEL), lambda i: (l, 1, 0)),
            _gain_spec(l, 1),
            pl.BlockSpec((None, D_MODEL, MEM_W), lambda i: (l, 0, 0)),
        ],
        out_specs=(pl.BlockSpec((tm, D_MODEL), lambda i: (i, 0)),
                   pl.BlockSpec((tm, MEM_W), lambda i: (i, 0))),
        scratch_shapes=[pltpu.VMEM((tm, D_MODEL), BF16)],
        compiler_params=_params(("parallel",), blocks, _nbytes((tm, D_MODEL), BF16),
                                _nbytes((tm, D_MODEL), F32)),
        name="out_proj",
    )(ro, so, x, w_out, w_out, g_cross, wq)


def _ffn_kernel(x_ref, g_ref, wg_ref, wu_ref, wd_ref, gf_ref, o_ref, h_ref, *, tm, nf, final_norm):
    j = pl.program_id(1)

    @pl.when(j == 0)
    def _():
        _rms_to(h_ref, x_ref, g_ref, tm)
        o_ref[...] = x_ref[...]

    h = h_ref[...]
    a = _dot(h, wg_ref[...])
    u = _dot(h, wu_ref[...])
    act = (a * _sigmoid(a) * u).astype(BF16)
    half = D_MODEL // 2
    for c in range(2):
        cols = slice(c * half, (c + 1) * half)
        o_ref[:, cols] += _dot(act, wd_ref[:, cols])

    if final_norm:
        @pl.when(j == nf - 1)
        def _():
            _rms_to(o_ref, o_ref, gf_ref, tm)


FFN_TF = 512


def _ffn(x, g, wg, wu, wd, g_final, l):
    m = x.shape[0]
    tm = min(m, 1024)
    tf = FFN_TF
    nf = D_FF // tf
    single_x = _nbytes((tm, D_MODEL), F32)
    blocks = _nbytes((tm, D_MODEL), F32) + 3 * _nbytes((D_MODEL, tf), BF16)
    scratch = _nbytes((tm, D_MODEL), BF16)
    temps = 3 * _nbytes((tm, tf), F32) + _nbytes((tm, D_MODEL // 2), F32)
    return pl.pallas_call(
        functools.partial(_ffn_kernel, tm=tm, nf=nf, final_norm=(l == DEPTH - 1)),
        out_shape=jax.ShapeDtypeStruct((m, D_MODEL), F32),
        grid=(m // tm, nf),
        in_specs=[
            pl.BlockSpec((tm, D_MODEL), lambda i, j: (i, 0), pipeline_mode=pl.Buffered(1)),
            _gain_spec(l, 2),
            pl.BlockSpec((None, None, D_MODEL, tf), lambda i, j: (l, j, 0, 0)),
            pl.BlockSpec((None, None, D_MODEL, tf), lambda i, j: (l, j, 0, 0)),
            pl.BlockSpec((None, tf, D_MODEL), lambda i, j: (l, j, 0)),
            pl.BlockSpec((1, D_MODEL), lambda i, j: (0, 0)),
        ],
        out_specs=pl.BlockSpec((tm, D_MODEL), lambda i, j: (i, 0)),
        scratch_shapes=[pltpu.VMEM((tm, D_MODEL), BF16)],
        compiler_params=_params(("parallel", "arbitrary"), blocks, scratch + single_x, temps),
        name="ffn",
    )(x, g, wg, wu, wd, g_final.reshape(1, D_MODEL))


def _matmul_residual_kernel(a_ref, w_ref, x_ref, o_ref):
    o_ref[...] = x_ref[...] + _dot(a_ref[...].astype(BF16), w_ref[...])


def _matmul_residual(a, w, x, l):
    m, k = a.shape
    n = w.shape[2]
    blocks = _nbytes((m, k), a.dtype) + _nbytes((k, n), BF16) + 2 * _nbytes((m, n), F32)
    return pl.pallas_call(
        _matmul_residual_kernel,
        out_shape=jax.ShapeDtypeStruct((m, n), F32),
        grid=(1,),
        in_specs=[pl.BlockSpec((m, k), lambda i: (0, 0)),
                  pl.BlockSpec((None, k, n), lambda i: (l, 0, 0)),
                  pl.BlockSpec((m, n), lambda i: (0, 0))],
        out_specs=pl.BlockSpec((m, n), lambda i: (0, 0)),
        compiler_params=_params(("arbitrary",), blocks),
        name="matmul_residual",
    )(a, w, x)


def _rope_tables(positions):
    half = HEAD_DIM // 2
    inv = ROPE_BASE ** (-np.arange(half, dtype=np.float64) / half)
    ang = np.asarray(positions, np.float64)[:, None] * inv[None, :]
    cos, sin = np.cos(ang), np.sin(ang)
    c = np.concatenate([cos, cos], axis=-1).astype(np.float32)
    s = np.concatenate([-sin, sin], axis=-1).astype(np.float32)
    return c, s


def _decay_tables():
    c = RET_CHUNK
    lg = np.log1p(-np.exp2(-5.0 - np.arange(RET_HEADS, dtype=np.float64)))
    idx = np.arange(c, dtype=np.float64)
    diff = idx[:, None] - idx[None, :]
    dmask = np.where(diff[None] >= 0, np.exp(lg[:, None, None] * np.maximum(diff, 0.0)[None]), 0.0)
    xi = np.exp(lg[:, None] * (idx + 1.0)[None])
    zeta = np.exp(lg[:, None] * (c - 1.0 - idx)[None])
    bcast = lambda t: np.broadcast_to(t[:, :, None], (RET_HEADS, c, LANES)).astype(np.float32)
    return dmask.astype(np.float32), bcast(xi), bcast(zeta)


def _rotary(x, c, s):
    return x * c + pltpu.roll(x, HEAD_DIM // 2, 1) * s


def _ret_prompt_kernel(q_ref, k_ref, v_ref, g_ref, c_ref, s_ref, dm_ref, xi_ref, zt_ref, gn_ref,
                       o_ref, st_ref, *, nchunk):
    h = pl.program_id(1)

    @pl.when(pl.program_id(0) == 0)
    def _():
        st_ref[h] = jnp.zeros((HEAD_DIM, HEAD_DIM), F32)

    c_tab = c_ref[...]
    s_tab = s_ref[...]
    qr = _rotary(q_ref[...], c_tab, s_tab)
    kr = _rotary(k_ref[...], c_tab, s_tab) * ATT_SCALE
    dmask = dm_ref[h]
    xi = xi_ref[h]
    zeta = zt_ref[h]
    g_chunk = xi[RET_CHUNK - 1:RET_CHUNK, :]
    gn = gn_ref[...]
    state = st_ref[h]
    for c in range(nchunk):
        rows = slice(c * RET_CHUNK, (c + 1) * RET_CHUNK)
        qc = qr[rows]
        kc = kr[rows]
        vc = v_ref[rows, :].astype(BF16)
        s = _dot_nt(qc.astype(BF16), kc.astype(BF16)) * dmask
        inner = _dot(s.astype(BF16), vc)
        cross = _dot((qc * xi).astype(BF16), state.astype(BF16))
        kv = _dot((kc * zeta).T.astype(BF16), vc)
        state = g_chunk * state + kv
        out = inner + cross
        out = out * lax.rsqrt(jnp.mean(out * out, axis=-1, keepdims=True) + EPS) * gn
        gate = g_ref[rows, :]
        o_ref[rows, :] = (gate * _sigmoid(gate) * out).astype(o_ref.dtype)
    st_ref[h] = state


def _ret_prompt(proj, c_tab, s_tab, dmask, xi, zeta, ret_gn, l):
    m = proj.shape[0]
    r = 1024
    nchunk = r // RET_CHUNK
    col = lambda off: pl.BlockSpec((r, HEAD_DIM), lambda i, h: (i, off + h))
    head_tab = pl.BlockSpec((RET_HEADS, RET_CHUNK, LANES), lambda i, h: (0, 0, 0))
    mask_tab = pl.BlockSpec((RET_HEADS, RET_CHUNK, RET_CHUNK), lambda i, h: (0, 0, 0))
    blocks = (6 * _nbytes((r, HEAD_DIM), F32) + 2 * _nbytes((RET_HEADS, RET_CHUNK, LANES), F32)
              + _nbytes((RET_HEADS, RET_CHUNK, RET_CHUNK), F32) + _nbytes((RET_HEADS, HEAD_DIM, HEAD_DIM), F32)
              + _nbytes((r, HEAD_DIM), BF16))
    return pl.pallas_call(
        functools.partial(_ret_prompt_kernel, nchunk=nchunk),
        out_shape=(jax.ShapeDtypeStruct((m, RET_W), BF16),
                   jax.ShapeDtypeStruct((RET_HEADS, HEAD_DIM, HEAD_DIM), F32)),
        grid=(m // r, RET_HEADS),
        in_specs=[col(COL_RQ), col(COL_RK), col(COL_RV), col(COL_RG),
                  pl.BlockSpec((r, HEAD_DIM), lambda i, h: (i, 0)),
                  pl.BlockSpec((r, HEAD_DIM), lambda i, h: (i, 0)),
                  mask_tab, head_tab, head_tab,
                  pl.BlockSpec((None, 1, HEAD_DIM), lambda i, h: (l, 0, h))],
        out_specs=(pl.BlockSpec((r, HEAD_DIM), lambda i, h: (i, h)),
                   pl.BlockSpec((RET_HEADS, HEAD_DIM, HEAD_DIM), lambda i, h: (0, 0, 0))),
        compiler_params=_params(("arbitrary", "arbitrary"), blocks, 0, 4 * _nbytes((r, HEAD_DIM), F32)),
        name="ret_prompt",
    )(proj, proj, proj, proj, c_tab, s_tab, dmask, xi, zeta, ret_gn)


def _swa_prompt_kernel(sink_ref, q_ref, k_ref, v_ref, kp_ref, vp_ref, o_ref, *, nblk, l):
    kvh = pl.program_id(0)
    i = pl.program_id(1)
    kfull = jnp.concatenate([kp_ref[...], k_ref[...]], axis=0).astype(BF16)
    vfull = jnp.concatenate([vp_ref[...], v_ref[...]], axis=0).astype(BF16)
    row = lax.broadcasted_iota(jnp.int32, (WINDOW, 2 * WINDOW), 0)
    col = lax.broadcasted_iota(jnp.int32, (WINDOW, 2 * WINDOW), 1)
    band = (col >= row) & (col <= row + WINDOW)
    for j in range(nblk):
        rows = slice(j * WINDOW, (j + 1) * WINDOW)
        kk = kfull[j * WINDOW:(j + 2) * WINDOW]
        vv = vfull[j * WINDOW:(j + 2) * WINDOW]
        if j == 0:
            mask = band & (col >= jnp.where(i > 0, 0, WINDOW))
        else:
            mask = band
        qj = q_ref[rows, :].astype(BF16)
        qs = jnp.concatenate([qj[:, h * HEAD_DIM:(h + 1) * HEAD_DIM] for h in range(SWA_GROUP)], axis=0)
        s = _dot_nt(qs, kk) * ATT_SCALE
        ps = []
        for h in range(SWA_GROUP):
            sh = jnp.where(mask, s[h * WINDOW:(h + 1) * WINDOW], -jnp.inf)
            sink = sink_ref[l, kvh * SWA_GROUP + h]
            mx = jnp.maximum(jnp.max(sh, axis=-1, keepdims=True), sink)
            e = jnp.exp(sh - mx)
            den = jnp.sum(e, axis=-1, keepdims=True) + jnp.exp(sink - mx)
            ps.append((e / den).astype(BF16))
        o = _dot(jnp.concatenate(ps, axis=0), vv)
        for h in range(SWA_GROUP):
            o_ref[rows, h * HEAD_DIM:(h + 1) * HEAD_DIM] = o[h * WINDOW:(h + 1) * WINDOW].astype(o_ref.dtype)


def _swa_prompt(proj, sinks, l):
    m = proj.shape[0]
    r = 512
    nblk = r // WINDOW
    gw = SWA_GROUP * HEAD_DIM
    prev = lambda off: pl.BlockSpec(
        (WINDOW, HEAD_DIM), lambda kv, i: (jnp.maximum(i * nblk - 1, 0), off + kv))
    blocks = (_nbytes((r, gw), F32) + 2 * _nbytes((r, HEAD_DIM), F32) + 2 * _nbytes((WINDOW, HEAD_DIM), F32)
              + _nbytes((r, gw), BF16))
    return pl.pallas_call(
        functools.partial(_swa_prompt_kernel, nblk=nblk, l=l),
        out_shape=jax.ShapeDtypeStruct((m, SWA_W), BF16),
        grid=(SWA_KV_HEADS, m // r),
        in_specs=[pl.BlockSpec(memory_space=pltpu.SMEM),
                  pl.BlockSpec((r, gw), lambda kv, i: (i, COL_SQ // SWA_GROUP + kv)),
                  pl.BlockSpec((r, HEAD_DIM), lambda kv, i: (i, COL_SK + kv)),
                  pl.BlockSpec((r, HEAD_DIM), lambda kv, i: (i, COL_SV + kv)),
                  prev(COL_SK), prev(COL_SV)],
        out_specs=pl.BlockSpec((r, gw), lambda kv, i: (i, kv)),
        compiler_params=_params(("parallel", "parallel"), blocks, 0, 8 * _nbytes((gw, 2 * WINDOW), F32)),
        name="swa_prompt",
    )(sinks, proj, proj, proj, proj, proj)


def _mem_kv_kernel(m_ref, g_ref, wk_ref, wv_ref, k_ref, v_ref):
    h = _rms_rows(m_ref[...], g_ref[...]).astype(BF16)
    k_ref[...] = _dot(h, wk_ref[...])
    v_ref[...] = _dot(h, wv_ref[...])


def _mem_kv(mem, g, wk, wv, l):
    full = lambda shape: pl.BlockSpec(shape, lambda i: (0,) * len(shape))
    layer = lambda shape: pl.BlockSpec((None,) + shape, lambda i: (l,) + (0,) * len(shape))
    blocks = (_nbytes((MEM_LEN, D_MODEL), F32) + 2 * _nbytes((D_MODEL, MEM_W), BF16)
              + 2 * _nbytes((MEM_LEN, MEM_W), F32))
    return pl.pallas_call(
        _mem_kv_kernel,
        out_shape=(jax.ShapeDtypeStruct((MEM_LEN, MEM_W), F32),) * 2,
        grid=(1,),
        in_specs=[full((MEM_LEN, D_MODEL)), _gain_spec(l, 1), layer((D_MODEL, MEM_W)), layer((D_MODEL, MEM_W))],
        out_specs=(full((MEM_LEN, MEM_W)),) * 2,
        compiler_params=_params(("arbitrary",), blocks, 0, _nbytes((MEM_LEN, D_MODEL), F32)),
        name="mem_kv",
    )(mem, g, wk, wv)


def _cross_prompt_kernel(q_ref, mk_ref, mv_ref, wo_ref, x_ref, o_ref):
    q = q_ref[...].astype(BF16)
    mk = mk_ref[...].astype(BF16)
    mv = mv_ref[...].astype(BF16)
    outs = []
    for h in range(MEM_HEADS):
        cols = slice(h * HEAD_DIM, (h + 1) * HEAD_DIM)
        s = _dot_nt(q[:, cols], mk[:, cols]) * ATT_SCALE
        e = jnp.exp(s - jnp.max(s, axis=-1, keepdims=True))
        p = e / jnp.sum(e, axis=-1, keepdims=True)
        outs.append(_dot(p.astype(BF16), mv[:, cols]).astype(BF16))
    o_ref[...] = x_ref[...] + _dot(jnp.concatenate(outs, axis=1), wo_ref[...])


def _cross_prompt(q, mk, mv, wo, x, l):
    m = x.shape[0]
    tm = 512
    blocks = (_nbytes((tm, MEM_W), F32) + 2 * _nbytes((MEM_LEN, MEM_W), F32) + _nbytes((MEM_W, D_MODEL), BF16)
              + 2 * _nbytes((tm, D_MODEL), F32))
    return pl.pallas_call(
        _cross_prompt_kernel,
        out_shape=jax.ShapeDtypeStruct((m, D_MODEL), F32),
        grid=(m // tm,),
        in_specs=[pl.BlockSpec((tm, MEM_W), lambda i: (i, 0)),
                  pl.BlockSpec((MEM_LEN, MEM_W), lambda i: (0, 0)),
                  pl.BlockSpec((MEM_LEN, MEM_W), lambda i: (0, 0)),
                  pl.BlockSpec((None, MEM_W, D_MODEL), lambda i: (l, 0, 0)),
                  pl.BlockSpec((tm, D_MODEL), lambda i: (i, 0))],
        out_specs=pl.BlockSpec((tm, D_MODEL), lambda i: (i, 0)),
        compiler_params=_params(("parallel",), blocks, 0, 2 * _nbytes((tm, D_MODEL), F32)),
        name="cross_prompt",
    )(q, mk, mv, wo, x)


def _chain(prev):
    if prev is None:
        return (), ()
    return (prev,), (pl.BlockSpec(memory_space=pl.ANY),)


def _ret_sample_kernel(gam_ref, q_ref, k_ref, v_ref, g_ref, c_ref, s_ref, gn_ref, st_ref, *rest, bt, nt):
    o_ref, nst_ref, qt_ref, kt_ref = rest[-4:]
    h = pl.program_id(0)
    t = pl.program_id(1)

    @pl.when(t == 0)
    def _():
        c_tab = c_ref[...]
        s_tab = s_ref[...]
        qt = _rotary(q_ref[...], c_tab, s_tab).T
        kt = (_rotary(k_ref[...], c_tab, s_tab) * ATT_SCALE).T
        for tt in range(nt):
            shift = (LANES - tt * bt) % LANES
            qt_ref[tt] = pltpu.roll(qt, shift, 1) if shift else qt
            kt_ref[tt] = pltpu.roll(kt, shift, 1) if shift else kt

    gamma = gam_ref[h]
    qt = qt_ref[t]
    kt = kt_ref[t]
    outs = []
    for ib in range(bt):
        new = gamma * st_ref[ib] + kt[:, ib:ib + 1] * v_ref[ib:ib + 1, :]
        nst_ref[ib] = new
        outs.append(jnp.sum(qt[:, ib:ib + 1] * new, axis=0, keepdims=True))
    out = jnp.concatenate(outs, axis=0)
    out = out * lax.rsqrt(jnp.mean(out * out, axis=-1, keepdims=True) + EPS) * gn_ref[...]
    gate = g_ref[...]
    o_ref[...] = gate * _sigmoid(gate) * out


def _ret_sample(proj, state, prev, gammas, c_tab, s_tab, ret_gn, l):
    b = proj.shape[0]
    bt = 32
    nt = b // bt
    col = lambda off: pl.BlockSpec((b, HEAD_DIM), lambda h, t: (0, off + h))
    tile = lambda off: pl.BlockSpec((bt, HEAD_DIM), lambda h, t: (t, off + h))
    row = pl.BlockSpec((1, HEAD_DIM), lambda h, t: (0, 0))
    st_spec = pl.BlockSpec((None, bt, None, HEAD_DIM, HEAD_DIM), lambda h, t: (l, t, h, 0, 0))
    extra, extra_specs = _chain(prev)
    n_in = 9 + len(extra)
    blocks = (2 * _nbytes((b, HEAD_DIM), F32) + 3 * _nbytes((bt, HEAD_DIM), F32)
              + 2 * _nbytes((bt, HEAD_DIM, HEAD_DIM), F32))
    scratch = 2 * _nbytes((nt, HEAD_DIM, LANES), F32)
    return pl.pallas_call(
        functools.partial(_ret_sample_kernel, bt=bt, nt=nt),
        out_shape=(jax.ShapeDtypeStruct((b, RET_W), F32), jax.ShapeDtypeStruct(state.shape, F32)),
        grid=(RET_HEADS, nt),
        in_specs=[pl.BlockSpec(memory_space=pltpu.SMEM),
                  col(COL_RQ), col(COL_RK), tile(COL_RV), tile(COL_RG), row, row,
                  pl.BlockSpec((None, 1, HEAD_DIM), lambda h, t: (l, 0, h)),
                  st_spec, *extra_specs],
        out_specs=(tile(0), st_spec),
        scratch_shapes=[pltpu.VMEM((nt, HEAD_DIM, LANES), F32), pltpu.VMEM((nt, HEAD_DIM, LANES), F32)],
        input_output_aliases=({n_in - 1: 1} if extra else {}),
        compiler_params=_params(("parallel", "arbitrary"), blocks, scratch, 4 * _nbytes((bt, HEAD_DIM, HEAD_DIM), F32)),
        name="ret_sample",
    )(gammas, proj, proj, proj, proj, c_tab, s_tab, ret_gn, state, *extra)


def _swa_sample_kernel(sink_ref, q_ref, kn_ref, vn_ref, ck_ref, cv_ref, *rest, bt):
    o_ref, nk_ref, nv_ref = rest[-3:]
    rows = SWA_KV_HEADS * WINDOW
    sink = sink_ref[...]
    hrow = lax.broadcasted_iota(jnp.int32, (SWA_HEADS, HEAD_DIM), 0)
    first_group = hrow < SWA_GROUP
    srow = lax.broadcasted_iota(jnp.int32, (SWA_HEADS, rows), 0)
    scol = lax.broadcasted_iota(jnp.int32, (SWA_HEADS, rows), 1)
    own = (scol & (SWA_KV_HEADS - 1)) == (srow // SWA_GROUP)

    def body(ib, carry):
        q = q_ref[ib]
        kn = kn_ref[ib]
        vn = vn_ref[ib]
        k_new = jnp.where(first_group, kn[0:1, :], kn[1:2, :])
        v_new = jnp.where(first_group, vn[0:1, :], vn[1:2, :])
        s_new = jnp.sum(q * k_new, axis=-1, keepdims=True) * ATT_SCALE
        ck = ck_ref[ib]
        cv = cv_ref[ib]
        s = jnp.where(own, _dot_nt(q.astype(BF16), ck.astype(BF16)) * ATT_SCALE, -jnp.inf)
        mx = jnp.maximum(jnp.maximum(jnp.max(s, axis=-1, keepdims=True), s_new), sink)
        e = jnp.exp(s - mx)
        e_new = jnp.exp(s_new - mx)
        den = jnp.sum(e, axis=-1, keepdims=True) + e_new + jnp.exp(sink - mx)
        o_ref[ib] = _dot((e / den).astype(BF16), cv.astype(BF16)) + (e_new / den) * v_new
        nk_ref[ib, 0:rows - SWA_KV_HEADS, :] = ck[SWA_KV_HEADS:rows, :]
        nv_ref[ib, 0:rows - SWA_KV_HEADS, :] = cv[SWA_KV_HEADS:rows, :]
        nk_ref[ib, rows - SWA_KV_HEADS:rows, :] = kn
        nv_ref[ib, rows - SWA_KV_HEADS:rows, :] = vn
        return carry

    lax.fori_loop(0, bt, body, 0, unroll=8)


def _swa_sample(q, k_new, v_new, cache_k, cache_v, prev_k, prev_v, sinks, l):
    b = q.shape[0]
    bt = 16
    rows = SWA_KV_HEADS * WINDOW
    blocks = (2 * _nbytes((bt, SWA_HEADS, HEAD_DIM), F32) + 2 * _nbytes((bt, 8, HEAD_DIM), F32)
              + 4 * _nbytes((bt, rows, HEAD_DIM), F32))
    cache_spec = pl.BlockSpec((None, bt, rows, HEAD_DIM), lambda t: (l, t, 0, 0))
    new_spec = pl.BlockSpec((bt, SWA_KV_HEADS, HEAD_DIM), lambda t: (t, 0, 0))
    head_spec = pl.BlockSpec((bt, SWA_HEADS, HEAD_DIM), lambda t: (t, 0, 0))
    extra_k, spec_k = _chain(prev_k)
    extra_v, spec_v = _chain(prev_v)
    aliases = {6: 1, 7: 2} if extra_k else {}
    return pl.pallas_call(
        functools.partial(_swa_sample_kernel, bt=bt),
        out_shape=(jax.ShapeDtypeStruct((b, SWA_HEADS, HEAD_DIM), F32),
                   jax.ShapeDtypeStruct(cache_k.shape, F32), jax.ShapeDtypeStruct(cache_v.shape, F32)),
        grid=(b // bt,),
        in_specs=[pl.BlockSpec((None, SWA_HEADS, 1), lambda t: (l, 0, 0)),
                  head_spec, new_spec, new_spec, cache_spec, cache_spec, *spec_k, *spec_v],
        out_specs=(head_spec, cache_spec, cache_spec),
        input_output_aliases=aliases,
        compiler_params=_params(("parallel",), blocks),
        name="swa_sample",
    )(sinks.reshape(DEPTH, SWA_HEADS, 1), q, k_new, v_new, cache_k, cache_v, *extra_k, *extra_v)


def _cross_sample_kernel(q_ref, mk_ref, mv_ref, o_ref, *, bt):
    rows = MEM_HEADS * MEM_LEN
    srow = lax.broadcasted_iota(jnp.int32, (MEM_HEADS, rows), 0)
    scol = lax.broadcasted_iota(jnp.int32, (MEM_HEADS, rows), 1)
    own = (scol & (MEM_HEADS - 1)) == srow

    def body(ib, carry):
        qb = q_ref[ib].astype(BF16)
        s = jnp.where(own, _dot_nt(qb, mk_ref[ib].astype(BF16)) * ATT_SCALE, -jnp.inf)
        e = jnp.exp(s - jnp.max(s, axis=-1, keepdims=True))
        p = (e / jnp.sum(e, axis=-1, keepdims=True)).astype(BF16)
        o_ref[ib] = _dot(p, mv_ref[ib].astype(BF16))
        return carry

    lax.fori_loop(0, bt, body, 0, unroll=8)


def _cross_sample(q, cache_k, cache_v, l):
    b = q.shape[0]
    bt = 16
    rows = MEM_HEADS * MEM_LEN
    blocks = 2 * _nbytes((bt, 8, HEAD_DIM), F32) + 2 * _nbytes((bt, rows, HEAD_DIM), F32)
    cache_spec = pl.BlockSpec((None, bt, rows, HEAD_DIM), lambda t: (l, t, 0, 0))
    head_spec = pl.BlockSpec((bt, MEM_HEADS, HEAD_DIM), lambda t: (t, 0, 0))
    return pl.pallas_call(
        functools.partial(_cross_sample_kernel, bt=bt),
        out_shape=jax.ShapeDtypeStruct((b, MEM_HEADS, HEAD_DIM), F32),
        grid=(b // bt,),
        in_specs=[head_spec, cache_spec, cache_spec],
        out_specs=head_spec,
        compiler_params=_params(("parallel",), blocks),
        name="cross_sample",
    )(q, cache_k, cache_v)


def kernel(x_prompt, x_sample, mem_prompt, state_ret, cache_win_k, cache_win_v, cache_mem_k, cache_mem_v,
           g_mix, w_in, ret_gn, sinks, w_out, g_cross, g_mem, wq_c, wk_c, wv_c, wo_c,
           g_ffn, w_gate, w_up, w_down, g_final):
    seq = x_prompt.shape[1]
    nb = x_sample.shape[0]
    w_in_b, w_out_b = _to_bf16_tiles(w_in, IN_PROJ_TN), _to_bf16(w_out)
    wq_b, wk_b, wv_b, wo_b = (_to_bf16(w) for w in (wq_c, wk_c, wv_c, wo_c))
    wg_b, wu_b, wd_b = _to_bf16_tiles(w_gate, FFN_TF), _to_bf16_tiles(w_up, FFN_TF), _to_bf16(w_down)
    gain = lambda g: g.reshape(DEPTH, 1, g.shape[-1])
    g_mix, g_cross, g_mem, g_ffn, ret_gn = (gain(g) for g in (g_mix, g_cross, g_mem, g_ffn, ret_gn))

    cp, sp = (jnp.asarray(t) for t in _rope_tables(np.arange(seq)))
    cs, ss = (jnp.asarray(t) for t in _rope_tables(np.array([PAST_LEN])))
    dmask, xi, zeta = (jnp.asarray(t) for t in _decay_tables())
    gammas = jnp.asarray((1.0 - np.exp2(-5.0 - np.arange(RET_HEADS))).astype(np.float32))

    xp = x_prompt[0]
    xs = x_sample[:, 0]
    mem = mem_prompt[0]
    kvw = SWA_KV_HEADS * HEAD_DIM
    win_k = cache_win_k.reshape(DEPTH, nb, WINDOW * SWA_KV_HEADS, HEAD_DIM)
    win_v = cache_win_v.reshape(DEPTH, nb, WINDOW * SWA_KV_HEADS, HEAD_DIM)
    mem_k = cache_mem_k.reshape(DEPTH, nb, MEM_LEN * MEM_HEADS, HEAD_DIM)
    mem_v = cache_mem_v.reshape(DEPTH, nb, MEM_LEN * MEM_HEADS, HEAD_DIM)

    ret_p, wink_p, winv_p, memk_p, memv_p = [], [], [], [], []
    ret_s = wink_s = winv_s = None
    for l in range(DEPTH):
        proj = _in_proj(xp, g_mix, w_in_b, l)
        ro, r_last = _ret_prompt(proj, cp, sp, dmask, xi, zeta, ret_gn, l)
        so = _swa_prompt(proj, sinks, l)
        x1, qc = _out_proj(ro, so, xp, w_out_b, g_cross, wq_b, l)
        mk, mv = _mem_kv(mem, g_mem, wk_b, wv_b, l)
        x2 = _cross_prompt(qc, mk, mv, wo_b, x1, l)
        xp = _ffn(x2, g_ffn, wg_b, wu_b, wd_b, g_final, l)
        ret_p.append(r_last[None])
        tail = proj[seq - WINDOW:]
        wink_p.append(tail[:, COL_SK * LANES:COL_SK * LANES + kvw].reshape(1, WINDOW, SWA_KV_HEADS, HEAD_DIM))
        winv_p.append(tail[:, COL_SV * LANES:COL_SV * LANES + kvw].reshape(1, WINDOW, SWA_KV_HEADS, HEAD_DIM))
        memk_p.append(mk.reshape(1, MEM_LEN, MEM_HEADS, HEAD_DIM))
        memv_p.append(mv.reshape(1, MEM_LEN, MEM_HEADS, HEAD_DIM))

        proj_s = _in_proj(xs, g_mix, w_in_b, l)
        ro_s, ret_s = _ret_sample(proj_s, state_ret, ret_s, gammas, cs, ss, ret_gn, l)
        sq = proj_s[:, COL_SQ * LANES:COL_SQ * LANES + SWA_W].reshape(nb, SWA_HEADS, HEAD_DIM)
        sk = proj_s[:, COL_SK * LANES:COL_SK * LANES + kvw].reshape(nb, SWA_KV_HEADS, HEAD_DIM)
        sv = proj_s[:, COL_SV * LANES:COL_SV * LANES + kvw].reshape(nb, SWA_KV_HEADS, HEAD_DIM)
        so_s, wink_s, winv_s = _swa_sample(sq, sk, sv, win_k, win_v, wink_s, winv_s, sinks, l)
        x1_s, qc_s = _out_proj(ro_s, so_s.reshape(nb, SWA_W), xs, w_out_b, g_cross, wq_b, l)
        oc_s = _cross_sample(qc_s.reshape(nb, MEM_HEADS, HEAD_DIM), mem_k, mem_v, l)
        x2_s = _matmul_residual(oc_s.reshape(nb, MEM_W), wo_b, x1_s, l)
        xs = _ffn(x2_s, g_ffn, wg_b, wu_b, wd_b, g_final, l)

    cache_shape = (DEPTH, nb, WINDOW, SWA_KV_HEADS, HEAD_DIM)
    return (xp[None], xs[:, None],
            jnp.stack(ret_p), jnp.stack(wink_p), jnp.stack(winv_p), jnp.stack(memk_p), jnp.stack(memv_p),
            ret_s, wink_s.reshape(cache_shape), winv_s.reshape(cache_shape))
```

```python
import functools

import numpy as np
import jax
import jax.numpy as jnp
from jax import lax
from jax.experimental import pallas as pl
from jax.experimental.pallas import tpu as pltpu

D_MODEL = 2048
DEPTH = 2
PAST_LEN = 8192
HEAD_DIM = 128
RET_HEADS = 8
RET_W = RET_HEADS * HEAD_DIM
RET_CHUNK = 256
SWA_HEADS = 8
SWA_KV_HEADS = 2
SWA_GROUP = SWA_HEADS // SWA_KV_HEADS
SWA_W = SWA_HEADS * HEAD_DIM
WINDOW = 128
MEM_LEN = 256
MEM_HEADS = 4
MEM_W = MEM_HEADS * HEAD_DIM
IN_W = 4 * RET_W + SWA_W + 2 * SWA_KV_HEADS * HEAD_DIM
D_FF = 5632
ROPE_BASE = 10000.0
EPS = 1e-6
ATT_SCALE = HEAD_DIM ** -0.5

COL_RQ, COL_RK, COL_RV, COL_RG = 0, 8, 16, 24
COL_SQ, COL_SK, COL_SV = 32, 40, 42

V7X_VMEM_BYTES = 64 * 1024 * 1024
V7X_VMEM_BUDGET = V7X_VMEM_BYTES - 8 * 1024 * 1024
LANES = 128
CAST_BLOCK_BYTES = 4 * 1024 * 1024

BF16 = jnp.bfloat16
F32 = jnp.float32


def _vmem_limit(block_bytes, scratch_bytes=0, temp_bytes=0):
    need = 2 * block_bytes + scratch_bytes + temp_bytes + 4 * 1024 * 1024
    return int(min(max(need, 16 * 1024 * 1024), V7X_VMEM_BUDGET))


def _nbytes(shape, dtype):
    return int(np.prod(shape)) * jnp.dtype(dtype).itemsize


def _params(semantics, block_bytes, scratch_bytes=0, temp_bytes=0):
    return pltpu.CompilerParams(
        dimension_semantics=semantics,
        vmem_limit_bytes=_vmem_limit(block_bytes, scratch_bytes, temp_bytes))


def _rms_rows(x, g):
    ms = jnp.mean(x * x, axis=-1, keepdims=True)
    return x * lax.rsqrt(ms + EPS) * g


def _rms_to(dst_ref, src_ref, g_ref, rows):
    g = g_ref[...]
    chunk = min(rows, 128)

    def body(c, carry):
        r = pl.multiple_of(c * chunk, chunk)
        x = src_ref[pl.ds(r, chunk), :]
        dst_ref[pl.ds(r, chunk), :] = _rms_rows(x, g).astype(dst_ref.dtype)
        return carry

    lax.fori_loop(0, rows // chunk, body, 0)


def _sigmoid(x):
    return 1.0 / (1.0 + jnp.exp(-x))


def _dot(a, b):
    return jnp.dot(a, b, preferred_element_type=F32)


def _dot_nt(a, b):
    return lax.dot_general(a, b, (((1,), (1,)), ((), ())), preferred_element_type=F32)


def _gain_spec(l, ngrid):
    zeros = (0,) * 2
    return pl.BlockSpec((None, 1, D_MODEL), lambda *_: (l,) + zeros)


def _cast_kernel(x_ref, o_ref):
    o_ref[...] = x_ref[...].astype(o_ref.dtype)


def _to_bf16(w):
    depth, r, c = w.shape
    tr = r
    while _nbytes((tr, c), F32) > CAST_BLOCK_BYTES and tr % 32 == 0:
        tr //= 2
    spec = pl.BlockSpec((None, tr, c), lambda d, i: (d, i, 0))
    return pl.pallas_call(
        _cast_kernel,
        out_shape=jax.ShapeDtypeStruct(w.shape, BF16),
        grid=(depth, r // tr),
        in_specs=[spec],
        out_specs=spec,
        compiler_params=_params(("parallel", "parallel"), _nbytes((tr, c), F32) + _nbytes((tr, c), BF16)),
        name="to_bf16",
    )(w)


def _cast_tiles_kernel(x_ref, o_ref, *, nt, tn):
    for j in range(nt):
        o_ref[j] = x_ref[:, j * tn:(j + 1) * tn].astype(o_ref.dtype)


def _to_bf16_tiles(w, tn):
    depth, r, c = w.shape
    nt = c // tn
    tr = r
    while _nbytes((tr, c), F32) > CAST_BLOCK_BYTES and tr % 32 == 0:
        tr //= 2
    return pl.pallas_call(
        functools.partial(_cast_tiles_kernel, nt=nt, tn=tn),
        out_shape=jax.ShapeDtypeStruct((depth, nt, r, tn), BF16),
        grid=(depth, r // tr),
        in_specs=[pl.BlockSpec((None, tr, c), lambda d, i: (d, i, 0))],
        out_specs=pl.BlockSpec((None, nt, tr, tn), lambda d, i: (d, 0, i, 0)),
        compiler_params=_params(("parallel", "parallel"), _nbytes((tr, c), F32) + _nbytes((tr, c), BF16)),
        name="to_bf16_tiles",
    )(w)


def _in_proj_kernel(x_ref, g_ref, w_ref, o_ref, h_ref, *, tm):
    @pl.when(pl.program_id(1) == 0)
    def _():
        _rms_to(h_ref, x_ref, g_ref, tm)

    o_ref[...] = _dot(h_ref[...], w_ref[...])


IN_PROJ_TN = 512


def _in_proj(x, g, w, l):
    m = x.shape[0]
    tm = min(m, 1024)
    tn = IN_PROJ_TN
    blocks = (_nbytes((tm, D_MODEL), F32) + _nbytes((D_MODEL, tn), BF16) + _nbytes((tm, tn), F32))
    return pl.pallas_call(
        functools.partial(_in_proj_kernel, tm=tm),
        out_shape=jax.ShapeDtypeStruct((m, IN_W), F32),
        grid=(m // tm, IN_W // tn),
        in_specs=[
            pl.BlockSpec((tm, D_MODEL), lambda i, j: (i, 0)),
            _gain_spec(l, 2),
            pl.BlockSpec((None, None, D_MODEL, tn), lambda i, j: (l, j, 0, 0)),
        ],
        out_specs=pl.BlockSpec((tm, tn), lambda i, j: (i, j)),
        scratch_shapes=[pltpu.VMEM((tm, D_MODEL), BF16)],
        compiler_params=_params(("parallel", "arbitrary"), blocks, _nbytes((tm, D_MODEL), BF16)),
        name="in_proj",
    )(x, g, w)


def _out_proj_kernel(ro_ref, so_ref, x_ref, wa_ref, wb_ref, gc_ref, wq_ref, x1_ref, q_ref, h_ref, *, tm):
    y = _dot(ro_ref[...].astype(BF16), wa_ref[...]) + _dot(so_ref[...].astype(BF16), wb_ref[...])
    x1_ref[...] = x_ref[...] + y
    _rms_to(h_ref, x1_ref, gc_ref, tm)
    q_ref[...] = _dot(h_ref[...], wq_ref[...])


def _out_proj(ro, so, x, w_out, g_cross, wq, l):
    m = x.shape[0]
    tm = min(m, 512)
    blocks = (_nbytes((tm, RET_W), ro.dtype) + _nbytes((tm, SWA_W), so.dtype)
              + 2 * _nbytes((tm, D_MODEL), F32) + _nbytes((D_MODEL, D_MODEL), BF16)
              + _nbytes((D_MODEL, MEM_W), BF16) + _nbytes((tm, MEM_W), F32))
    return pl.pallas_call(
        functools.partial(_out_proj_kernel, tm=tm),
        out_shape=(jax.ShapeDtypeStruct((m, D_MODEL), F32), jax.ShapeDtypeStruct((m, MEM_W), F32)),
        grid=(m // tm,),
        in_specs=[
            pl.BlockSpec((tm, RET_W), lambda i: (i, 0)),
            pl.BlockSpec((tm, SWA_W), lambda i: (i, 0)),
            pl.BlockSpec((tm, D_MODEL), lambda i: (i, 0)),
            pl.BlockSpec((None, RET_W, D_MODEL), lambda i: (l, 0, 0)),
            pl.BlockSpec((None, SWA_W, D_MODEL), lambda i: (l, 1, 0)),
            _gain_spec(l, 1),
            pl.BlockSpec((None, D_MODEL, MEM_W), lambda i: (l, 0, 0)),
        ],
        out_specs=(pl.BlockSpec((tm, D_MODEL), lambda i: (i, 0)),
                   pl.BlockSpec((tm, MEM_W), lambda i: (i, 0))),
        scratch_shapes=[pltpu.VMEM((tm, D_MODEL), BF16)],
        compiler_params=_params(("parallel",), blocks, _nbytes((tm, D_MODEL), BF16),
                                _nbytes((tm, D_MODEL), F32)),
        name="out_proj",
    )(ro, so, x, w_out, w_out, g_cross, wq)


def _ffn_kernel(x_ref, g_ref, wg_ref, wu_ref, wd_ref, gf_ref, o_ref, h_ref, *, tm, nf, final_norm):
    j = pl.program_id(1)

    @pl.when(j == 0)
    def _():
        _rms_to(h_ref, x_ref, g_ref, tm)
        o_ref[...] = x_ref[...]

    h = h_ref[...]
    a = _dot(h, wg_ref[...])
    u = _dot(h, wu_ref[...])
    act = (a * _sigmoid(a) * u).astype(BF16)
    half = D_MODEL // 2
    for c in range(2):
        cols = slice(c * half, (c + 1) * half)
        o_ref[:, cols] += _dot(act, wd_ref[:, cols])

    if final_norm:
        @pl.when(j == nf - 1)
        def _():
            _rms_to(o_ref, o_ref, gf_ref, tm)


FFN_TF = 512


def _ffn(x, g, wg, wu, wd, g_final, l):
    m = x.shape[0]
    tm = min(m, 512)
    tf = FFN_TF
    nf = D_FF // tf
    blocks = 2 * _nbytes((tm, D_MODEL), F32) + 3 * _nbytes((D_MODEL, tf), BF16)
    scratch = _nbytes((tm, D_MODEL), BF16)
    temps = 3 * _nbytes((tm, tf), F32) + _nbytes((tm, D_MODEL // 2), F32)
    return pl.pallas_call(
        functools.partial(_ffn_kernel, tm=tm, nf=nf, final_norm=(l == DEPTH - 1)),
        out_shape=jax.ShapeDtypeStruct((m, D_MODEL), F32),
        grid=(m // tm, nf),
        in_specs=[
            pl.BlockSpec((tm, D_MODEL), lambda i, j: (i, 0)),
            _gain_spec(l, 2),
            pl.BlockSpec((None, None, D_MODEL, tf), lambda i, j: (l, j, 0, 0)),
            pl.BlockSpec((None, None, D_MODEL, tf), lambda i, j: (l, j, 0, 0)),
            pl.BlockSpec((None, tf, D_MODEL), lambda i, j: (l, j, 0)),
            pl.BlockSpec((1, D_MODEL), lambda i, j: (0, 0)),
        ],
        out_specs=pl.BlockSpec((tm, D_MODEL), lambda i, j: (i, 0)),
        scratch_shapes=[pltpu.VMEM((tm, D_MODEL), BF16)],
        compiler_params=_params(("parallel", "arbitrary"), blocks, scratch, temps),
        name="ffn",
    )(x, g, wg, wu, wd, g_final.reshape(1, D_MODEL))


def _matmul_residual_kernel(a_ref, w_ref, x_ref, o_ref):
    o_ref[...] = x_ref[...] + _dot(a_ref[...].astype(BF16), w_ref[...])


def _matmul_residual(a, w, x, l):
    m, k = a.shape
    n = w.shape[2]
    blocks = _nbytes((m, k), a.dtype) + _nbytes((k, n), BF16) + 2 * _nbytes((m, n), F32)
    return pl.pallas_call(
        _matmul_residual_kernel,
        out_shape=jax.ShapeDtypeStruct((m, n), F32),
        grid=(1,),
        in_specs=[pl.BlockSpec((m, k), lambda i: (0, 0)),
                  pl.BlockSpec((None, k, n), lambda i: (l, 0, 0)),
                  pl.BlockSpec((m, n), lambda i: (0, 0))],
        out_specs=pl.BlockSpec((m, n), lambda i: (0, 0)),
        compiler_params=_params(("arbitrary",), blocks),
        name="matmul_residual",
    )(a, w, x)


def _rope_tables(positions):
    half = HEAD_DIM // 2
    inv = ROPE_BASE ** (-np.arange(half, dtype=np.float64) / half)
    ang = np.asarray(positions, np.float64)[:, None] * inv[None, :]
    cos, sin = np.cos(ang), np.sin(ang)
    c = np.concatenate([cos, cos], axis=-1).astype(np.float32)
    s = np.concatenate([-sin, sin], axis=-1).astype(np.float32)
    return c, s


def _decay_tables():
    c = RET_CHUNK
    lg = np.log1p(-np.exp2(-5.0 - np.arange(RET_HEADS, dtype=np.float64)))
    idx = np.arange(c, dtype=np.float64)
    diff = idx[:, None] - idx[None, :]
    dmask = np.where(diff[None] >= 0, np.exp(lg[:, None, None] * np.maximum(diff, 0.0)[None]), 0.0)
    xi = np.exp(lg[:, None] * (idx + 1.0)[None])
    zeta = np.exp(lg[:, None] * (c - 1.0 - idx)[None])
    bcast = lambda t: np.broadcast_to(t[:, :, None], (RET_HEADS, c, LANES)).astype(np.float32)
    return dmask.astype(np.float32), bcast(xi), bcast(zeta)


def _rotary(x, c, s):
    return x * c + pltpu.roll(x, HEAD_DIM // 2, 1) * s


def _ret_prompt_kernel(q_ref, k_ref, v_ref, g_ref, c_ref, s_ref, dm_ref, xi_ref, zt_ref, gn_ref,
                       o_ref, st_ref, *, nchunk):
    h = pl.program_id(1)

    @pl.when(pl.program_id(0) == 0)
    def _():
        st_ref[h] = jnp.zeros((HEAD_DIM, HEAD_DIM), F32)

    c_tab = c_ref[...]
    s_tab = s_ref[...]
    qr = _rotary(q_ref[...], c_tab, s_tab)
    kr = _rotary(k_ref[...], c_tab, s_tab) * ATT_SCALE
    dmask = dm_ref[h]
    xi = xi_ref[h]
    zeta = zt_ref[h]
    g_chunk = xi[RET_CHUNK - 1:RET_CHUNK, :]
    gn = gn_ref[...]
    state = st_ref[h]
    for c in range(nchunk):
        rows = slice(c * RET_CHUNK, (c + 1) * RET_CHUNK)
        qc = qr[rows]
        kc = kr[rows]
        vc = v_ref[rows, :].astype(BF16)
        s = _dot_nt(qc.astype(BF16), kc.astype(BF16)) * dmask
        inner = _dot(s.astype(BF16), vc)
        cross = _dot((qc * xi).astype(BF16), state.astype(BF16))
        kv = _dot((kc * zeta).T.astype(BF16), vc)
        state = g_chunk * state + kv
        out = inner + cross
        out = out * lax.rsqrt(jnp.mean(out * out, axis=-1, keepdims=True) + EPS) * gn
        gate = g_ref[rows, :]
        o_ref[rows, :] = (gate * _sigmoid(gate) * out).astype(o_ref.dtype)
    st_ref[h] = state


def _ret_prompt(proj, c_tab, s_tab, dmask, xi, zeta, ret_gn, l):
    m = proj.shape[0]
    r = 1024
    nchunk = r // RET_CHUNK
    col = lambda off: pl.BlockSpec((r, HEAD_DIM), lambda i, h: (i, off + h))
    head_tab = pl.BlockSpec((RET_HEADS, RET_CHUNK, LANES), lambda i, h: (0, 0, 0))
    mask_tab = pl.BlockSpec((RET_HEADS, RET_CHUNK, RET_CHUNK), lambda i, h: (0, 0, 0))
    blocks = (6 * _nbytes((r, HEAD_DIM), F32) + 2 * _nbytes((RET_HEADS, RET_CHUNK, LANES), F32)
              + _nbytes((RET_HEADS, RET_CHUNK, RET_CHUNK), F32) + _nbytes((RET_HEADS, HEAD_DIM, HEAD_DIM), F32)
              + _nbytes((r, HEAD_DIM), BF16))
    return pl.pallas_call(
        functools.partial(_ret_prompt_kernel, nchunk=nchunk),
        out_shape=(jax.ShapeDtypeStruct((m, RET_W), BF16),
                   jax.ShapeDtypeStruct((RET_HEADS, HEAD_DIM, HEAD_DIM), F32)),
        grid=(m // r, RET_HEADS),
        in_specs=[col(COL_RQ), col(COL_RK), col(COL_RV), col(COL_RG),
                  pl.BlockSpec((r, HEAD_DIM), lambda i, h: (i, 0)),
                  pl.BlockSpec((r, HEAD_DIM), lambda i, h: (i, 0)),
                  mask_tab, head_tab, head_tab,
                  pl.BlockSpec((None, 1, HEAD_DIM), lambda i, h: (l, 0, h))],
        out_specs=(pl.BlockSpec((r, HEAD_DIM), lambda i, h: (i, h)),
                   pl.BlockSpec((RET_HEADS, HEAD_DIM, HEAD_DIM), lambda i, h: (0, 0, 0))),
        compiler_params=_params(("arbitrary", "arbitrary"), blocks, 0, 4 * _nbytes((r, HEAD_DIM), F32)),
        name="ret_prompt",
    )(proj, proj, proj, proj, c_tab, s_tab, dmask, xi, zeta, ret_gn)


def _swa_prompt_kernel(sink_ref, q_ref, k_ref, v_ref, kp_ref, vp_ref, o_ref, *, nblk, l):
    kvh = pl.program_id(0)
    i = pl.program_id(1)
    kfull = jnp.concatenate([kp_ref[...], k_ref[...]], axis=0).astype(BF16)
    vfull = jnp.concatenate([vp_ref[...], v_ref[...]], axis=0).astype(BF16)
    row = lax.broadcasted_iota(jnp.int32, (WINDOW, 2 * WINDOW), 0)
    col = lax.broadcasted_iota(jnp.int32, (WINDOW, 2 * WINDOW), 1)
    band = (col >= row) & (col <= row + WINDOW)
    for j in range(nblk):
        rows = slice(j * WINDOW, (j + 1) * WINDOW)
        kk = kfull[j * WINDOW:(j + 2) * WINDOW]
        vv = vfull[j * WINDOW:(j + 2) * WINDOW]
        if j == 0:
            mask = band & (col >= jnp.where(i > 0, 0, WINDOW))
        else:
            mask = band
        qj = q_ref[rows, :].astype(BF16)
        qs = jnp.concatenate([qj[:, h * HEAD_DIM:(h + 1) * HEAD_DIM] for h in range(SWA_GROUP)], axis=0)
        s = _dot_nt(qs, kk) * ATT_SCALE
        ps = []
        for h in range(SWA_GROUP):
            sh = jnp.where(mask, s[h * WINDOW:(h + 1) * WINDOW], -jnp.inf)
            sink = sink_ref[l, kvh * SWA_GROUP + h]
            mx = jnp.maximum(jnp.max(sh, axis=-1, keepdims=True), sink)
            e = jnp.exp(sh - mx)
            den = jnp.sum(e, axis=-1, keepdims=True) + jnp.exp(sink - mx)
            ps.append((e / den).astype(BF16))
        o = _dot(jnp.concatenate(ps, axis=0), vv)
        for h in range(SWA_GROUP):
            o_ref[rows, h * HEAD_DIM:(h + 1) * HEAD_DIM] = o[h * WINDOW:(h + 1) * WINDOW].astype(o_ref.dtype)


def _swa_prompt(proj, sinks, l):
    m = proj.shape[0]
    r = 512
    nblk = r // WINDOW
    gw = SWA_GROUP * HEAD_DIM
    prev = lambda off: pl.BlockSpec(
        (WINDOW, HEAD_DIM), lambda kv, i: (jnp.maximum(i * nblk - 1, 0), off + kv))
    blocks = (_nbytes((r, gw), F32) + 2 * _nbytes((r, HEAD_DIM), F32) + 2 * _nbytes((WINDOW, HEAD_DIM), F32)
              + _nbytes((r, gw), BF16))
    return pl.pallas_call(
        functools.partial(_swa_prompt_kernel, nblk=nblk, l=l),
        out_shape=jax.ShapeDtypeStruct((m, SWA_W), BF16),
        grid=(SWA_KV_HEADS, m // r),
        in_specs=[pl.BlockSpec(memory_space=pltpu.SMEM),
                  pl.BlockSpec((r, gw), lambda kv, i: (i, COL_SQ // SWA_GROUP + kv)),
                  pl.BlockSpec((r, HEAD_DIM), lambda kv, i: (i, COL_SK + kv)),
                  pl.BlockSpec((r, HEAD_DIM), lambda kv, i: (i, COL_SV + kv)),
                  prev(COL_SK), prev(COL_SV)],
        out_specs=pl.BlockSpec((r, gw), lambda kv, i: (i, kv)),
        compiler_params=_params(("parallel", "parallel"), blocks, 0, 8 * _nbytes((gw, 2 * WINDOW), F32)),
        name="swa_prompt",
    )(sinks, proj, proj, proj, proj, proj)


def _mem_kv_kernel(m_ref, g_ref, wk_ref, wv_ref, k_ref, v_ref):
    h = _rms_rows(m_ref[...], g_ref[...]).astype(BF16)
    k_ref[...] = _dot(h, wk_ref[...])
    v_ref[...] = _dot(h, wv_ref[...])


def _mem_kv(mem, g, wk, wv, l):
    full = lambda shape: pl.BlockSpec(shape, lambda i: (0,) * len(shape))
    layer = lambda shape: pl.BlockSpec((None,) + shape, lambda i: (l,) + (0,) * len(shape))
    blocks = (_nbytes((MEM_LEN, D_MODEL), F32) + 2 * _nbytes((D_MODEL, MEM_W), BF16)
              + 2 * _nbytes((MEM_LEN, MEM_W), F32))
    return pl.pallas_call(
        _mem_kv_kernel,
        out_shape=(jax.ShapeDtypeStruct((MEM_LEN, MEM_W), F32),) * 2,
        grid=(1,),
        in_specs=[full((MEM_LEN, D_MODEL)), _gain_spec(l, 1), layer((D_MODEL, MEM_W)), layer((D_MODEL, MEM_W))],
        out_specs=(full((MEM_LEN, MEM_W)),) * 2,
        compiler_params=_params(("arbitrary",), blocks, 0, _nbytes((MEM_LEN, D_MODEL), F32)),
        name="mem_kv",
    )(mem, g, wk, wv)


def _cross_prompt_kernel(q_ref, mk_ref, mv_ref, wo_ref, x_ref, o_ref):
    q = q_ref[...].astype(BF16)
    mk = mk_ref[...].astype(BF16)
    mv = mv_ref[...].astype(BF16)
    outs = []
    for h in range(MEM_HEADS):
        cols = slice(h * HEAD_DIM, (h + 1) * HEAD_DIM)
        s = _dot_nt(q[:, cols], mk[:, cols]) * ATT_SCALE
        e = jnp.exp(s - jnp.max(s, axis=-1, keepdims=True))
        p = e / jnp.sum(e, axis=-1, keepdims=True)
        outs.append(_dot(p.astype(BF16), mv[:, cols]).astype(BF16))
    o_ref[...] = x_ref[...] + _dot(jnp.concatenate(outs, axis=1), wo_ref[...])


def _cross_prompt(q, mk, mv, wo, x, l):
    m = x.shape[0]
    tm = 512
    blocks = (_nbytes((tm, MEM_W), F32) + 2 * _nbytes((MEM_LEN, MEM_W), F32) + _nbytes((MEM_W, D_MODEL), BF16)
              + 2 * _nbytes((tm, D_MODEL), F32))
    return pl.pallas_call(
        _cross_prompt_kernel,
        out_shape=jax.ShapeDtypeStruct((m, D_MODEL), F32),
        grid=(m // tm,),
        in_specs=[pl.BlockSpec((tm, MEM_W), lambda i: (i, 0)),
                  pl.BlockSpec((MEM_LEN, MEM_W), lambda i: (0, 0)),
                  pl.BlockSpec((MEM_LEN, MEM_W), lambda i: (0, 0)),
                  pl.BlockSpec((None, MEM_W, D_MODEL), lambda i: (l, 0, 0)),
                  pl.BlockSpec((tm, D_MODEL), lambda i: (i, 0))],
        out_specs=pl.BlockSpec((tm, D_MODEL), lambda i: (i, 0)),
        compiler_params=_params(("parallel",), blocks, 0, 2 * _nbytes((tm, D_MODEL), F32)),
        name="cross_prompt",
    )(q, mk, mv, wo, x)


def _chain(prev):
    if prev is None:
        return (), ()
    return (prev,), (pl.BlockSpec(memory_space=pl.ANY),)


def _ret_sample_kernel(gam_ref, q_ref, k_ref, v_ref, g_ref, c_ref, s_ref, gn_ref, st_ref, *rest, bt, nt):
    o_ref, nst_ref, qt_ref, kt_ref = rest[-4:]
    h = pl.program_id(0)
    t = pl.program_id(1)

    @pl.when(t == 0)
    def _():
        c_tab = c_ref[...]
        s_tab = s_ref[...]
        qt = _rotary(q_ref[...], c_tab, s_tab).T
        kt = (_rotary(k_ref[...], c_tab, s_tab) * ATT_SCALE).T
        for tt in range(nt):
            shift = (LANES - tt * bt) % LANES
            qt_ref[tt] = pltpu.roll(qt, shift, 1) if shift else qt
            kt_ref[tt] = pltpu.roll(kt, shift, 1) if shift else kt

    gamma = gam_ref[h]
    qt = qt_ref[t]
    kt = kt_ref[t]
    outs = []
    for ib in range(bt):
        new = gamma * st_ref[ib] + kt[:, ib:ib + 1] * v_ref[ib:ib + 1, :]
        nst_ref[ib] = new
        outs.append(jnp.sum(qt[:, ib:ib + 1] * new, axis=0, keepdims=True))
    out = jnp.concatenate(outs, axis=0)
    out = out * lax.rsqrt(jnp.mean(out * out, axis=-1, keepdims=True) + EPS) * gn_ref[...]
    gate = g_ref[...]
    o_ref[...] = gate * _sigmoid(gate) * out


def _ret_sample(proj, state, prev, gammas, c_tab, s_tab, ret_gn, l):
    b = proj.shape[0]
    bt = 32
    nt = b // bt
    col = lambda off: pl.BlockSpec((b, HEAD_DIM), lambda h, t: (0, off + h))
    tile = lambda off: pl.BlockSpec((bt, HEAD_DIM), lambda h, t: (t, off + h))
    row = pl.BlockSpec((1, HEAD_DIM), lambda h, t: (0, 0))
    st_spec = pl.BlockSpec((None, bt, None, HEAD_DIM, HEAD_DIM), lambda h, t: (l, t, h, 0, 0))
    extra, extra_specs = _chain(prev)
    n_in = 9 + len(extra)
    blocks = (2 * _nbytes((b, HEAD_DIM), F32) + 3 * _nbytes((bt, HEAD_DIM), F32)
              + 2 * _nbytes((bt, HEAD_DIM, HEAD_DIM), F32))
    scratch = 2 * _nbytes((nt, HEAD_DIM, LANES), F32)
    return pl.pallas_call(
        functools.partial(_ret_sample_kernel, bt=bt, nt=nt),
        out_shape=(jax.ShapeDtypeStruct((b, RET_W), F32), jax.ShapeDtypeStruct(state.shape, F32)),
        grid=(RET_HEADS, nt),
        in_specs=[pl.BlockSpec(memory_space=pltpu.SMEM),
                  col(COL_RQ), col(COL_RK), tile(COL_RV), tile(COL_RG), row, row,
                  pl.BlockSpec((None, 1, HEAD_DIM), lambda h, t: (l, 0, h)),
                  st_spec, *extra_specs],
        out_specs=(tile(0), st_spec),
        scratch_shapes=[pltpu.VMEM((nt, HEAD_DIM, LANES), F32), pltpu.VMEM((nt, HEAD_DIM, LANES), F32)],
        input_output_aliases=({n_in - 1: 1} if extra else {}),
        compiler_params=_params(("parallel", "arbitrary"), blocks, scratch, 4 * _nbytes((bt, HEAD_DIM, HEAD_DIM), F32)),
        name="ret_sample",
    )(gammas, proj, proj, proj, proj, c_tab, s_tab, ret_gn, state, *extra)


def _swa_sample_kernel(sink_ref, q_ref, kn_ref, vn_ref, ck_ref, cv_ref, *rest, bt):
    o_ref, nk_ref, nv_ref = rest[-3:]
    rows = SWA_KV_HEADS * WINDOW
    sink = sink_ref[...]
    hrow = lax.broadcasted_iota(jnp.int32, (SWA_HEADS, HEAD_DIM), 0)
    first_group = hrow < SWA_GROUP
    srow = lax.broadcasted_iota(jnp.int32, (SWA_HEADS, rows), 0)
    scol = lax.broadcasted_iota(jnp.int32, (SWA_HEADS, rows), 1)
    own = (scol & (SWA_KV_HEADS - 1)) == (srow // SWA_GROUP)

    def body(ib, carry):
        q = q_ref[ib]
        kn = kn_ref[ib]
        vn = vn_ref[ib]
        k_new = jnp.where(first_group, kn[0:1, :], kn[1:2, :])
        v_new = jnp.where(first_group, vn[0:1, :], vn[1:2, :])
        s_new = jnp.sum(q * k_new, axis=-1, keepdims=True) * ATT_SCALE
        ck = ck_ref[ib]
        cv = cv_ref[ib]
        s = jnp.where(own, _dot_nt(q.astype(BF16), ck.astype(BF16)) * ATT_SCALE, -jnp.inf)
        mx = jnp.maximum(jnp.maximum(jnp.max(s, axis=-1, keepdims=True), s_new), sink)
        e = jnp.exp(s - mx)
        e_new = jnp.exp(s_new - mx)
        den = jnp.sum(e, axis=-1, keepdims=True) + e_new + jnp.exp(sink - mx)
        o_ref[ib] = _dot((e / den).astype(BF16), cv.astype(BF16)) + (e_new / den) * v_new
        nk_ref[ib, 0:rows - SWA_KV_HEADS, :] = ck[SWA_KV_HEADS:rows, :]
        nv_ref[ib, 0:rows - SWA_KV_HEADS, :] = cv[SWA_KV_HEADS:rows, :]
        nk_ref[ib, rows - SWA_KV_HEADS:rows, :] = kn
        nv_ref[ib, rows - SWA_KV_HEADS:rows, :] = vn
        return carry

    lax.fori_loop(0, bt, body, 0, unroll=8)


def _swa_sample(q, k_new, v_new, cache_k, cache_v, prev_k, prev_v, sinks, l):
    b = q.shape[0]
    bt = 16
    rows = SWA_KV_HEADS * WINDOW
    blocks = (2 * _nbytes((bt, SWA_HEADS, HEAD_DIM), F32) + 2 * _nbytes((bt, 8, HEAD_DIM), F32)
              + 4 * _nbytes((bt, rows, HEAD_DIM), F32))
    cache_spec = pl.BlockSpec((None, bt, rows, HEAD_DIM), lambda t: (l, t, 0, 0))
    new_spec = pl.BlockSpec((bt, SWA_KV_HEADS, HEAD_DIM), lambda t: (t, 0, 0))
    head_spec = pl.BlockSpec((bt, SWA_HEADS, HEAD_DIM), lambda t: (t, 0, 0))
    extra_k, spec_k = _chain(prev_k)
    extra_v, spec_v = _chain(prev_v)
    aliases = {6: 1, 7: 2} if extra_k else {}
    return pl.pallas_call(
        functools.partial(_swa_sample_kernel, bt=bt),
        out_shape=(jax.ShapeDtypeStruct((b, SWA_HEADS, HEAD_DIM), F32),
                   jax.ShapeDtypeStruct(cache_k.shape, F32), jax.ShapeDtypeStruct(cache_v.shape, F32)),
        grid=(b // bt,),
        in_specs=[pl.BlockSpec((None, SWA_HEADS, 1), lambda t: (l, 0, 0)),
                  head_spec, new_spec, new_spec, cache_spec, cache_spec, *spec_k, *spec_v],
        out_specs=(head_spec, cache_spec, cache_spec),
        input_output_aliases=aliases,
        compiler_params=_params(("parallel",), blocks),
        name="swa_sample",
    )(sinks.reshape(DEPTH, SWA_HEADS, 1), q, k_new, v_new, cache_k, cache_v, *extra_k, *extra_v)


def _cross_sample_kernel(q_ref, mk_ref, mv_ref, o_ref, *, bt):
    rows = MEM_HEADS * MEM_LEN
    srow = lax.broadcasted_iota(jnp.int32, (MEM_HEADS, rows), 0)
    scol = lax.broadcasted_iota(jnp.int32, (MEM_HEADS, rows), 1)
    own = (scol & (MEM_HEADS - 1)) == srow

    def body(ib, carry):
        qb = q_ref[ib].astype(BF16)
        s = jnp.where(own, _dot_nt(qb, mk_ref[ib].astype(BF16)) * ATT_SCALE, -jnp.inf)
        e = jnp.exp(s - jnp.max(s, axis=-1, keepdims=True))
        p = (e / jnp.sum(e, axis=-1, keepdims=True)).astype(BF16)
        o_ref[ib] = _dot(p, mv_ref[ib].astype(BF16))
        return carry

    lax.fori_loop(0, bt, body, 0, unroll=8)


def _cross_sample(q, cache_k, cache_v, l):
    b = q.shape[0]
    bt = 16
    rows = MEM_HEADS * MEM_LEN
    blocks = 2 * _nbytes((bt, 8, HEAD_DIM), F32) + 2 * _nbytes((bt, rows, HEAD_DIM), F32)
    cache_spec = pl.BlockSpec((None, bt, rows, HEAD_DIM), lambda t: (l, t, 0, 0))
    head_spec = pl.BlockSpec((bt, MEM_HEADS, HEAD_DIM), lambda t: (t, 0, 0))
    return pl.pallas_call(
        functools.partial(_cross_sample_kernel, bt=bt),
        out_shape=jax.ShapeDtypeStruct((b, MEM_HEADS, HEAD_DIM), F32),
        grid=(b // bt,),
        in_specs=[head_spec, cache_spec, cache_spec],
        out_specs=head_spec,
        compiler_params=_params(("parallel",), blocks),
        name="cross_sample",
    )(q, cache_k, cache_v)


def kernel(x_prompt, x_sample, mem_prompt, state_ret, cache_win_k, cache_win_v, cache_mem_k, cache_mem_v,
           g_mix, w_in, ret_gn, sinks, w_out, g_cross, g_mem, wq_c, wk_c, wv_c, wo_c,
           g_ffn, w_gate, w_up, w_down, g_final):
    seq = x_prompt.shape[1]
    nb = x_sample.shape[0]
    w_in_b, w_out_b = _to_bf16_tiles(w_in, IN_PROJ_TN), _to_bf16(w_out)
    wq_b, wk_b, wv_b, wo_b = (_to_bf16(w) for w in (wq_c, wk_c, wv_c, wo_c))
    wg_b, wu_b, wd_b = _to_bf16_tiles(w_gate, FFN_TF), _to_bf16_tiles(w_up, FFN_TF), _to_bf16(w_down)
    gain = lambda g: g.reshape(DEPTH, 1, g.shape[-1])
    g_mix, g_cross, g_mem, g_ffn, ret_gn = (gain(g) for g in (g_mix, g_cross, g_mem, g_ffn, ret_gn))

    cp, sp = (jnp.asarray(t) for t in _rope_tables(np.arange(seq)))
    cs, ss = (jnp.asarray(t) for t in _rope_tables(np.array([PAST_LEN])))
    dmask, xi, zeta = (jnp.asarray(t) for t in _decay_tables())
    gammas = jnp.asarray((1.0 - np.exp2(-5.0 - np.arange(RET_HEADS))).astype(np.float32))

    xp = x_prompt[0]
    xs = x_sample[:, 0]
    mem = mem_prompt[0]
    kvw = SWA_KV_HEADS * HEAD_DIM
    win_k = cache_win_k.reshape(DEPTH, nb, WINDOW * SWA_KV_HEADS, HEAD_DIM)
    win_v = cache_win_v.reshape(DEPTH, nb, WINDOW * SWA_KV_HEADS, HEAD_DIM)
    mem_k = cache_mem_k.reshape(DEPTH, nb, MEM_LEN * MEM_HEADS, HEAD_DIM)
    mem_v = cache_mem_v.reshape(DEPTH, nb, MEM_LEN * MEM_HEADS, HEAD_DIM)

    ret_p, wink_p, winv_p, memk_p, memv_p = [], [], [], [], []
    ret_s = wink_s = winv_s = None
    for l in range(DEPTH):
        proj = _in_proj(xp, g_mix, w_in_b, l)
        ro, r_last = _ret_prompt(proj, cp, sp, dmask, xi, zeta, ret_gn, l)
        so = _swa_prompt(proj, sinks, l)
        x1, qc = _out_proj(ro, so, xp, w_out_b, g_cross, wq_b, l)
        mk, mv = _mem_kv(mem, g_mem, wk_b, wv_b, l)
        x2 = _cross_prompt(qc, mk, mv, wo_b, x1, l)
        xp = _ffn(x2, g_ffn, wg_b, wu_b, wd_b, g_final, l)
        ret_p.append(r_last[None])
        tail = proj[seq - WINDOW:]
        wink_p.append(tail[:, COL_SK * LANES:COL_SK * LANES + kvw].reshape(1, WINDOW, SWA_KV_HEADS, HEAD_DIM))
        winv_p.append(tail[:, COL_SV * LANES:COL_SV * LANES + kvw].reshape(1, WINDOW, SWA_KV_HEADS, HEAD_DIM))
        memk_p.append(mk.reshape(1, MEM_LEN, MEM_HEADS, HEAD_DIM))
        memv_p.append(mv.reshape(1, MEM_LEN, MEM_HEADS, HEAD_DIM))

        proj_s = _in_proj(xs, g_mix, w_in_b, l)
        ro_s, ret_s = _ret_sample(proj_s, state_ret, ret_s, gammas, cs, ss, ret_gn, l)
        sq = proj_s[:, COL_SQ * LANES:COL_SQ * LANES + SWA_W].reshape(nb, SWA_HEADS, HEAD_DIM)
        sk = proj_s[:, COL_SK * LANES:COL_SK * LANES + kvw].reshape(nb, SWA_KV_HEADS, HEAD_DIM)
        sv = proj_s[:, COL_SV * LANES:COL_SV * LANES + kvw].reshape(nb, SWA_KV_HEADS, HEAD_DIM)
        so_s, wink_s, winv_s = _swa_sample(sq, sk, sv, win_k, win_v, wink_s, winv_s, sinks, l)
        x1_s, qc_s = _out_proj(ro_s, so_s.reshape(nb, SWA_W), xs, w_out_b, g_cross, wq_b, l)
        oc_s = _cross_sample(qc_s.reshape(nb, MEM_HEADS, HEAD_DIM), mem_k, mem_v, l)
        x2_s = _matmul_residual(oc_s.reshape(nb, MEM_W), wo_b, x1_s, l)
        xs = _ffn(x2_s, g_ffn, wg_b, wu_b, wd_b, g_final, l)

    cache_shape = (DEPTH, nb, WINDOW, SWA_KV_HEADS, HEAD_DIM)
    return (xp[None], xs[:, None],
            jnp.stack(ret_p), jnp.stack(wink_p), jnp.stack(winv_p), jnp.stack(memk_p), jnp.stack(memv_p),
            ret_s, wink_s.reshape(cache_shape), winv_s.reshape(cache_shape))
```

```python
import functools

import numpy as np
import jax
import jax.numpy as jnp
from jax import lax
from jax.experimental import pallas as pl
from jax.experimental.pallas import tpu as pltpu

D_MODEL = 2048
DEPTH = 2
PAST_LEN = 8192
HEAD_DIM = 128
RET_HEADS = 8
RET_W = RET_HEADS * HEAD_DIM
RET_CHUNK = 256
SWA_HEADS = 8
SWA_KV_HEADS = 2
SWA_GROUP = SWA_HEADS // SWA_KV_HEADS
SWA_W = SWA_HEADS * HEAD_DIM
WINDOW = 128
MEM_LEN = 256
MEM_HEADS = 4
MEM_W = MEM_HEADS * HEAD_DIM
IN_W = 4 * RET_W + SWA_W + 2 * SWA_KV_HEADS * HEAD_DIM
D_FF = 5632
ROPE_BASE = 10000.0
EPS = 1e-6
ATT_SCALE = HEAD_DIM ** -0.5

COL_RQ, COL_RK, COL_RV, COL_RG = 0, 8, 16, 24
COL_SQ, COL_SK, COL_SV = 32, 40, 42

V7X_VMEM_BYTES = 64 * 1024 * 1024
V7X_VMEM_BUDGET = V7X_VMEM_BYTES - 8 * 1024 * 1024
LANES = 128
CAST_BLOCK_BYTES = 4 * 1024 * 1024

BF16 = jnp.bfloat16
F32 = jnp.float32


def _vmem_limit(block_bytes, scratch_bytes=0, temp_bytes=0):
    need = 2 * block_bytes + scratch_bytes + temp_bytes + 4 * 1024 * 1024
    return int(min(max(need, 16 * 1024 * 1024), V7X_VMEM_BUDGET))


def _nbytes(shape, dtype):
    return int(np.prod(shape)) * jnp.dtype(dtype).itemsize


def _params(semantics, block_bytes, scratch_bytes=0, temp_bytes=0):
    return pltpu.CompilerParams(
        dimension_semantics=semantics,
        vmem_limit_bytes=_vmem_limit(block_bytes, scratch_bytes, temp_bytes))


def _rms_rows(x, g):
    ms = jnp.mean(x * x, axis=-1, keepdims=True)
    return x * lax.rsqrt(ms + EPS) * g


def _rms_to(dst_ref, src_ref, g_ref, rows):
    g = g_ref[...]
    chunk = min(rows, 128)

    def body(c, carry):
        r = pl.multiple_of(c * chunk, chunk)
        x = src_ref[pl.ds(r, chunk), :]
        dst_ref[pl.ds(r, chunk), :] = _rms_rows(x, g).astype(dst_ref.dtype)
        return carry

    lax.fori_loop(0, rows // chunk, body, 0)


def _sigmoid(x):
    return 1.0 / (1.0 + jnp.exp(-x))


def _dot(a, b):
    return jnp.dot(a, b, preferred_element_type=F32)


def _dot_nt(a, b):
    return lax.dot_general(a, b, (((1,), (1,)), ((), ())), preferred_element_type=F32)


def _gain_spec(l, ngrid):
    zeros = (0,) * 2
    return pl.BlockSpec((None, 1, D_MODEL), lambda *_: (l,) + zeros)


def _cast_kernel(x_ref, o_ref):
    o_ref[...] = x_ref[...].astype(o_ref.dtype)


def _to_bf16(w):
    depth, r, c = w.shape
    tr = r
    while _nbytes((tr, c), F32) > CAST_BLOCK_BYTES and tr % 32 == 0:
        tr //= 2
    spec = pl.BlockSpec((None, tr, c), lambda d, i: (d, i, 0))
    return pl.pallas_call(
        _cast_kernel,
        out_shape=jax.ShapeDtypeStruct(w.shape, BF16),
        grid=(depth, r // tr),
        in_specs=[spec],
        out_specs=spec,
        compiler_params=_params(("parallel", "parallel"), _nbytes((tr, c), F32) + _nbytes((tr, c), BF16)),
        name="to_bf16",
    )(w)


def _cast_tiles_kernel(x_ref, o_ref, *, nt, tn):
    for j in range(nt):
        o_ref[j] = x_ref[:, j * tn:(j + 1) * tn].astype(o_ref.dtype)


def _to_bf16_tiles(w, tn):
    depth, r, c = w.shape
    nt = c // tn
    tr = r
    while _nbytes((tr, c), F32) > CAST_BLOCK_BYTES and tr % 32 == 0:
        tr //= 2
    return pl.pallas_call(
        functools.partial(_cast_tiles_kernel, nt=nt, tn=tn),
        out_shape=jax.ShapeDtypeStruct((depth, nt, r, tn), BF16),
        grid=(depth, r // tr),
        in_specs=[pl.BlockSpec((None, tr, c), lambda d, i: (d, i, 0))],
        out_specs=pl.BlockSpec((None, nt, tr, tn), lambda d, i: (d, 0, i, 0)),
        compiler_params=_params(("parallel", "parallel"), _nbytes((tr, c), F32) + _nbytes((tr, c), BF16)),
        name="to_bf16_tiles",
    )(w)


def _in_proj_kernel(x_ref, g_ref, w_ref, o_ref, h_ref, *, tm):
    @pl.when(pl.program_id(1) == 0)
    def _():
        _rms_to(h_ref, x_ref, g_ref, tm)

    o_ref[...] = _dot(h_ref[...], w_ref[...])


IN_PROJ_TN = 512


def _in_proj(x, g, w, l):
    m = x.shape[0]
    tm = min(m, 1024)
    tn = IN_PROJ_TN
    blocks = (_nbytes((tm, D_MODEL), F32) + _nbytes((D_MODEL, tn), BF16) + _nbytes((tm, tn), F32))
    return pl.pallas_call(
        functools.partial(_in_proj_kernel, tm=tm),
        out_shape=jax.ShapeDtypeStruct((m, IN_W), F32),
        grid=(m // tm, IN_W // tn),
        in_specs=[
            pl.BlockSpec((tm, D_MODEL), lambda i, j: (i, 0)),
            _gain_spec(l, 2),
            pl.BlockSpec((None, None, D_MODEL, tn), lambda i, j: (l, j, 0, 0)),
        ],
        out_specs=pl.BlockSpec((tm, tn), lambda i, j: (i, j)),
        scratch_shapes=[pltpu.VMEM((tm, D_MODEL), BF16)],
        compiler_params=_params(("parallel", "arbitrary"), blocks, _nbytes((tm, D_MODEL), BF16)),
        name="in_proj",
    )(x, g, w)


def _out_proj_kernel(ro_ref, so_ref, x_ref, wa_ref, wb_ref, gc_ref, wq_ref, x1_ref, q_ref, h_ref, *, tm):
    y = _dot(ro_ref[...].astype(BF16), wa_ref[...]) + _dot(so_ref[...].astype(BF16), wb_ref[...])
    x1_ref[...] = x_ref[...] + y
    _rms_to(h_ref, x1_ref, gc_ref, tm)
    q_ref[...] = _dot(h_ref[...], wq_ref[...])


def _out_proj(ro, so, x, w_out, g_cross, wq, l):
    m = x.shape[0]
    tm = min(m, 512)
    blocks = (_nbytes((tm, RET_W), ro.dtype) + _nbytes((tm, SWA_W), so.dtype)
              + 2 * _nbytes((tm, D_MODEL), F32) + _nbytes((D_MODEL, D_MODEL), BF16)
              + _nbytes((D_MODEL, MEM_W), BF16) + _nbytes((tm, MEM_W), F32))
    return pl.pallas_call(
        functools.partial(_out_proj_kernel, tm=tm),
        out_shape=(jax.ShapeDtypeStruct((m, D_MODEL), F32), jax.ShapeDtypeStruct((m, MEM_W), F32)),
        grid=(m // tm,),
        in_specs=[
            pl.BlockSpec((tm, RET_W), lambda i: (i, 0)),
            pl.BlockSpec((tm, SWA_W), lambda i: (i, 0)),
            pl.BlockSpec((tm, D_MODEL), lambda i: (i, 0)),
            pl.BlockSpec((None, RET_W, D_MODEL), lambda i: (l, 0, 0)),
            pl.BlockSpec((None, SWA_W, D_MODEL), lambda i: (l, 1, 0)),
            _gain_spec(l, 1),
            pl.BlockSpec((None, D_MODEL, MEM_W), lambda i: (l, 0, 0)),
        ],
        out_specs=(pl.BlockSpec((tm, D_MODEL), lambda i: (i, 0)),
                   pl.BlockSpec((tm, MEM_W), lambda i: (i, 0))),
        scratch_shapes=[pltpu.VMEM((tm, D_MODEL), BF16)],
        compiler_params=_params(("parallel",), blocks, _nbytes((tm, D_MODEL), BF16),
                                _nbytes((tm, D_MODEL), F32)),
        name="out_proj",
    )(ro, so, x, w_out, w_out, g_cross, wq)


def _ffn_kernel(x_ref, g_ref, wg_ref, wu_ref, wd_ref, gf_ref, o_ref, h_ref, *, tm, nf, final_norm):
    j = pl.program_id(1)

    @pl.when(j == 0)
    def _():
        _rms_to(h_ref, x_ref, g_ref, tm)
        o_ref[...] = x_ref[...]

    h = h_ref[...]
    a = _dot(h, wg_ref[...])
    u = _dot(h, wu_ref[...])
    act = (a * _sigmoid(a) * u).astype(BF16)
    half = D_MODEL // 2
    for c in range(2):
        cols = slice(c * half, (c + 1) * half)
        o_ref[:, cols] += _dot(act, wd_ref[:, cols])

    if final_norm:
        @pl.when(j == nf - 1)
        def _():
            _rms_to(o_ref, o_ref, gf_ref, tm)


FFN_TF = 512


def _ffn(x, g, wg, wu, wd, g_final, l):
    m = x.shape[0]
    tm = min(m, 512)
    tf = FFN_TF
    nf = D_FF // tf
    blocks = 2 * _nbytes((tm, D_MODEL), F32) + 3 * _nbytes((D_MODEL, tf), BF16)
    scratch = _nbytes((tm, D_MODEL), BF16)
    temps = 3 * _nbytes((tm, tf), F32) + _nbytes((tm, D_MODEL // 2), F32)
    return pl.pallas_call(
        functools.partial(_ffn_kernel, tm=tm, nf=nf, final_norm=(l == DEPTH - 1)),
        out_shape=jax.ShapeDtypeStruct((m, D_MODEL), F32),
        grid=(m // tm, nf),
        in_specs=[
            pl.BlockSpec((tm, D_MODEL), lambda i, j: (i, 0)),
            _gain_spec(l, 2),
            pl.BlockSpec((None, None, D_MODEL, tf), lambda i, j: (l, j, 0, 0)),
            pl.BlockSpec((None, None, D_MODEL, tf), lambda i, j: (l, j, 0, 0)),
            pl.BlockSpec((None, tf, D_MODEL), lambda i, j: (l, j, 0)),
            pl.BlockSpec((1, D_MODEL), lambda i, j: (0, 0)),
        ],
        out_specs=pl.BlockSpec((tm, D_MODEL), lambda i, j: (i, 0)),
        scratch_shapes=[pltpu.VMEM((tm, D_MODEL), BF16)],
        compiler_params=_params(("parallel", "arbitrary"), blocks, scratch, temps),
        name="ffn",
    )(x, g, wg, wu, wd, g_final.reshape(1, D_MODEL))


def _matmul_residual_kernel(a_ref, w_ref, x_ref, o_ref):
    o_ref[...] = x_ref[...] + _dot(a_ref[...].astype(BF16), w_ref[...])


def _matmul_residual(a, w, x, l):
    m, k = a.shape
    n = w.shape[2]
    blocks = _nbytes((m, k), a.dtype) + _nbytes((k, n), BF16) + 2 * _nbytes((m, n), F32)
    return pl.pallas_call(
        _matmul_residual_kernel,
        out_shape=jax.ShapeDtypeStruct((m, n), F32),
        grid=(1,),
        in_specs=[pl.BlockSpec((m, k), lambda i: (0, 0)),
                  pl.BlockSpec((None, k, n), lambda i: (l, 0, 0)),
                  pl.BlockSpec((m, n), lambda i: (0, 0))],
        out_specs=pl.BlockSpec((m, n), lambda i: (0, 0)),
        compiler_params=_params(("arbitrary",), blocks),
        name="matmul_residual",
    )(a, w, x)


def _rope_tables(positions):
    half = HEAD_DIM // 2
    inv = ROPE_BASE ** (-np.arange(half, dtype=np.float64) / half)
    ang = np.asarray(positions, np.float64)[:, None] * inv[None, :]
    cos, sin = np.cos(ang), np.sin(ang)
    c = np.concatenate([cos, cos], axis=-1).astype(np.float32)
    s = np.concatenate([-sin, sin], axis=-1).astype(np.float32)
    return c, s


def _decay_tables():
    c = RET_CHUNK
    lg = np.log1p(-np.exp2(-5.0 - np.arange(RET_HEADS, dtype=np.float64)))
    idx = np.arange(c, dtype=np.float64)
    diff = idx[:, None] - idx[None, :]
    dmask = np.where(diff[None] >= 0, np.exp(lg[:, None, None] * np.maximum(diff, 0.0)[None]), 0.0)
    xi = np.exp(lg[:, None] * (idx + 1.0)[None])
    zeta = np.exp(lg[:, None] * (c - 1.0 - idx)[None])
    bcast = lambda t: np.broadcast_to(t[:, :, None], (RET_HEADS, c, LANES)).astype(np.float32)
    return dmask.astype(np.float32), bcast(xi), bcast(zeta)


def _rotary(x, c, s):
    return x * c + pltpu.roll(x, HEAD_DIM // 2, 1) * s


def _ret_prompt_kernel(q_ref, k_ref, v_ref, g_ref, c_ref, s_ref, dm_ref, xi_ref, zt_ref, gn_ref,
                       o_ref, st_ref, *, nchunk):
    h = pl.program_id(1)

    @pl.when(pl.program_id(0) == 0)
    def _():
        st_ref[h] = jnp.zeros((HEAD_DIM, HEAD_DIM), F32)

    c_tab = c_ref[...]
    s_tab = s_ref[...]
    qr = _rotary(q_ref[...], c_tab, s_tab)
    kr = _rotary(k_ref[...], c_tab, s_tab) * ATT_SCALE
    dmask = dm_ref[h]
    xi = xi_ref[h]
    zeta = zt_ref[h]
    g_chunk = xi[RET_CHUNK - 1:RET_CHUNK, :]
    gn = gn_ref[...]
    state = st_ref[h]
    for c in range(nchunk):
        rows = slice(c * RET_CHUNK, (c + 1) * RET_CHUNK)
        qc = qr[rows]
        kc = kr[rows]
        vc = v_ref[rows, :].astype(BF16)
        s = _dot_nt(qc.astype(BF16), kc.astype(BF16)) * dmask
        inner = _dot(s.astype(BF16), vc)
        cross = _dot((qc * xi).astype(BF16), state.astype(BF16))
        kv = _dot((kc * zeta).T.astype(BF16), vc)
        state = g_chunk * state + kv
        out = inner + cross
        out = out * lax.rsqrt(jnp.mean(out * out, axis=-1, keepdims=True) + EPS) * gn
        gate = g_ref[rows, :]
        o_ref[rows, :] = (gate * _sigmoid(gate) * out).astype(o_ref.dtype)
    st_ref[h] = state


def _ret_prompt(proj, c_tab, s_tab, dmask, xi, zeta, ret_gn, l):
    m = proj.shape[0]
    r = 2048
    nchunk = r // RET_CHUNK
    col = lambda off: pl.BlockSpec((r, HEAD_DIM), lambda i, h: (i, off + h))
    head_tab = pl.BlockSpec((RET_HEADS, RET_CHUNK, LANES), lambda i, h: (0, 0, 0))
    mask_tab = pl.BlockSpec((RET_HEADS, RET_CHUNK, RET_CHUNK), lambda i, h: (0, 0, 0))
    blocks = (6 * _nbytes((r, HEAD_DIM), F32) + 2 * _nbytes((RET_HEADS, RET_CHUNK, LANES), F32)
              + _nbytes((RET_HEADS, RET_CHUNK, RET_CHUNK), F32) + _nbytes((RET_HEADS, HEAD_DIM, HEAD_DIM), F32)
              + _nbytes((r, HEAD_DIM), BF16))
    return pl.pallas_call(
        functools.partial(_ret_prompt_kernel, nchunk=nchunk),
        out_shape=(jax.ShapeDtypeStruct((m, RET_W), BF16),
                   jax.ShapeDtypeStruct((RET_HEADS, HEAD_DIM, HEAD_DIM), F32)),
        grid=(m // r, RET_HEADS),
        in_specs=[col(COL_RQ), col(COL_RK), col(COL_RV), col(COL_RG),
                  pl.BlockSpec((r, HEAD_DIM), lambda i, h: (i, 0)),
                  pl.BlockSpec((r, HEAD_DIM), lambda i, h: (i, 0)),
                  mask_tab, head_tab, head_tab,
                  pl.BlockSpec((None, 1, HEAD_DIM), lambda i, h: (l, 0, h))],
        out_specs=(pl.BlockSpec((r, HEAD_DIM), lambda i, h: (i, h)),
                   pl.BlockSpec((RET_HEADS, HEAD_DIM, HEAD_DIM), lambda i, h: (0, 0, 0))),
        compiler_params=_params(("arbitrary", "arbitrary"), blocks, 0, 4 * _nbytes((r, HEAD_DIM), F32)),
        name="ret_prompt",
    )(proj, proj, proj, proj, c_tab, s_tab, dmask, xi, zeta, ret_gn)


def _swa_prompt_kernel(sink_ref, q_ref, k_ref, v_ref, kp_ref, vp_ref, o_ref, *, nblk, l):
    kvh = pl.program_id(0)
    i = pl.program_id(1)
    kfull = jnp.concatenate([kp_ref[...], k_ref[...]], axis=0).astype(BF16)
    vfull = jnp.concatenate([vp_ref[...], v_ref[...]], axis=0).astype(BF16)
    row = lax.broadcasted_iota(jnp.int32, (WINDOW, 2 * WINDOW), 0)
    col = lax.broadcasted_iota(jnp.int32, (WINDOW, 2 * WINDOW), 1)
    band = (col >= row) & (col <= row + WINDOW)
    for j in range(nblk):
        rows = slice(j * WINDOW, (j + 1) * WINDOW)
        kk = kfull[j * WINDOW:(j + 2) * WINDOW]
        vv = vfull[j * WINDOW:(j + 2) * WINDOW]
        if j == 0:
            mask = band & (col >= jnp.where(i > 0, 0, WINDOW))
        else:
            mask = band
        qj = q_ref[rows, :].astype(BF16)
        qs = jnp.concatenate([qj[:, h * HEAD_DIM:(h + 1) * HEAD_DIM] for h in range(SWA_GROUP)], axis=0)
        s = _dot_nt(qs, kk) * ATT_SCALE
        ps = []
        for h in range(SWA_GROUP):
            sh = jnp.where(mask, s[h * WINDOW:(h + 1) * WINDOW], -jnp.inf)
            sink = sink_ref[l, kvh * SWA_GROUP + h]
            mx = jnp.maximum(jnp.max(sh, axis=-1, keepdims=True), sink)
            e = jnp.exp(sh - mx)
            den = jnp.sum(e, axis=-1, keepdims=True) + jnp.exp(sink - mx)
            ps.append((e / den).astype(BF16))
        o = _dot(jnp.concatenate(ps, axis=0), vv)
        for h in range(SWA_GROUP):
            o_ref[rows, h * HEAD_DIM:(h + 1) * HEAD_DIM] = o[h * WINDOW:(h + 1) * WINDOW].astype(o_ref.dtype)


def _swa_prompt(proj, sinks, l):
    m = proj.shape[0]
    r = 1024
    nblk = r // WINDOW
    gw = SWA_GROUP * HEAD_DIM
    prev = lambda off: pl.BlockSpec(
        (WINDOW, HEAD_DIM), lambda kv, i: (jnp.maximum(i * nblk - 1, 0), off + kv))
    blocks = (_nbytes((r, gw), F32) + 2 * _nbytes((r, HEAD_DIM), F32) + 2 * _nbytes((WINDOW, HEAD_DIM), F32)
              + _nbytes((r, gw), BF16))
    return pl.pallas_call(
        functools.partial(_swa_prompt_kernel, nblk=nblk, l=l),
        out_shape=jax.ShapeDtypeStruct((m, SWA_W), BF16),
        grid=(SWA_KV_HEADS, m // r),
        in_specs=[pl.BlockSpec(memory_space=pltpu.SMEM),
                  pl.BlockSpec((r, gw), lambda kv, i: (i, COL_SQ // SWA_GROUP + kv)),
                  pl.BlockSpec((r, HEAD_DIM), lambda kv, i: (i, COL_SK + kv)),
                  pl.BlockSpec((r, HEAD_DIM), lambda kv, i: (i, COL_SV + kv)),
                  prev(COL_SK), prev(COL_SV)],
        out_specs=pl.BlockSpec((r, gw), lambda kv, i: (i, kv)),
        compiler_params=_params(("parallel", "parallel"), blocks, 0, 8 * _nbytes((gw, 2 * WINDOW), F32)),
        name="swa_prompt",
    )(sinks, proj, proj, proj, proj, proj)


def _mem_kv_kernel(m_ref, g_ref, wk_ref, wv_ref, k_ref, v_ref):
    h = _rms_rows(m_ref[...], g_ref[...]).astype(BF16)
    k_ref[...] = _dot(h, wk_ref[...])
    v_ref[...] = _dot(h, wv_ref[...])


def _mem_kv(mem, g, wk, wv, l):
    full = lambda shape: pl.BlockSpec(shape, lambda i: (0,) * len(shape))
    layer = lambda shape: pl.BlockSpec((None,) + shape, lambda i: (l,) + (0,) * len(shape))
    blocks = (_nbytes((MEM_LEN, D_MODEL), F32) + 2 * _nbytes((D_MODEL, MEM_W), BF16)
              + 2 * _nbytes((MEM_LEN, MEM_W), F32))
    return pl.pallas_call(
        _mem_kv_kernel,
        out_shape=(jax.ShapeDtypeStruct((MEM_LEN, MEM_W), F32),) * 2,
        grid=(1,),
        in_specs=[full((MEM_LEN, D_MODEL)), _gain_spec(l, 1), layer((D_MODEL, MEM_W)), layer((D_MODEL, MEM_W))],
        out_specs=(full((MEM_LEN, MEM_W)),) * 2,
        compiler_params=_params(("arbitrary",), blocks, 0, _nbytes((MEM_LEN, D_MODEL), F32)),
        name="mem_kv",
    )(mem, g, wk, wv)


def _cross_prompt_kernel(q_ref, mk_ref, mv_ref, wo_ref, x_ref, o_ref):
    q = q_ref[...].astype(BF16)
    mk = mk_ref[...].astype(BF16)
    mv = mv_ref[...].astype(BF16)
    outs = []
    for h in range(MEM_HEADS):
        cols = slice(h * HEAD_DIM, (h + 1) * HEAD_DIM)
        s = _dot_nt(q[:, cols], mk[:, cols]) * ATT_SCALE
        e = jnp.exp(s - jnp.max(s, axis=-1, keepdims=True))
        p = e / jnp.sum(e, axis=-1, keepdims=True)
        outs.append(_dot(p.astype(BF16), mv[:, cols]).astype(BF16))
    o_ref[...] = x_ref[...] + _dot(jnp.concatenate(outs, axis=1), wo_ref[...])


def _cross_prompt(q, mk, mv, wo, x, l):
    m = x.shape[0]
    tm = 512
    blocks = (_nbytes((tm, MEM_W), F32) + 2 * _nbytes((MEM_LEN, MEM_W), F32) + _nbytes((MEM_W, D_MODEL), BF16)
              + 2 * _nbytes((tm, D_MODEL), F32))
    return pl.pallas_call(
        _cross_prompt_kernel,
        out_shape=jax.ShapeDtypeStruct((m, D_MODEL), F32),
        grid=(m // tm,),
        in_specs=[pl.BlockSpec((tm, MEM_W), lambda i: (i, 0)),
                  pl.BlockSpec((MEM_LEN, MEM_W), lambda i: (0, 0)),
                  pl.BlockSpec((MEM_LEN, MEM_W), lambda i: (0, 0)),
                  pl.BlockSpec((None, MEM_W, D_MODEL), lambda i: (l, 0, 0)),
                  pl.BlockSpec((tm, D_MODEL), lambda i: (i, 0))],
        out_specs=pl.BlockSpec((tm, D_MODEL), lambda i: (i, 0)),
        compiler_params=_params(("parallel",), blocks, 0, 2 * _nbytes((tm, D_MODEL), F32)),
        name="cross_prompt",
    )(q, mk, mv, wo, x)


def _chain(prev):
    if prev is None:
        return (), ()
    return (prev,), (pl.BlockSpec(memory_space=pl.ANY),)


def _ret_sample_kernel(gam_ref, q_ref, k_ref, v_ref, g_ref, c_ref, s_ref, gn_ref, st_ref, *rest, bt, nt):
    o_ref, nst_ref, qt_ref, kt_ref = rest[-4:]
    h = pl.program_id(0)
    t = pl.program_id(1)

    @pl.when(t == 0)
    def _():
        c_tab = c_ref[...]
        s_tab = s_ref[...]
        qt = _rotary(q_ref[...], c_tab, s_tab).T
        kt = (_rotary(k_ref[...], c_tab, s_tab) * ATT_SCALE).T
        for tt in range(nt):
            shift = (LANES - tt * bt) % LANES
            qt_ref[tt] = pltpu.roll(qt, shift, 1) if shift else qt
            kt_ref[tt] = pltpu.roll(kt, shift, 1) if shift else kt

    gamma = gam_ref[h]
    qt = qt_ref[t]
    kt = kt_ref[t]
    outs = []
    for ib in range(bt):
        new = gamma * st_ref[ib] + kt[:, ib:ib + 1] * v_ref[ib:ib + 1, :]
        nst_ref[ib] = new
        outs.append(jnp.sum(qt[:, ib:ib + 1] * new, axis=0, keepdims=True))
    out = jnp.concatenate(outs, axis=0)
    out = out * lax.rsqrt(jnp.mean(out * out, axis=-1, keepdims=True) + EPS) * gn_ref[...]
    gate = g_ref[...]
    o_ref[...] = gate * _sigmoid(gate) * out


def _ret_sample(proj, state, prev, gammas, c_tab, s_tab, ret_gn, l):
    b = proj.shape[0]
    bt = 64
    nt = b // bt
    col = lambda off: pl.BlockSpec((b, HEAD_DIM), lambda h, t: (0, off + h))
    tile = lambda off: pl.BlockSpec((bt, HEAD_DIM), lambda h, t: (t, off + h))
    row = pl.BlockSpec((1, HEAD_DIM), lambda h, t: (0, 0))
    st_spec = pl.BlockSpec((None, bt, None, HEAD_DIM, HEAD_DIM), lambda h, t: (l, t, h, 0, 0))
    extra, extra_specs = _chain(prev)
    n_in = 9 + len(extra)
    blocks = (2 * _nbytes((b, HEAD_DIM), F32) + 3 * _nbytes((bt, HEAD_DIM), F32)
              + 2 * _nbytes((bt, HEAD_DIM, HEAD_DIM), F32))
    scratch = 2 * _nbytes((nt, HEAD_DIM, LANES), F32)
    return pl.pallas_call(
        functools.partial(_ret_sample_kernel, bt=bt, nt=nt),
        out_shape=(jax.ShapeDtypeStruct((b, RET_W), F32), jax.ShapeDtypeStruct(state.shape, F32)),
        grid=(RET_HEADS, nt),
        in_specs=[pl.BlockSpec(memory_space=pltpu.SMEM),
                  col(COL_RQ), col(COL_RK), tile(COL_RV), tile(COL_RG), row, row,
                  pl.BlockSpec((None, 1, HEAD_DIM), lambda h, t: (l, 0, h)),
                  st_spec, *extra_specs],
        out_specs=(tile(0), st_spec),
        scratch_shapes=[pltpu.VMEM((nt, HEAD_DIM, LANES), F32), pltpu.VMEM((nt, HEAD_DIM, LANES), F32)],
        input_output_aliases=({n_in - 1: 1} if extra else {}),
        compiler_params=_params(("parallel", "arbitrary"), blocks, scratch, 4 * _nbytes((bt, HEAD_DIM, HEAD_DIM), F32)),
        name="ret_sample",
    )(gammas, proj, proj, proj, proj, c_tab, s_tab, ret_gn, state, *extra)


def _swa_sample_kernel(sink_ref, q_ref, kn_ref, vn_ref, ck_ref, cv_ref, *rest, bt):
    o_ref, nk_ref, nv_ref = rest[-3:]
    rows = SWA_KV_HEADS * WINDOW
    sink = sink_ref[...]
    hrow = lax.broadcasted_iota(jnp.int32, (SWA_HEADS, HEAD_DIM), 0)
    first_group = hrow < SWA_GROUP
    srow = lax.broadcasted_iota(jnp.int32, (SWA_HEADS, rows), 0)
    scol = lax.broadcasted_iota(jnp.int32, (SWA_HEADS, rows), 1)
    own = (scol & (SWA_KV_HEADS - 1)) == (srow // SWA_GROUP)

    def body(ib, carry):
        q = q_ref[ib]
        kn = kn_ref[ib]
        vn = vn_ref[ib]
        k_new = jnp.where(first_group, kn[0:1, :], kn[1:2, :])
        v_new = jnp.where(first_group, vn[0:1, :], vn[1:2, :])
        s_new = jnp.sum(q * k_new, axis=-1, keepdims=True) * ATT_SCALE
        ck = ck_ref[ib]
        cv = cv_ref[ib]
        s = jnp.where(own, _dot_nt(q.astype(BF16), ck.astype(BF16)) * ATT_SCALE, -jnp.inf)
        mx = jnp.maximum(jnp.maximum(jnp.max(s, axis=-1, keepdims=True), s_new), sink)
        e = jnp.exp(s - mx)
        e_new = jnp.exp(s_new - mx)
        den = jnp.sum(e, axis=-1, keepdims=True) + e_new + jnp.exp(sink - mx)
        o_ref[ib] = _dot((e / den).astype(BF16), cv.astype(BF16)) + (e_new / den) * v_new
        nk_ref[ib, 0:rows - SWA_KV_HEADS, :] = ck[SWA_KV_HEADS:rows, :]
        nv_ref[ib, 0:rows - SWA_KV_HEADS, :] = cv[SWA_KV_HEADS:rows, :]
        nk_ref[ib, rows - SWA_KV_HEADS:rows, :] = kn
        nv_ref[ib, rows - SWA_KV_HEADS:rows, :] = vn
        return carry

    lax.fori_loop(0, bt, body, 0, unroll=8)


def _swa_sample(q, k_new, v_new, cache_k, cache_v, prev_k, prev_v, sinks, l):
    b = q.shape[0]
    bt = 16
    rows = SWA_KV_HEADS * WINDOW
    blocks = (2 * _nbytes((bt, SWA_HEADS, HEAD_DIM), F32) + 2 * _nbytes((bt, 8, HEAD_DIM), F32)
              + 4 * _nbytes((bt, rows, HEAD_DIM), F32))
    cache_spec = pl.BlockSpec((None, bt, rows, HEAD_DIM), lambda t: (l, t, 0, 0))
    new_spec = pl.BlockSpec((bt, SWA_KV_HEADS, HEAD_DIM), lambda t: (t, 0, 0))
    head_spec = pl.BlockSpec((bt, SWA_HEADS, HEAD_DIM), lambda t: (t, 0, 0))
    extra_k, spec_k = _chain(prev_k)
    extra_v, spec_v = _chain(prev_v)
    aliases = {6: 1, 7: 2} if extra_k else {}
    return pl.pallas_call(
        functools.partial(_swa_sample_kernel, bt=bt),
        out_shape=(jax.ShapeDtypeStruct((b, SWA_HEADS, HEAD_DIM), F32),
                   jax.ShapeDtypeStruct(cache_k.shape, F32), jax.ShapeDtypeStruct(cache_v.shape, F32)),
        grid=(b // bt,),
        in_specs=[pl.BlockSpec((None, SWA_HEADS, 1), lambda t: (l, 0, 0)),
                  head_spec, new_spec, new_spec, cache_spec, cache_spec, *spec_k, *spec_v],
        out_specs=(head_spec, cache_spec, cache_spec),
        input_output_aliases=aliases,
        compiler_params=_params(("parallel",), blocks),
        name="swa_sample",
    )(sinks.reshape(DEPTH, SWA_HEADS, 1), q, k_new, v_new, cache_k, cache_v, *extra_k, *extra_v)


def _cross_sample_kernel(q_ref, mk_ref, mv_ref, o_ref, *, bt):
    rows = MEM_HEADS * MEM_LEN
    srow = lax.broadcasted_iota(jnp.int32, (MEM_HEADS, rows), 0)
    scol = lax.broadcasted_iota(jnp.int32, (MEM_HEADS, rows), 1)
    own = (scol & (MEM_HEADS - 1)) == srow

    def body(ib, carry):
        qb = q_ref[ib].astype(BF16)
        s = jnp.where(own, _dot_nt(qb, mk_ref[ib].astype(BF16)) * ATT_SCALE, -jnp.inf)
        e = jnp.exp(s - jnp.max(s, axis=-1, keepdims=True))
        p = (e / jnp.sum(e, axis=-1, keepdims=True)).astype(BF16)
        o_ref[ib] = _dot(p, mv_ref[ib].astype(BF16))
        return carry

    lax.fori_loop(0, bt, body, 0, unroll=8)


def _cross_sample(q, cache_k, cache_v, l):
    b = q.shape[0]
    bt = 16
    rows = MEM_HEADS * MEM_LEN
    blocks = 2 * _nbytes((bt, 8, HEAD_DIM), F32) + 2 * _nbytes((bt, rows, HEAD_DIM), F32)
    cache_spec = pl.BlockSpec((None, bt, rows, HEAD_DIM), lambda t: (l, t, 0, 0))
    head_spec = pl.BlockSpec((bt, MEM_HEADS, HEAD_DIM), lambda t: (t, 0, 0))
    return pl.pallas_call(
        functools.partial(_cross_sample_kernel, bt=bt),
        out_shape=jax.ShapeDtypeStruct((b, MEM_HEADS, HEAD_DIM), F32),
        grid=(b // bt,),
        in_specs=[head_spec, cache_spec, cache_spec],
        out_specs=head_spec,
        compiler_params=_params(("parallel",), blocks),
        name="cross_sample",
    )(q, cache_k, cache_v)


def kernel(x_prompt, x_sample, mem_prompt, state_ret, cache_win_k, cache_win_v, cache_mem_k, cache_mem_v,
           g_mix, w_in, ret_gn, sinks, w_out, g_cross, g_mem, wq_c, wk_c, wv_c, wo_c,
           g_ffn, w_gate, w_up, w_down, g_final):
    seq = x_prompt.shape[1]
    nb = x_sample.shape[0]
    w_in_b, w_out_b = _to_bf16_tiles(w_in, IN_PROJ_TN), _to_bf16(w_out)
    wq_b, wk_b, wv_b, wo_b = (_to_bf16(w) for w in (wq_c, wk_c, wv_c, wo_c))
    wg_b, wu_b, wd_b = _to_bf16_tiles(w_gate, FFN_TF), _to_bf16_tiles(w_up, FFN_TF), _to_bf16(w_down)
    gain = lambda g: g.reshape(DEPTH, 1, g.shape[-1])
    g_mix, g_cross, g_mem, g_ffn, ret_gn = (gain(g) for g in (g_mix, g_cross, g_mem, g_ffn, ret_gn))

    cp, sp = (jnp.asarray(t) for t in _rope_tables(np.arange(seq)))
    cs, ss = (jnp.asarray(t) for t in _rope_tables(np.array([PAST_LEN])))
    dmask, xi, zeta = (jnp.asarray(t) for t in _decay_tables())
    gammas = jnp.asarray((1.0 - np.exp2(-5.0 - np.arange(RET_HEADS))).astype(np.float32))

    xp = x_prompt[0]
    xs = x_sample[:, 0]
    mem = mem_prompt[0]
    kvw = SWA_KV_HEADS * HEAD_DIM
    win_k = cache_win_k.reshape(DEPTH, nb, WINDOW * SWA_KV_HEADS, HEAD_DIM)
    win_v = cache_win_v.reshape(DEPTH, nb, WINDOW * SWA_KV_HEADS, HEAD_DIM)
    mem_k = cache_mem_k.reshape(DEPTH, nb, MEM_LEN * MEM_HEADS, HEAD_DIM)
    mem_v = cache_mem_v.reshape(DEPTH, nb, MEM_LEN * MEM_HEADS, HEAD_DIM)

    ret_p, wink_p, winv_p, memk_p, memv_p = [], [], [], [], []
    ret_s = wink_s = winv_s = None
    for l in range(DEPTH):
        proj = _in_proj(xp, g_mix, w_in_b, l)
        ro, r_last = _ret_prompt(proj, cp, sp, dmask, xi, zeta, ret_gn, l)
        so = _swa_prompt(proj, sinks, l)
        x1, qc = _out_proj(ro, so, xp, w_out_b, g_cross, wq_b, l)
        mk, mv = _mem_kv(mem, g_mem, wk_b, wv_b, l)
        x2 = _cross_prompt(qc, mk, mv, wo_b, x1, l)
        xp = _ffn(x2, g_ffn, wg_b, wu_b, wd_b, g_final, l)
        ret_p.append(r_last[None])
        tail = proj[seq - WINDOW:]
        wink_p.append(tail[:, COL_SK * LANES:COL_SK * LANES + kvw].reshape(1, WINDOW, SWA_KV_HEADS, HEAD_DIM))
        winv_p.append(tail[:, COL_SV * LANES:COL_SV * LANES + kvw].reshape(1, WINDOW, SWA_KV_HEADS, HEAD_DIM))
        memk_p.append(mk.reshape(1, MEM_LEN, MEM_HEADS, HEAD_DIM))
        memv_p.append(mv.reshape(1, MEM_LEN, MEM_HEADS, HEAD_DIM))

        proj_s = _in_proj(xs, g_mix, w_in_b, l)
        ro_s, ret_s = _ret_sample(proj_s, state_ret, ret_s, gammas, cs, ss, ret_gn, l)
        sq = proj_s[:, COL_SQ * LANES:COL_SQ * LANES + SWA_W].reshape(nb, SWA_HEADS, HEAD_DIM)
        sk = proj_s[:, COL_SK * LANES:COL_SK * LANES + kvw].reshape(nb, SWA_KV_HEADS, HEAD_DIM)
        sv = proj_s[:, COL_SV * LANES:COL_SV * LANES + kvw].reshape(nb, SWA_KV_HEADS, HEAD_DIM)
        so_s, wink_s, winv_s = _swa_sample(sq, sk, sv, win_k, win_v, wink_s, winv_s, sinks, l)
        x1_s, qc_s = _out_proj(ro_s, so_s.reshape(nb, SWA_W), xs, w_out_b, g_cross, wq_b, l)
        oc_s = _cross_sample(qc_s.reshape(nb, MEM_HEADS, HEAD_DIM), mem_k, mem_v, l)
        x2_s = _matmul_residual(oc_s.reshape(nb, MEM_W), wo_b, x1_s, l)
        xs = _ffn(x2_s, g_ffn, wg_b, wu_b, wd_b, g_final, l)

    cache_shape = (DEPTH, nb, WINDOW, SWA_KV_HEADS, HEAD_DIM)
    return (xp[None], xs[:, None],
            jnp.stack(ret_p), jnp.stack(wink_p), jnp.stack(winv_p), jnp.stack(memk_p), jnp.stack(memv_p),
            ret_s, wink_s.reshape(cache_shape), winv_s.reshape(cache_shape))
```

```python
import functools

import numpy as np
import jax
import jax.numpy as jnp
from jax import lax
from jax.experimental import pallas as pl
from jax.experimental.pallas import tpu as pltpu

D_MODEL = 2048
DEPTH = 2
PAST_LEN = 8192
HEAD_DIM = 128
RET_HEADS = 8
RET_W = RET_HEADS * HEAD_DIM
RET_CHUNK = 256
SWA_HEADS = 8
SWA_KV_HEADS = 2
SWA_GROUP = SWA_HEADS // SWA_KV_HEADS
SWA_W = SWA_HEADS * HEAD_DIM
WINDOW = 128
MEM_LEN = 256
MEM_HEADS = 4
MEM_W = MEM_HEADS * HEAD_DIM
IN_W = 4 * RET_W + SWA_W + 2 * SWA_KV_HEADS * HEAD_DIM
D_FF = 5632
ROPE_BASE = 10000.0
EPS = 1e-6
ATT_SCALE = HEAD_DIM ** -0.5

COL_RQ, COL_RK, COL_RV, COL_RG = 0, 8, 16, 24
COL_SQ, COL_SK, COL_SV = 32, 40, 42

V7X_VMEM_BYTES = 64 * 1024 * 1024
V7X_VMEM_BUDGET = V7X_VMEM_BYTES - 8 * 1024 * 1024
LANES = 128
CAST_BLOCK_BYTES = 4 * 1024 * 1024

BF16 = jnp.bfloat16
F32 = jnp.float32


def _vmem_limit(block_bytes, scratch_bytes=0, temp_bytes=0):
    need = 2 * block_bytes + scratch_bytes + temp_bytes + 4 * 1024 * 1024
    return int(min(max(need, 16 * 1024 * 1024), V7X_VMEM_BUDGET))


def _nbytes(shape, dtype):
    return int(np.prod(shape)) * jnp.dtype(dtype).itemsize


def _params(semantics, block_bytes, scratch_bytes=0, temp_bytes=0):
    return pltpu.CompilerParams(
        dimension_semantics=semantics,
        vmem_limit_bytes=_vmem_limit(block_bytes, scratch_bytes, temp_bytes))


def _rms_rows(x, g):
    ms = jnp.mean(x * x, axis=-1, keepdims=True)
    return x * lax.rsqrt(ms + EPS) * g


def _rms_to(dst_ref, src_ref, g_ref, rows):
    g = g_ref[...]
    chunk = min(rows, 128)

    def body(c, carry):
        r = pl.multiple_of(c * chunk, chunk)
        x = src_ref[pl.ds(r, chunk), :]
        dst_ref[pl.ds(r, chunk), :] = _rms_rows(x, g).astype(dst_ref.dtype)
        return carry

    lax.fori_loop(0, rows // chunk, body, 0)


def _sigmoid(x):
    return 1.0 / (1.0 + jnp.exp(-x))


def _dot(a, b):
    return jnp.dot(a, b, preferred_element_type=F32)


def _dot_nt(a, b):
    return lax.dot_general(a, b, (((1,), (1,)), ((), ())), preferred_element_type=F32)


def _gain_spec(l, ngrid):
    zeros = (0,) * 2
    return pl.BlockSpec((None, 1, D_MODEL), lambda *_: (l,) + zeros)


def _cast_kernel(x_ref, o_ref):
    o_ref[...] = x_ref[...].astype(o_ref.dtype)


def _to_bf16(w):
    depth, r, c = w.shape
    tr = r
    while _nbytes((tr, c), F32) > CAST_BLOCK_BYTES and tr % 32 == 0:
        tr //= 2
    spec = pl.BlockSpec((None, tr, c), lambda d, i: (d, i, 0))
    return pl.pallas_call(
        _cast_kernel,
        out_shape=jax.ShapeDtypeStruct(w.shape, BF16),
        grid=(depth, r // tr),
        in_specs=[spec],
        out_specs=spec,
        compiler_params=_params(("parallel", "parallel"), _nbytes((tr, c), F32) + _nbytes((tr, c), BF16)),
        name="to_bf16",
    )(w)


def _cast_tiles_kernel(x_ref, o_ref, *, nt, tn):
    for j in range(nt):
        o_ref[j] = x_ref[:, j * tn:(j + 1) * tn].astype(o_ref.dtype)


def _to_bf16_tiles(w, tn):
    depth, r, c = w.shape
    nt = c // tn
    tr = r
    while _nbytes((tr, c), F32) > CAST_BLOCK_BYTES and tr % 32 == 0:
        tr //= 2
    return pl.pallas_call(
        functools.partial(_cast_tiles_kernel, nt=nt, tn=tn),
        out_shape=jax.ShapeDtypeStruct((depth, nt, r, tn), BF16),
        grid=(depth, r // tr),
        in_specs=[pl.BlockSpec((None, tr, c), lambda d, i: (d, i, 0))],
        out_specs=pl.BlockSpec((None, nt, tr, tn), lambda d, i: (d, 0, i, 0)),
        compiler_params=_params(("parallel", "parallel"), _nbytes((tr, c), F32) + _nbytes((tr, c), BF16)),
        name="to_bf16_tiles",
    )(w)


def _in_proj_kernel(x_ref, g_ref, w_ref, o_ref, h_ref, *, tm):
    @pl.when(pl.program_id(1) == 0)
    def _():
        _rms_to(h_ref, x_ref, g_ref, tm)

    o_ref[...] = _dot(h_ref[...], w_ref[...])


IN_PROJ_TN = 2816


def _in_proj(x, g, w, l):
    m = x.shape[0]
    tm = min(m, 512)
    tn = IN_PROJ_TN
    blocks = (_nbytes((tm, D_MODEL), F32) + _nbytes((D_MODEL, tn), BF16) + _nbytes((tm, tn), F32))
    return pl.pallas_call(
        functools.partial(_in_proj_kernel, tm=tm),
        out_shape=jax.ShapeDtypeStruct((m, IN_W), F32),
        grid=(m // tm, IN_W // tn),
        in_specs=[
            pl.BlockSpec((tm, D_MODEL), lambda i, j: (i, 0)),
            _gain_spec(l, 2),
            pl.BlockSpec((None, None, D_MODEL, tn), lambda i, j: (l, j, 0, 0)),
        ],
        out_specs=pl.BlockSpec((tm, tn), lambda i, j: (i, j)),
        scratch_shapes=[pltpu.VMEM((tm, D_MODEL), BF16)],
        compiler_params=_params(("parallel", "arbitrary"), blocks, _nbytes((tm, D_MODEL), BF16)),
        name="in_proj",
    )(x, g, w)


def _out_proj_kernel(ro_ref, so_ref, x_ref, wa_ref, wb_ref, gc_ref, wq_ref, x1_ref, q_ref, h_ref, *, tm):
    y = _dot(ro_ref[...].astype(BF16), wa_ref[...]) + _dot(so_ref[...].astype(BF16), wb_ref[...])
    x1_ref[...] = x_ref[...] + y
    _rms_to(h_ref, x1_ref, gc_ref, tm)
    q_ref[...] = _dot(h_ref[...], wq_ref[...])


def _out_proj(ro, so, x, w_out, g_cross, wq, l):
    m = x.shape[0]
    tm = min(m, 512)
    blocks = (_nbytes((tm, RET_W), ro.dtype) + _nbytes((tm, SWA_W), so.dtype)
              + 2 * _nbytes((tm, D_MODEL), F32) + _nbytes((D_MODEL, D_MODEL), BF16)
              + _nbytes((D_MODEL, MEM_W), BF16) + _nbytes((tm, MEM_W), F32))
    return pl.pallas_call(
        functools.partial(_out_proj_kernel, tm=tm),
        out_shape=(jax.ShapeDtypeStruct((m, D_MODEL), F32), jax.ShapeDtypeStruct((m, MEM_W), F32)),
        grid=(m // tm,),
        in_specs=[
            pl.BlockSpec((tm, RET_W), lambda i: (i, 0)),
            pl.BlockSpec((tm, SWA_W), lambda i: (i, 0)),
            pl.BlockSpec((tm, D_MODEL), lambda i: (i, 0)),
            pl.BlockSpec((None, RET_W, D_MODEL), lambda i: (l, 0, 0)),
            pl.BlockSpec((None, SWA_W, D_MODEL), lambda i: (l, 1, 0)),
            _gain_spec(l, 1),
            pl.BlockSpec((None, D_MODEL, MEM_W), lambda i: (l, 0, 0)),
        ],
        out_specs=(pl.BlockSpec((tm, D_MODEL), lambda i: (i, 0)),
                   pl.BlockSpec((tm, MEM_W), lambda i: (i, 0))),
        scratch_shapes=[pltpu.VMEM((tm, D_MODEL), BF16)],
        compiler_params=_params(("parallel",), blocks, _nbytes((tm, D_MODEL), BF16),
                                _nbytes((tm, D_MODEL), F32)),
        name="out_proj",
    )(ro, so, x, w_out, w_out, g_cross, wq)


def _ffn_kernel(x_ref, g_ref, wg_ref, wu_ref, wd_ref, gf_ref, o_ref, h_ref, *, tm, nf, final_norm):
    j = pl.program_id(1)

    @pl.when(j == 0)
    def _():
        _rms_to(h_ref, x_ref, g_ref, tm)
        o_ref[...] = x_ref[...]

    h = h_ref[...]
    a = _dot(h, wg_ref[...])
    u = _dot(h, wu_ref[...])
    act = (a * _sigmoid(a) * u).astype(BF16)
    half = D_MODEL // 2
    for c in range(2):
        cols = slice(c * half, (c + 1) * half)
        o_ref[:, cols] += _dot(act, wd_ref[:, cols])

    if final_norm:
        @pl.when(j == nf - 1)
        def _():
            _rms_to(o_ref, o_ref, gf_ref, tm)


FFN_TF = 512


def _ffn(x, g, wg, wu, wd, g_final, l):
    m = x.shape[0]
    tm = min(m, 512)
    tf = FFN_TF
    nf = D_FF // tf
    blocks = 2 * _nbytes((tm, D_MODEL), F32) + 3 * _nbytes((D_MODEL, tf), BF16)
    scratch = _nbytes((tm, D_MODEL), BF16)
    temps = 3 * _nbytes((tm, tf), F32) + _nbytes((tm, D_MODEL // 2), F32)
    return pl.pallas_call(
        functools.partial(_ffn_kernel, tm=tm, nf=nf, final_norm=(l == DEPTH - 1)),
        out_shape=jax.ShapeDtypeStruct((m, D_MODEL), F32),
        grid=(m // tm, nf),
        in_specs=[
            pl.BlockSpec((tm, D_MODEL), lambda i, j: (i, 0)),
            _gain_spec(l, 2),
            pl.BlockSpec((None, None, D_MODEL, tf), lambda i, j: (l, j, 0, 0)),
            pl.BlockSpec((None, None, D_MODEL, tf), lambda i, j: (l, j, 0, 0)),
            pl.BlockSpec((None, tf, D_MODEL), lambda i, j: (l, j, 0)),
            pl.BlockSpec((1, D_MODEL), lambda i, j: (0, 0)),
        ],
        out_specs=pl.BlockSpec((tm, D_MODEL), lambda i, j: (i, 0)),
        scratch_shapes=[pltpu.VMEM((tm, D_MODEL), BF16)],
        compiler_params=_params(("parallel", "arbitrary"), blocks, scratch, temps),
        name="ffn",
    )(x, g, wg, wu, wd, g_final.reshape(1, D_MODEL))


def _matmul_residual_kernel(a_ref, w_ref, x_ref, o_ref):
    o_ref[...] = x_ref[...] + _dot(a_ref[...].astype(BF16), w_ref[...])


def _matmul_residual(a, w, x, l):
    m, k = a.shape
    n = w.shape[2]
    blocks = _nbytes((m, k), a.dtype) + _nbytes((k, n), BF16) + 2 * _nbytes((m, n), F32)
    return pl.pallas_call(
        _matmul_residual_kernel,
        out_shape=jax.ShapeDtypeStruct((m, n), F32),
        grid=(1,),
        in_specs=[pl.BlockSpec((m, k), lambda i: (0, 0)),
                  pl.BlockSpec((None, k, n), lambda i: (l, 0, 0)),
                  pl.BlockSpec((m, n), lambda i: (0, 0))],
        out_specs=pl.BlockSpec((m, n), lambda i: (0, 0)),
        compiler_params=_params(("arbitrary",), blocks),
        name="matmul_residual",
    )(a, w, x)


def _rope_tables(positions):
    half = HEAD_DIM // 2
    inv = ROPE_BASE ** (-np.arange(half, dtype=np.float64) / half)
    ang = np.asarray(positions, np.float64)[:, None] * inv[None, :]
    cos, sin = np.cos(ang), np.sin(ang)
    c = np.concatenate([cos, cos], axis=-1).astype(np.float32)
    s = np.concatenate([-sin, sin], axis=-1).astype(np.float32)
    return c, s


def _decay_tables():
    c = RET_CHUNK
    lg = np.log1p(-np.exp2(-5.0 - np.arange(RET_HEADS, dtype=np.float64)))
    idx = np.arange(c, dtype=np.float64)
    diff = idx[:, None] - idx[None, :]
    dmask = np.where(diff[None] >= 0, np.exp(lg[:, None, None] * np.maximum(diff, 0.0)[None]), 0.0)
    xi = np.exp(lg[:, None] * (idx + 1.0)[None])
    zeta = np.exp(lg[:, None] * (c - 1.0 - idx)[None])
    bcast = lambda t: np.broadcast_to(t[:, :, None], (RET_HEADS, c, LANES)).astype(np.float32)
    return dmask.astype(np.float32), bcast(xi), bcast(zeta)


def _rotary(x, c, s):
    return x * c + pltpu.roll(x, HEAD_DIM // 2, 1) * s


def _ret_prompt_kernel(q_ref, k_ref, v_ref, g_ref, c_ref, s_ref, dm_ref, xi_ref, zt_ref, gn_ref,
                       o_ref, st_ref, *, nchunk):
    h = pl.program_id(1)

    @pl.when(pl.program_id(0) == 0)
    def _():
        st_ref[h] = jnp.zeros((HEAD_DIM, HEAD_DIM), F32)

    c_tab = c_ref[...]
    s_tab = s_ref[...]
    qr = _rotary(q_ref[...], c_tab, s_tab)
    kr = _rotary(k_ref[...], c_tab, s_tab) * ATT_SCALE
    dmask = dm_ref[h]
    xi = xi_ref[h]
    zeta = zt_ref[h]
    g_chunk = xi[RET_CHUNK - 1:RET_CHUNK, :]
    gn = gn_ref[...]
    state = st_ref[h]
    for c in range(nchunk):
        rows = slice(c * RET_CHUNK, (c + 1) * RET_CHUNK)
        qc = qr[rows]
        kc = kr[rows]
        vc = v_ref[rows, :].astype(BF16)
        s = _dot_nt(qc.astype(BF16), kc.astype(BF16)) * dmask
        inner = _dot(s.astype(BF16), vc)
        cross = _dot((qc * xi).astype(BF16), state.astype(BF16))
        kv = _dot((kc * zeta).T.astype(BF16), vc)
        state = g_chunk * state + kv
        out = inner + cross
        out = out * lax.rsqrt(jnp.mean(out * out, axis=-1, keepdims=True) + EPS) * gn
        gate = g_ref[rows, :]
        o_ref[rows, :] = (gate * _sigmoid(gate) * out).astype(o_ref.dtype)
    st_ref[h] = state


def _ret_prompt(proj, c_tab, s_tab, dmask, xi, zeta, ret_gn, l):
    m = proj.shape[0]
    r = 4096
    nchunk = r // RET_CHUNK
    col = lambda off: pl.BlockSpec((r, HEAD_DIM), lambda i, h: (i, off + h))
    head_tab = pl.BlockSpec((RET_HEADS, RET_CHUNK, LANES), lambda i, h: (0, 0, 0))
    mask_tab = pl.BlockSpec((RET_HEADS, RET_CHUNK, RET_CHUNK), lambda i, h: (0, 0, 0))
    blocks = (6 * _nbytes((r, HEAD_DIM), F32) + 2 * _nbytes((RET_HEADS, RET_CHUNK, LANES), F32)
              + _nbytes((RET_HEADS, RET_CHUNK, RET_CHUNK), F32) + _nbytes((RET_HEADS, HEAD_DIM, HEAD_DIM), F32)
              + _nbytes((r, HEAD_DIM), BF16))
    return pl.pallas_call(
        functools.partial(_ret_prompt_kernel, nchunk=nchunk),
        out_shape=(jax.ShapeDtypeStruct((m, RET_W), BF16),
                   jax.ShapeDtypeStruct((RET_HEADS, HEAD_DIM, HEAD_DIM), F32)),
        grid=(m // r, RET_HEADS),
        in_specs=[col(COL_RQ), col(COL_RK), col(COL_RV), col(COL_RG),
                  pl.BlockSpec((r, HEAD_DIM), lambda i, h: (i, 0)),
                  pl.BlockSpec((r, HEAD_DIM), lambda i, h: (i, 0)),
                  mask_tab, head_tab, head_tab,
                  pl.BlockSpec((None, 1, HEAD_DIM), lambda i, h: (l, 0, h))],
        out_specs=(pl.BlockSpec((r, HEAD_DIM), lambda i, h: (i, h)),
                   pl.BlockSpec((RET_HEADS, HEAD_DIM, HEAD_DIM), lambda i, h: (0, 0, 0))),
        compiler_params=_params(("arbitrary", "arbitrary"), blocks, 0, 4 * _nbytes((r, HEAD_DIM), F32)),
        name="ret_prompt",
    )(proj, proj, proj, proj, c_tab, s_tab, dmask, xi, zeta, ret_gn)


def _swa_prompt_kernel(sink_ref, q_ref, k_ref, v_ref, kp_ref, vp_ref, o_ref, *, nblk, l):
    kvh = pl.program_id(0)
    i = pl.program_id(1)
    kfull = jnp.concatenate([kp_ref[...], k_ref[...]], axis=0).astype(BF16)
    vfull = jnp.concatenate([vp_ref[...], v_ref[...]], axis=0).astype(BF16)
    row = lax.broadcasted_iota(jnp.int32, (WINDOW, 2 * WINDOW), 0)
    col = lax.broadcasted_iota(jnp.int32, (WINDOW, 2 * WINDOW), 1)
    band = (col >= row) & (col <= row + WINDOW)
    for j in range(nblk):
        rows = slice(j * WINDOW, (j + 1) * WINDOW)
        kk = kfull[j * WINDOW:(j + 2) * WINDOW]
        vv = vfull[j * WINDOW:(j + 2) * WINDOW]
        if j == 0:
            mask = band & (col >= jnp.where(i > 0, 0, WINDOW))
        else:
            mask = band
        qj = q_ref[rows, :].astype(BF16)
        qs = jnp.concatenate([qj[:, h * HEAD_DIM:(h + 1) * HEAD_DIM] for h in range(SWA_GROUP)], axis=0)
        s = _dot_nt(qs, kk) * ATT_SCALE
        ps = []
        for h in range(SWA_GROUP):
            sh = jnp.where(mask, s[h * WINDOW:(h + 1) * WINDOW], -jnp.inf)
            sink = sink_ref[l, kvh * SWA_GROUP + h]
            mx = jnp.maximum(jnp.max(sh, axis=-1, keepdims=True), sink)
            e = jnp.exp(sh - mx)
            den = jnp.sum(e, axis=-1, keepdims=True) + jnp.exp(sink - mx)
            ps.append((e / den).astype(BF16))
        o = _dot(jnp.concatenate(ps, axis=0), vv)
        for h in range(SWA_GROUP):
            o_ref[rows, h * HEAD_DIM:(h + 1) * HEAD_DIM] = o[h * WINDOW:(h + 1) * WINDOW].astype(o_ref.dtype)


def _swa_prompt(proj, sinks, l):
    m = proj.shape[0]
    r = 2048
    nblk = r // WINDOW
    gw = SWA_GROUP * HEAD_DIM
    prev = lambda off: pl.BlockSpec(
        (WINDOW, HEAD_DIM), lambda kv, i: (jnp.maximum(i * nblk - 1, 0), off + kv))
    blocks = (_nbytes((r, gw), F32) + 2 * _nbytes((r, HEAD_DIM), F32) + 2 * _nbytes((WINDOW, HEAD_DIM), F32)
              + _nbytes((r, gw), BF16))
    return pl.pallas_call(
        functools.partial(_swa_prompt_kernel, nblk=nblk, l=l),
        out_shape=jax.ShapeDtypeStruct((m, SWA_W), BF16),
        grid=(SWA_KV_HEADS, m // r),
        in_specs=[pl.BlockSpec(memory_space=pltpu.SMEM),
                  pl.BlockSpec((r, gw), lambda kv, i: (i, COL_SQ // SWA_GROUP + kv)),
                  pl.BlockSpec((r, HEAD_DIM), lambda kv, i: (i, COL_SK + kv)),
                  pl.BlockSpec((r, HEAD_DIM), lambda kv, i: (i, COL_SV + kv)),
                  prev(COL_SK), prev(COL_SV)],
        out_specs=pl.BlockSpec((r, gw), lambda kv, i: (i, kv)),
        compiler_params=_params(("parallel", "parallel"), blocks, 0, 8 * _nbytes((gw, 2 * WINDOW), F32)),
        name="swa_prompt",
    )(sinks, proj, proj, proj, proj, proj)


def _mem_kv_kernel(m_ref, g_ref, wk_ref, wv_ref, k_ref, v_ref):
    h = _rms_rows(m_ref[...], g_ref[...]).astype(BF16)
    k_ref[...] = _dot(h, wk_ref[...])
    v_ref[...] = _dot(h, wv_ref[...])


def _mem_kv(mem, g, wk, wv, l):
    full = lambda shape: pl.BlockSpec(shape, lambda i: (0,) * len(shape))
    layer = lambda shape: pl.BlockSpec((None,) + shape, lambda i: (l,) + (0,) * len(shape))
    blocks = (_nbytes((MEM_LEN, D_MODEL), F32) + 2 * _nbytes((D_MODEL, MEM_W), BF16)
              + 2 * _nbytes((MEM_LEN, MEM_W), F32))
    return pl.pallas_call(
        _mem_kv_kernel,
        out_shape=(jax.ShapeDtypeStruct((MEM_LEN, MEM_W), F32),) * 2,
        grid=(1,),
        in_specs=[full((MEM_LEN, D_MODEL)), _gain_spec(l, 1), layer((D_MODEL, MEM_W)), layer((D_MODEL, MEM_W))],
        out_specs=(full((MEM_LEN, MEM_W)),) * 2,
        compiler_params=_params(("arbitrary",), blocks, 0, _nbytes((MEM_LEN, D_MODEL), F32)),
        name="mem_kv",
    )(mem, g, wk, wv)


def _cross_prompt_kernel(q_ref, mk_ref, mv_ref, wo_ref, x_ref, o_ref):
    q = q_ref[...].astype(BF16)
    mk = mk_ref[...].astype(BF16)
    mv = mv_ref[...].astype(BF16)
    outs = []
    for h in range(MEM_HEADS):
        cols = slice(h * HEAD_DIM, (h + 1) * HEAD_DIM)
        s = _dot_nt(q[:, cols], mk[:, cols]) * ATT_SCALE
        e = jnp.exp(s - jnp.max(s, axis=-1, keepdims=True))
        p = e / jnp.sum(e, axis=-1, keepdims=True)
        outs.append(_dot(p.astype(BF16), mv[:, cols]).astype(BF16))
    o_ref[...] = x_ref[...] + _dot(jnp.concatenate(outs, axis=1), wo_ref[...])


def _cross_prompt(q, mk, mv, wo, x, l):
    m = x.shape[0]
    tm = 512
    blocks = (_nbytes((tm, MEM_W), F32) + 2 * _nbytes((MEM_LEN, MEM_W), F32) + _nbytes((MEM_W, D_MODEL), BF16)
              + 2 * _nbytes((tm, D_MODEL), F32))
    return pl.pallas_call(
        _cross_prompt_kernel,
        out_shape=jax.ShapeDtypeStruct((m, D_MODEL), F32),
        grid=(m // tm,),
        in_specs=[pl.BlockSpec((tm, MEM_W), lambda i: (i, 0)),
                  pl.BlockSpec((MEM_LEN, MEM_W), lambda i: (0, 0)),
                  pl.BlockSpec((MEM_LEN, MEM_W), lambda i: (0, 0)),
                  pl.BlockSpec((None, MEM_W, D_MODEL), lambda i: (l, 0, 0)),
                  pl.BlockSpec((tm, D_MODEL), lambda i: (i, 0))],
        out_specs=pl.BlockSpec((tm, D_MODEL), lambda i: (i, 0)),
        compiler_params=_params(("parallel",), blocks, 0, 2 * _nbytes((tm, D_MODEL), F32)),
        name="cross_prompt",
    )(q, mk, mv, wo, x)


def _chain(prev):
    if prev is None:
        return (), ()
    return (prev,), (pl.BlockSpec(memory_space=pl.ANY),)


def _ret_sample_kernel(gam_ref, q_ref, k_ref, v_ref, g_ref, c_ref, s_ref, gn_ref, st_ref, *rest, bt, nt):
    o_ref, nst_ref, qt_ref, kt_ref = rest[-4:]
    h = pl.program_id(0)
    t = pl.program_id(1)

    @pl.when(t == 0)
    def _():
        c_tab = c_ref[...]
        s_tab = s_ref[...]
        qt = _rotary(q_ref[...], c_tab, s_tab).T
        kt = (_rotary(k_ref[...], c_tab, s_tab) * ATT_SCALE).T
        for tt in range(nt):
            shift = (LANES - tt * bt) % LANES
            qt_ref[tt] = pltpu.roll(qt, shift, 1) if shift else qt
            kt_ref[tt] = pltpu.roll(kt, shift, 1) if shift else kt

    gamma = gam_ref[h]
    qt = qt_ref[t]
    kt = kt_ref[t]
    outs = []
    for ib in range(bt):
        new = gamma * st_ref[ib] + kt[:, ib:ib + 1] * v_ref[ib:ib + 1, :]
        nst_ref[ib] = new
        outs.append(jnp.sum(qt[:, ib:ib + 1] * new, axis=0, keepdims=True))
    out = jnp.concatenate(outs, axis=0)
    out = out * lax.rsqrt(jnp.mean(out * out, axis=-1, keepdims=True) + EPS) * gn_ref[...]
    gate = g_ref[...]
    o_ref[...] = gate * _sigmoid(gate) * out


def _ret_sample(proj, state, prev, gammas, c_tab, s_tab, ret_gn, l):
    b = proj.shape[0]
    bt = 128
    nt = b // bt
    col = lambda off: pl.BlockSpec((b, HEAD_DIM), lambda h, t: (0, off + h))
    tile = lambda off: pl.BlockSpec((bt, HEAD_DIM), lambda h, t: (t, off + h))
    row = pl.BlockSpec((1, HEAD_DIM), lambda h, t: (0, 0))
    st_spec = pl.BlockSpec((None, bt, None, HEAD_DIM, HEAD_DIM), lambda h, t: (l, t, h, 0, 0))
    extra, extra_specs = _chain(prev)
    n_in = 9 + len(extra)
    blocks = (2 * _nbytes((b, HEAD_DIM), F32) + 3 * _nbytes((bt, HEAD_DIM), F32)
              + 2 * _nbytes((bt, HEAD_DIM, HEAD_DIM), F32))
    scratch = 2 * _nbytes((nt, HEAD_DIM, LANES), F32)
    return pl.pallas_call(
        functools.partial(_ret_sample_kernel, bt=bt, nt=nt),
        out_shape=(jax.ShapeDtypeStruct((b, RET_W), F32), jax.ShapeDtypeStruct(state.shape, F32)),
        grid=(RET_HEADS, nt),
        in_specs=[pl.BlockSpec(memory_space=pltpu.SMEM),
                  col(COL_RQ), col(COL_RK), tile(COL_RV), tile(COL_RG), row, row,
                  pl.BlockSpec((None, 1, HEAD_DIM), lambda h, t: (l, 0, h)),
                  st_spec, *extra_specs],
        out_specs=(tile(0), st_spec),
        scratch_shapes=[pltpu.VMEM((nt, HEAD_DIM, LANES), F32), pltpu.VMEM((nt, HEAD_DIM, LANES), F32)],
        input_output_aliases=({n_in - 1: 1} if extra else {}),
        compiler_params=_params(("parallel", "arbitrary"), blocks, scratch, 4 * _nbytes((bt, HEAD_DIM, HEAD_DIM), F32)),
        name="ret_sample",
    )(gammas, proj, proj, proj, proj, c_tab, s_tab, ret_gn, state, *extra)


def _swa_sample_kernel(sink_ref, q_ref, kn_ref, vn_ref, ck_ref, cv_ref, *rest, bt):
    o_ref, nk_ref, nv_ref = rest[-3:]
    rows = SWA_KV_HEADS * WINDOW
    sink = sink_ref[...]
    hrow = lax.broadcasted_iota(jnp.int32, (SWA_HEADS, HEAD_DIM), 0)
    first_group = hrow < SWA_GROUP
    srow = lax.broadcasted_iota(jnp.int32, (SWA_HEADS, rows), 0)
    scol = lax.broadcasted_iota(jnp.int32, (SWA_HEADS, rows), 1)
    own = (scol & (SWA_KV_HEADS - 1)) == (srow // SWA_GROUP)

    def body(ib, carry):
        q = q_ref[ib]
        kn = kn_ref[ib]
        vn = vn_ref[ib]
        k_new = jnp.where(first_group, kn[0:1, :], kn[1:2, :])
        v_new = jnp.where(first_group, vn[0:1, :], vn[1:2, :])
        s_new = jnp.sum(q * k_new, axis=-1, keepdims=True) * ATT_SCALE
        ck = ck_ref[ib]
        cv = cv_ref[ib]
        s = jnp.where(own, _dot_nt(q.astype(BF16), ck.astype(BF16)) * ATT_SCALE, -jnp.inf)
        mx = jnp.maximum(jnp.maximum(jnp.max(s, axis=-1, keepdims=True), s_new), sink)
        e = jnp.exp(s - mx)
        e_new = jnp.exp(s_new - mx)
        den = jnp.sum(e, axis=-1, keepdims=True) + e_new + jnp.exp(sink - mx)
        o_ref[ib] = _dot((e / den).astype(BF16), cv.astype(BF16)) + (e_new / den) * v_new
        nk_ref[ib, 0:rows - SWA_KV_HEADS, :] = ck[SWA_KV_HEADS:rows, :]
        nv_ref[ib, 0:rows - SWA_KV_HEADS, :] = cv[SWA_KV_HEADS:rows, :]
        nk_ref[ib, rows - SWA_KV_HEADS:rows, :] = kn
        nv_ref[ib, rows - SWA_KV_HEADS:rows, :] = vn
        return carry

    lax.fori_loop(0, bt, body, 0, unroll=8)


def _swa_sample(q, k_new, v_new, cache_k, cache_v, prev_k, prev_v, sinks, l):
    b = q.shape[0]
    bt = 32
    rows = SWA_KV_HEADS * WINDOW
    blocks = (2 * _nbytes((bt, SWA_HEADS, HEAD_DIM), F32) + 2 * _nbytes((bt, 8, HEAD_DIM), F32)
              + 4 * _nbytes((bt, rows, HEAD_DIM), F32))
    cache_spec = pl.BlockSpec((None, bt, rows, HEAD_DIM), lambda t: (l, t, 0, 0))
    new_spec = pl.BlockSpec((bt, SWA_KV_HEADS, HEAD_DIM), lambda t: (t, 0, 0))
    head_spec = pl.BlockSpec((bt, SWA_HEADS, HEAD_DIM), lambda t: (t, 0, 0))
    extra_k, spec_k = _chain(prev_k)
    extra_v, spec_v = _chain(prev_v)
    aliases = {6: 1, 7: 2} if extra_k else {}
    return pl.pallas_call(
        functools.partial(_swa_sample_kernel, bt=bt),
        out_shape=(jax.ShapeDtypeStruct((b, SWA_HEADS, HEAD_DIM), F32),
                   jax.ShapeDtypeStruct(cache_k.shape, F32), jax.ShapeDtypeStruct(cache_v.shape, F32)),
        grid=(b // bt,),
        in_specs=[pl.BlockSpec((None, SWA_HEADS, 1), lambda t: (l, 0, 0)),
                  head_spec, new_spec, new_spec, cache_spec, cache_spec, *spec_k, *spec_v],
        out_specs=(head_spec, cache_spec, cache_spec),
        input_output_aliases=aliases,
        compiler_params=_params(("parallel",), blocks),
        name="swa_sample",
    )(sinks.reshape(DEPTH, SWA_HEADS, 1), q, k_new, v_new, cache_k, cache_v, *extra_k, *extra_v)


def _cross_sample_kernel(q_ref, mk_ref, mv_ref, o_ref, *, bt):
    rows = MEM_HEADS * MEM_LEN
    srow = lax.broadcasted_iota(jnp.int32, (MEM_HEADS, rows), 0)
    scol = lax.broadcasted_iota(jnp.int32, (MEM_HEADS, rows), 1)
    own = (scol & (MEM_HEADS - 1)) == srow

    def body(ib, carry):
        qb = q_ref[ib].astype(BF16)
        s = jnp.where(own, _dot_nt(qb, mk_ref[ib].astype(BF16)) * ATT_SCALE, -jnp.inf)
        e = jnp.exp(s - jnp.max(s, axis=-1, keepdims=True))
        p = (e / jnp.sum(e, axis=-1, keepdims=True)).astype(BF16)
        o_ref[ib] = _dot(p, mv_ref[ib].astype(BF16))
        return carry

    lax.fori_loop(0, bt, body, 0, unroll=8)


def _cross_sample(q, cache_k, cache_v, l):
    b = q.shape[0]
    bt = 16
    rows = MEM_HEADS * MEM_LEN
    blocks = 2 * _nbytes((bt, 8, HEAD_DIM), F32) + 2 * _nbytes((bt, rows, HEAD_DIM), F32)
    cache_spec = pl.BlockSpec((None, bt, rows, HEAD_DIM), lambda t: (l, t, 0, 0))
    head_spec = pl.BlockSpec((bt, MEM_HEADS, HEAD_DIM), lambda t: (t, 0, 0))
    return pl.pallas_call(
        functools.partial(_cross_sample_kernel, bt=bt),
        out_shape=jax.ShapeDtypeStruct((b, MEM_HEADS, HEAD_DIM), F32),
        grid=(b // bt,),
        in_specs=[head_spec, cache_spec, cache_spec],
        out_specs=head_spec,
        compiler_params=_params(("parallel",), blocks),
        name="cross_sample",
    )(q, cache_k, cache_v)


def kernel(x_prompt, x_sample, mem_prompt, state_ret, cache_win_k, cache_win_v, cache_mem_k, cache_mem_v,
           g_mix, w_in, ret_gn, sinks, w_out, g_cross, g_mem, wq_c, wk_c, wv_c, wo_c,
           g_ffn, w_gate, w_up, w_down, g_final):
    seq = x_prompt.shape[1]
    nb = x_sample.shape[0]
    w_in_b, w_out_b = _to_bf16_tiles(w_in, IN_PROJ_TN), _to_bf16(w_out)
    wq_b, wk_b, wv_b, wo_b = (_to_bf16(w) for w in (wq_c, wk_c, wv_c, wo_c))
    wg_b, wu_b, wd_b = _to_bf16_tiles(w_gate, FFN_TF), _to_bf16_tiles(w_up, FFN_TF), _to_bf16(w_down)
    gain = lambda g: g.reshape(DEPTH, 1, g.shape[-1])
    g_mix, g_cross, g_mem, g_ffn, ret_gn = (gain(g) for g in (g_mix, g_cross, g_mem, g_ffn, ret_gn))

    cp, sp = (jnp.asarray(t) for t in _rope_tables(np.arange(seq)))
    cs, ss = (jnp.asarray(t) for t in _rope_tables(np.array([PAST_LEN])))
    dmask, xi, zeta = (jnp.asarray(t) for t in _decay_tables())
    gammas = jnp.asarray((1.0 - np.exp2(-5.0 - np.arange(RET_HEADS))).astype(np.float32))

    xp = x_prompt[0]
    xs = x_sample[:, 0]
    mem = mem_prompt[0]
    kvw = SWA_KV_HEADS * HEAD_DIM
    win_k = cache_win_k.reshape(DEPTH, nb, WINDOW * SWA_KV_HEADS, HEAD_DIM)
    win_v = cache_win_v.reshape(DEPTH, nb, WINDOW * SWA_KV_HEADS, HEAD_DIM)
    mem_k = cache_mem_k.reshape(DEPTH, nb, MEM_LEN * MEM_HEADS, HEAD_DIM)
    mem_v = cache_mem_v.reshape(DEPTH, nb, MEM_LEN * MEM_HEADS, HEAD_DIM)

    ret_p, wink_p, winv_p, memk_p, memv_p = [], [], [], [], []
    ret_s = wink_s = winv_s = None
    for l in range(DEPTH):
        proj = _in_proj(xp, g_mix, w_in_b, l)
        ro, r_last = _ret_prompt(proj, cp, sp, dmask, xi, zeta, ret_gn, l)
        so = _swa_prompt(proj, sinks, l)
        x1, qc = _out_proj(ro, so, xp, w_out_b, g_cross, wq_b, l)
        mk, mv = _mem_kv(mem, g_mem, wk_b, wv_b, l)
        x2 = _cross_prompt(qc, mk, mv, wo_b, x1, l)
        xp = _ffn(x2, g_ffn, wg_b, wu_b, wd_b, g_final, l)
        ret_p.append(r_last[None])
        tail = proj[seq - WINDOW:]
        wink_p.append(tail[:, COL_SK * LANES:COL_SK * LANES + kvw].reshape(1, WINDOW, SWA_KV_HEADS, HEAD_DIM))
        winv_p.append(tail[:, COL_SV * LANES:COL_SV * LANES + kvw].reshape(1, WINDOW, SWA_KV_HEADS, HEAD_DIM))
        memk_p.append(mk.reshape(1, MEM_LEN, MEM_HEADS, HEAD_DIM))
        memv_p.append(mv.reshape(1, MEM_LEN, MEM_HEADS, HEAD_DIM))

        proj_s = _in_proj(xs, g_mix, w_in_b, l)
        ro_s, ret_s = _ret_sample(proj_s, state_ret, ret_s, gammas, cs, ss, ret_gn, l)
        sq = proj_s[:, COL_SQ * LANES:COL_SQ * LANES + SWA_W].reshape(nb, SWA_HEADS, HEAD_DIM)
        sk = proj_s[:, COL_SK * LANES:COL_SK * LANES + kvw].reshape(nb, SWA_KV_HEADS, HEAD_DIM)
        sv = proj_s[:, COL_SV * LANES:COL_SV * LANES + kvw].reshape(nb, SWA_KV_HEADS, HEAD_DIM)
        so_s, wink_s, winv_s = _swa_sample(sq, sk, sv, win_k, win_v, wink_s, winv_s, sinks, l)
        x1_s, qc_s = _out_proj(ro_s, so_s.reshape(nb, SWA_W), xs, w_out_b, g_cross, wq_b, l)
        oc_s = _cross_sample(qc_s.reshape(nb, MEM_HEADS, HEAD_DIM), mem_k, mem_v, l)
        x2_s = _matmul_residual(oc_s.reshape(nb, MEM_W), wo_b, x1_s, l)
        xs = _ffn(x2_s, g_ffn, wg_b, wu_b, wd_b, g_final, l)

    cache_shape = (DEPTH, nb, WINDOW, SWA_KV_HEADS, HEAD_DIM)
    return (xp[None], xs[:, None],
            jnp.stack(ret_p), jnp.stack(wink_p), jnp.stack(winv_p), jnp.stack(memk_p), jnp.stack(memv_p),
            ret_s, wink_s.reshape(cache_shape), winv_s.reshape(cache_shape))
```
